```python
import math
import jax, jax.numpy as jnp
from jax import lax
import numpy as np

D_MODEL = 1024
BATCH = 2
SEQ = 8192
DEPTH = 4
DEC_BATCH = 32
DEC_SEQ = 8
PAST_LEN = 8192
PAGE_SIZE = 128

N_MIXERS = 2
N_SSM_LAYERS = (DEPTH + 1) // 2
N_ATT_LAYERS = DEPTH // 2

SSM_EXPAND = 2
D_INNER = SSM_EXPAND * D_MODEL
SSM_HEAD_DIM = 64
N_SSM_HEADS = D_INNER // SSM_HEAD_DIM
SSM_GROUPS = 4
HEADS_PER_GROUP = N_SSM_HEADS // SSM_GROUPS
D_STATE = 128
CONV_W = 4
CONV_DIM = D_INNER + 2 * SSM_GROUPS * D_STATE
SSM_IN_DIM = D_INNER + CONV_DIM + N_SSM_HEADS
SSM_CHUNK = 128

ATT_HEAD_DIM = 64
N_ATT_HEADS = D_MODEL // ATT_HEAD_DIM
N_KV_HEADS = 4
KV_GROUP = N_ATT_HEADS // N_KV_HEADS
N_IDX_HEADS = 8
IDX_DIM = 64
TOPK_MAX = 256
Q_BLOCK = 128
ATT_SPLITS = [N_ATT_HEADS * ATT_HEAD_DIM, N_KV_HEADS * ATT_HEAD_DIM, N_KV_HEADS * ATT_HEAD_DIM, N_IDX_HEADS * IDX_DIM, IDX_DIM, N_IDX_HEADS]
ATT_IN_DIM = sum(ATT_SPLITS)
ROPE_THETA = 10000.0

N_MEM = 256
N_MEM_HEADS = 4
MEM_HEAD_DIM = D_MODEL // N_MEM_HEADS

FFN_HIDDEN = -(-8 * D_MODEL // (3 * 256)) * 256

DEEPNORM_ALPHA = (2 * DEPTH) ** 0.25
DEEPNORM_BETA = (8 * DEPTH) ** -0.25
LN_EPS = 1e-5

kernel_name = 'hybrid_ssd_dsa_memxattn_deepnorm_step'

F32 = jnp.float32


def layer_norm(x, g, b):
    xf = x.astype(F32)
    mu = jnp.mean(xf, -1, keepdims=True)
    var = jnp.mean(jnp.square(xf - mu), -1, keepdims=True)
    return ((xf - mu) * lax.rsqrt(var + LN_EPS) * g.astype(F32) + b.astype(F32)).astype(x.dtype)


def rope(x, pos):
    d = x.shape[-1]
    inv_freq = ROPE_THETA ** (-jnp.arange(0, d, 2, dtype=F32) / d)
    ang = pos.astype(F32)[:, None] * inv_freq[None, :]
    cos = jnp.cos(ang)[:, None, :]
    sin = jnp.sin(ang)[:, None, :]
    x1, x2 = jnp.split(x.astype(F32), 2, axis=-1)
    return jnp.concatenate([x1 * cos - x2 * sin, x1 * sin + x2 * cos], -1).astype(x.dtype)


def swiglu(x, w_in, w_out):
    a, b = jnp.split(x @ w_in, 2, axis=-1)
    return (jax.nn.silu(a) * b) @ w_out


def mem_attend(x, w_q, w_o, mk, mv):
    bsz, t = x.shape[:2]
    q = (x @ w_q).reshape(bsz, t, N_MEM_HEADS, MEM_HEAD_DIM)
    logits = jnp.einsum('bthd,bmhd->bhtm', q.astype(F32), mk.astype(F32)) * MEM_HEAD_DIM ** -0.5
    p = jax.nn.softmax(logits, axis=-1)
    o = jnp.einsum('bhtm,bmhd->bthd', p, mv.astype(F32)).reshape(bsz, t, D_MODEL)
    return o.astype(x.dtype) @ w_o


def ssd_chunked(xdt, a, bm, cm, h0, chunk):
    bsz, L, G, R, P = xdt.shape
    N = bm.shape[-1]
    nc = L // chunk
    x = xdt.reshape(bsz, nc, chunk, G, R, P)
    a = a.reshape(bsz, nc, chunk, G, R)
    bm = bm.reshape(bsz, nc, chunk, G, N)
    cm = cm.reshape(bsz, nc, chunk, G, N)
    acum = jnp.cumsum(a, axis=2)
    acum_t = jnp.moveaxis(acum, 2, -1)
    seg = acum_t[..., :, None] - acum_t[..., None, :]
    causal = jnp.tril(jnp.ones((chunk, chunk), dtype=bool))
    lmat = jnp.exp(jnp.where(causal, seg, -jnp.inf))
    cb = jnp.einsum('bclgn,bcsgn->bcgls', cm, bm)
    y_diag = jnp.einsum('bcgrls,bcsgrp->bclgrp', cb[:, :, :, None] * lmat, x)
    decay = jnp.exp(acum[:, :, -1:] - acum)
    states = jnp.einsum('bclgn,bclgrp->bcgrpn', bm, x * decay[..., None])
    chunk_decay = jnp.exp(acum[:, :, -1])

    def step(h, inp):
        st, dec = inp
        return h * dec[..., None, None] + st, h

    h_last, h_prev = lax.scan(step, h0, (jnp.moveaxis(states, 1, 0), jnp.moveaxis(chunk_decay, 1, 0)))
    h_prev = jnp.moveaxis(h_prev, 0, 1)
    y_off = jnp.einsum('bclgn,bcgrpn->bclgrp', cm, h_prev) * jnp.exp(acum)[..., None]
    return (y_diag + y_off).reshape(bsz, L, G, R, P), h_last


def mamba_mixer(x, w_in, conv_w, conv_b, dt_bias, a_log, d_skip, norm_w, w_out, conv_state, ssm_state):
    bsz, t = x.shape[:2]
    z, xbc, dt = jnp.split(x @ w_in, [D_INNER, D_INNER + CONV_DIM], axis=-1)
    xbc_ext = jnp.concatenate([conv_state.astype(xbc.dtype), xbc], axis=1)
    conv = conv_b
    for w in range(CONV_W):
        conv = conv + xbc_ext[:, w:w + t] * conv_w[w]
    new_conv = xbc_ext[:, t:]
    xbc = jax.nn.silu(conv).astype(F32)
    xs, bm, cm = jnp.split(xbc, [D_INNER, D_INNER + SSM_GROUPS * D_STATE], axis=-1)
    xs = xs.reshape(bsz, t, SSM_GROUPS, HEADS_PER_GROUP, SSM_HEAD_DIM)
    bm = bm.reshape(bsz, t, SSM_GROUPS, D_STATE)
    cm = cm.reshape(bsz, t, SSM_GROUPS, D_STATE)
    dt = jax.nn.softplus(dt.astype(F32) + dt_bias.astype(F32)).reshape(bsz, t, SSM_GROUPS, HEADS_PER_GROUP)
    a = -jnp.exp(a_log.astype(F32)).reshape(SSM_GROUPS, HEADS_PER_GROUP)
    chunk = SSM_CHUNK if t % SSM_CHUNK == 0 else t
    h0 = ssm_state.astype(F32).reshape(bsz, SSM_GROUPS, HEADS_PER_GROUP, SSM_HEAD_DIM, D_STATE)
    y, h_new = ssd_chunked(xs * dt[..., None], dt * a, bm, cm, h0, chunk)
    y = (y + xs * d_skip.astype(F32).reshape(SSM_GROUPS, HEADS_PER_GROUP)[:, :, None]).reshape(bsz, t, D_INNER)
    yg = y * jax.nn.silu(z.astype(F32))
    yg = yg * lax.rsqrt(jnp.mean(jnp.square(yg), -1, keepdims=True) + LN_EPS) * norm_w.astype(F32)
    out = yg.astype(x.dtype) @ w_out
    return out, new_conv, h_new.reshape(bsz, N_SSM_HEADS, SSM_HEAD_DIM, D_STATE).astype(ssm_state.dtype)


def dsa_project(x, w_in, pos):
    bsz, t = x.shape[:2]
    cuts = list(np.cumsum(ATT_SPLITS)[:-1])
    q, k, v, qi, ki, wi = jnp.split(x @ w_in, cuts, axis=-1)
    q = rope(q.reshape(bsz, t, N_ATT_HEADS, ATT_HEAD_DIM), pos)
    k = rope(k.reshape(bsz, t, N_KV_HEADS, ATT_HEAD_DIM), pos)
    v = v.reshape(bsz, t, N_KV_HEADS, ATT_HEAD_DIM)
    qi = rope(qi.reshape(bsz, t, N_IDX_HEADS, IDX_DIM), pos)
    ki = rope(ki[:, :, None, :], pos)[:, :, 0, :]
    return q, k, v, qi, ki, wi


def dsa_select_attend(q, qi, wi, ki_all, q_pos, gather_kv, topk):
    bsz, t = q.shape[:2]
    L = ki_all.shape[1]
    s = jnp.einsum('bthd,bsd->bths', qi.astype(F32), ki_all.astype(F32)) * IDX_DIM ** -0.5
    score = jnp.einsum('bth,bths->bts', wi.astype(F32) * N_IDX_HEADS ** -0.5, jax.nn.relu(s))
    admissible = jnp.arange(L, dtype=jnp.int32)[None, :] <= q_pos[:, None]
    score = jnp.where(admissible[None], score, -jnp.inf)
    _, idx = lax.top_k(score, topk)
    valid = idx <= q_pos[None, :, None]
    k_sel, v_sel = gather_kv(idx)
    qg = q.reshape(bsz, t, N_KV_HEADS, KV_GROUP, ATT_HEAD_DIM)
    logits = jnp.einsum('btjgd,btsjd->btjgs', qg.astype(F32), k_sel.astype(F32)) * ATT_HEAD_DIM ** -0.5
    logits = jnp.where(valid[:, :, None, None, :], logits, -jnp.inf)
    p = jax.nn.softmax(logits, axis=-1)
    o = jnp.einsum('btjgs,btsjd->btjgd', p, v_sel.astype(F32))
    return o.reshape(bsz, t, N_ATT_HEADS * ATT_HEAD_DIM).astype(q.dtype)


def take_rows(a, i):
    return jax.vmap(lambda ab, ib: ab[ib])(a, i)


def dsa_prompt(x, w_in, w_out, pos):
    bsz, t = x.shape[:2]
    q, k, v, qi, ki, wi = dsa_project(x, w_in, pos)
    topk = min(TOPK_MAX, t // 4)
    qb = Q_BLOCK if t % Q_BLOCK == 0 else t
    nb = t // qb

    def gather(idx):
        return take_rows(k, idx), take_rows(v, idx)

    def blk(args):
        q_b, qi_b, wi_b, p_b = args
        return dsa_select_attend(q_b, qi_b, wi_b, ki, p_b, gather, topk)

    def to_blocks(a):
        return jnp.swapaxes(a.reshape((bsz, nb, qb) + a.shape[2:]), 0, 1)

    out = lax.map(blk, (to_blocks(q), to_blocks(qi), to_blocks(wi), pos.reshape(nb, qb)))
    out = jnp.swapaxes(out, 0, 1).reshape(bsz, t, D_MODEL)
    return out @ w_out, k, v, ki


def dsa_sample(x, w_in, w_out, pos, pool_k, pool_v, pool_ki, page_table):
    bsz, t = x.shape[:2]
    q, k, v, qi, ki, wi = dsa_project(x, w_in, pos)
    past = page_table.shape[1] * PAGE_SIZE
    ki_past = pool_ki[page_table].reshape(bsz, past, IDX_DIM)
    ki_all = jnp.concatenate([ki_past.astype(ki.dtype), ki], axis=1)
    topk = min(TOPK_MAX, (past + t) // 4)

    def gather(idx):
        in_past = idx < past
        pidx = jnp.minimum(idx, past - 1)
        phys = jnp.take_along_axis(page_table, (pidx // PAGE_SIZE).reshape(bsz, -1), axis=1).reshape(idx.shape)
        off = pidx % PAGE_SIZE
        nidx = jnp.clip(idx - past, 0, t - 1)
        sel = in_past[..., None, None]
        k_g = jnp.where(sel, pool_k[phys, off].astype(k.dtype), take_rows(k, nidx))
        v_g = jnp.where(sel, pool_v[phys, off].astype(v.dtype), take_rows(v, nidx))
        return k_g, v_g

    out = dsa_select_attend(q, qi, wi, ki_all, pos, gather, topk)
    return out @ w_out, k, v, ki


def run_trunk(x, mem_k, mem_v, conv0, ssm0, attend, p):
    conv_out, ssm_out, k_out, v_out, ki_out = [], [], [], [], []
    for i in range(DEPTH):
        j = i // N_MIXERS
        if i % N_MIXERS == 0:
            h, cs, ss = mamba_mixer(x, p['w_ssm_in'][j], p['ssm_conv_w'][j], p['ssm_conv_b'][j], p['ssm_dt_bias'][j],
                                    p['ssm_a_log'][j], p['ssm_d'][j], p['ssm_norm_w'][j], p['w_ssm_out'][j],
                                    conv0[j], ssm0[j])
            conv_out.append(cs)
            ssm_out.append(ss)
        else:
            h, k, v, ki = attend(j, x)
            k_out.append(k)
            v_out.append(v)
            ki_out.append(ki)
        x = layer_norm(DEEPNORM_ALPHA * x + h, p['ln_g'][i, 0], p['ln_b'][i, 0])
        x = layer_norm(DEEPNORM_ALPHA * x + mem_attend(x, p['w_mem_q'][i], p['w_mem_out'][i], mem_k[i], mem_v[i]),
                       p['ln_g'][i, 1], p['ln_b'][i, 1])
        x = layer_norm(DEEPNORM_ALPHA * x + swiglu(x, p['w_ffn_in'][i], p['w_ffn_out'][i]), p['ln_g'][i, 2], p['ln_b'][i, 2])
    return x, jnp.stack(k_out), jnp.stack(v_out), jnp.stack(ki_out), jnp.stack(conv_out), jnp.stack(ssm_out)


def setup_inputs(seed: int = 0) -> dict:
    key = jax.random.key(seed)
    ks = iter(jax.random.split(key, 40))

    def nrm(shape, scale):
        return jax.random.normal(next(ks), shape, F32) * scale

    n_pages = PAST_LEN // PAGE_SIZE
    n_used = DEC_BATCH * n_pages
    n_pool = n_used + max(1, n_used // 4)
    page_table = jax.random.permutation(next(ks), n_pool)[:n_used].reshape(DEC_BATCH, n_pages).astype(jnp.int32)
    dt0 = jnp.exp(jax.random.uniform(next(ks), (N_SSM_LAYERS, N_SSM_HEADS), F32, math.log(1e-3), math.log(1e-1)))
    dt_bias = dt0 + jnp.log(-jnp.expm1(-dt0))
    a_log = jnp.log(jax.random.uniform(next(ks), (N_SSM_LAYERS, N_SSM_HEADS), F32, 1.0, 16.0))
    d_in = D_MODEL ** -0.5
    return {
        'x_prompt': nrm((BATCH, SEQ, D_MODEL), 1.0),
        'x_sample': nrm((DEC_BATCH, DEC_SEQ, D_MODEL), 1.0),
        'cache_k': nrm((N_ATT_LAYERS, n_pool, PAGE_SIZE, N_KV_HEADS, ATT_HEAD_DIM), 1.0),
        'cache_v': nrm((N_ATT_LAYERS, n_pool, PAGE_SIZE, N_KV_HEADS, ATT_HEAD_DIM), 1.0),
        'cache_kidx': nrm((N_ATT_LAYERS, n_pool, PAGE_SIZE, IDX_DIM), 1.0),
        'cache_mem_k': nrm((DEPTH, DEC_BATCH, N_MEM, N_MEM_HEADS, MEM_HEAD_DIM), 1.0),
        'cache_mem_v': nrm((DEPTH, DEC_BATCH, N_MEM, N_MEM_HEADS, MEM_HEAD_DIM), 1.0),
        'state_conv': nrm((N_SSM_LAYERS, DEC_BATCH, CONV_W - 1, CONV_DIM), 1.0),
        'state_ssm': nrm((N_SSM_LAYERS, DEC_BATCH, N_SSM_HEADS, SSM_HEAD_DIM, D_STATE), 0.5),
        'page_table': page_table,
        'mem_prompt': nrm((BATCH, N_MEM, D_MODEL), 1.0),
        'w_ssm_in': nrm((N_SSM_LAYERS, D_MODEL, SSM_IN_DIM), d_in),
        'ssm_conv_w': nrm((N_SSM_LAYERS, CONV_W, CONV_DIM), CONV_W ** -0.5),
        'ssm_conv_b': nrm((N_SSM_LAYERS, CONV_DIM), 0.01),
        'ssm_dt_bias': dt_bias,
        'ssm_a_log': a_log,
        'ssm_d': 1.0 + nrm((N_SSM_LAYERS, N_SSM_HEADS), 0.02),
        'ssm_norm_w': 1.0 + nrm((N_SSM_LAYERS, D_INNER), 0.02),
        'w_ssm_out': nrm((N_SSM_LAYERS, D_INNER, D_MODEL), DEEPNORM_BETA * D_INNER ** -0.5),
        'w_att_in': nrm((N_ATT_LAYERS, D_MODEL, ATT_IN_DIM), d_in),
        'w_att_out': nrm((N_ATT_LAYERS, D_MODEL, D_MODEL), DEEPNORM_BETA * d_in),
        'w_mem_q': nrm((DEPTH, D_MODEL, D_MODEL), d_in),
        'w_mem_kv': nrm((DEPTH, D_MODEL, 2 * D_MODEL), d_in),
        'w_mem_out': nrm((DEPTH, D_MODEL, D_MODEL), DEEPNORM_BETA * d_in),
        'w_ffn_in': nrm((DEPTH, D_MODEL, 2 * FFN_HIDDEN), d_in),
        'w_ffn_out': nrm((DEPTH, FFN_HIDDEN, D_MODEL), DEEPNORM_BETA * FFN_HIDDEN ** -0.5),
        'ln_g': 1.0 + nrm((DEPTH, 3, D_MODEL), 0.02),
        'ln_b': nrm((DEPTH, 3, D_MODEL), 0.02),
    }


def reference(x_prompt, x_sample, cache_k, cache_v, cache_kidx, cache_mem_k, cache_mem_v, state_conv, state_ssm,
              page_table, mem_prompt, w_ssm_in, ssm_conv_w, ssm_conv_b, ssm_dt_bias, ssm_a_log, ssm_d, ssm_norm_w,
              w_ssm_out, w_att_in, w_att_out, w_mem_q, w_mem_kv, w_mem_out, w_ffn_in, w_ffn_out, ln_g, ln_b):
    params = {'w_ssm_in': w_ssm_in, 'ssm_conv_w': ssm_conv_w, 'ssm_conv_b': ssm_conv_b, 'ssm_dt_bias': ssm_dt_bias,
              'ssm_a_log': ssm_a_log, 'ssm_d': ssm_d, 'ssm_norm_w': ssm_norm_w, 'w_ssm_out': w_ssm_out,
              'w_mem_q': w_mem_q, 'w_mem_out': w_mem_out, 'w_ffn_in': w_ffn_in, 'w_ffn_out': w_ffn_out,
              'ln_g': ln_g, 'ln_b': ln_b}
    bp, seq = x_prompt.shape[:2]
    pos_p = jnp.arange(seq, dtype=jnp.int32)
    mem_kv = jnp.einsum('bmd,lde->lbme', mem_prompt, w_mem_kv)
    memk_flat, memv_flat = jnp.split(mem_kv, 2, axis=-1)
    memk_prompt = memk_flat.reshape(DEPTH, bp, N_MEM, N_MEM_HEADS, MEM_HEAD_DIM)
    memv_prompt = memv_flat.reshape(DEPTH, bp, N_MEM, N_MEM_HEADS, MEM_HEAD_DIM)
    conv0 = jnp.zeros((N_SSM_LAYERS, bp, CONV_W - 1, CONV_DIM), x_prompt.dtype)
    ssm0 = jnp.zeros((N_SSM_LAYERS, bp, N_SSM_HEADS, SSM_HEAD_DIM, D_STATE), x_prompt.dtype)
    attend_p = lambda j, x: dsa_prompt(x, w_att_in[j], w_att_out[j], pos_p)
    y_prompt, k_prompt, v_prompt, kidx_prompt, conv_prompt, ssm_prompt = run_trunk(
        x_prompt, memk_prompt, memv_prompt, conv0, ssm0, attend_p, params)
    past = page_table.shape[1] * PAGE_SIZE
    pos_s = past + jnp.arange(x_sample.shape[1], dtype=jnp.int32)
    attend_s = lambda j, x: dsa_sample(x, w_att_in[j], w_att_out[j], pos_s, cache_k[j], cache_v[j], cache_kidx[j], page_table)
    y_sample, k_sample, v_sample, kidx_sample, conv_sample, ssm_sample = run_trunk(
        x_sample, cache_mem_k, cache_mem_v, state_conv, state_ssm, attend_s, params)
    return (y_prompt, y_sample, k_prompt, v_prompt, kidx_prompt, conv_prompt, ssm_prompt, memk_prompt, memv_prompt,
            k_sample, v_sample, kidx_sample, conv_sample, ssm_sample)
```

```python
import functools
import math

import jax
import jax.numpy as jnp
import numpy as np
from jax import lax
from jax.experimental import pallas as pl
from jax.experimental.pallas import tpu as pltpu

F32 = jnp.float32
BF16 = jnp.bfloat16

DEPTH = 4
N_MIXERS = 2
D_MODEL = 1024
PAGE_SIZE = 128

D_INNER = 2048
SSM_HEAD_DIM = 64
N_SSM_HEADS = 32
SSM_GROUPS = 4
HEADS_PER_GROUP = 8
D_STATE = 128
CONV_W = 4
CONV_DIM = D_INNER + 2 * SSM_GROUPS * D_STATE
SSM_CHUNK = 128

ATT_HEAD_DIM = 64
N_ATT_HEADS = 16
N_KV_HEADS = 4
KV_GROUP = 4
N_IDX_HEADS = 8
IDX_DIM = 64
TOPK_MAX = 256
Q_BLOCK = 128
ATT_SPLITS = [1024, 256, 256, 512, 64, 8]
ROPE_THETA = 10000.0

N_MEM = 256
N_MEM_HEADS = 4
MEM_HEAD_DIM = 256
FFN_HIDDEN = 2816

DEEPNORM_ALPHA = (2 * DEPTH) ** 0.25
LN_EPS = 1e-5

VMEM_LIMIT_BYTES = 48 * 1024 * 1024


def _layer_norm_rows(y, g, b):
    mu = jnp.mean(y, axis=-1, keepdims=True)
    d = y - mu
    var = jnp.mean(d * d, axis=-1, keepdims=True)
    return d * lax.rsqrt(var + LN_EPS) * g + b


def _ffn_ln_kernel(x_ref, wa_ref, wb_ref, wo_ref, g_ref, b_ref, o_ref, xb_ref, acc_ref):
    k = pl.program_id(1)

    @pl.when(k == 0)
    def _():
        xb_ref[...] = x_ref[...].astype(BF16)
        acc_ref[...] = jnp.zeros_like(acc_ref)

    xb = xb_ref[...]
    a = jnp.dot(xb, wa_ref[...], preferred_element_type=F32)
    b = jnp.dot(xb, wb_ref[...], preferred_element_type=F32)
    h = (a * jax.nn.sigmoid(a)) * b
    acc_ref[...] += jnp.dot(h.astype(BF16), wo_ref[...], preferred_element_type=F32)

    @pl.when(k == pl.num_programs(1) - 1)
    def _():
        y = DEEPNORM_ALPHA * x_ref[...] + acc_ref[...]
        o_ref[...] = _layer_norm_rows(y, g_ref[...], b_ref[...])


def ffn_ln(x, w_in, w_out, g, b, *, tm, th=256):
    m, d = x.shape
    hidden = w_out.shape[0]
    nh = hidden // th
    return pl.pallas_call(
        _ffn_ln_kernel,
        grid=(m // tm, nh),
        in_specs=[
            pl.BlockSpec((tm, d), lambda i, k: (i, 0)),
            pl.BlockSpec((d, th), lambda i, k: (0, k)),
            pl.BlockSpec((d, th), lambda i, k: (0, k + nh)),
            pl.BlockSpec((th, d), lambda i, k: (k, 0)),
            pl.BlockSpec((1, d), lambda i, k: (0, 0)),
            pl.BlockSpec((1, d), lambda i, k: (0, 0)),
        ],
        out_specs=pl.BlockSpec((tm, d), lambda i, k: (i, 0)),
        out_shape=jax.ShapeDtypeStruct((m, d), F32),
        scratch_shapes=[pltpu.VMEM((tm, d), BF16), pltpu.VMEM((tm, d), F32)],
        compiler_params=pltpu.CompilerParams(
            dimension_semantics=("parallel", "arbitrary"), vmem_limit_bytes=VMEM_LIMIT_BYTES),
        name="ffn_ln",
    )(x, w_in, w_in, w_out, g.reshape(1, d), b.reshape(1, d))


def layer_norm(x, g, b):
    xf = x.astype(F32)
    mu = jnp.mean(xf, -1, keepdims=True)
    var = jnp.mean(jnp.square(xf - mu), -1, keepdims=True)
    return ((xf - mu) * lax.rsqrt(var + LN_EPS) * g.astype(F32) + b.astype(F32)).astype(x.dtype)


def rope(x, pos):
    d = x.shape[-1]
    inv_freq = ROPE_THETA ** (-jnp.arange(0, d, 2, dtype=F32) / d)
    ang = pos.astype(F32)[:, None] * inv_freq[None, :]
    cos = jnp.cos(ang)[:, None, :]
    sin = jnp.sin(ang)[:, None, :]
    x1, x2 = jnp.split(x.astype(F32), 2, axis=-1)
    return jnp.concatenate([x1 * cos - x2 * sin, x1 * sin + x2 * cos], -1).astype(x.dtype)


def mem_attend(x, w_q, w_o, mk, mv):
    bsz, t = x.shape[:2]
    q = (x @ w_q).reshape(bsz, t, N_MEM_HEADS, MEM_HEAD_DIM)
    logits = jnp.einsum('bthd,bmhd->bhtm', q.astype(F32), mk.astype(F32)) * MEM_HEAD_DIM ** -0.5
    p = jax.nn.softmax(logits, axis=-1)
    o = jnp.einsum('bhtm,bmhd->bthd', p, mv.astype(F32)).reshape(bsz, t, D_MODEL)
    return o.astype(x.dtype) @ w_o


def ssd_chunked(xdt, a, bm, cm, h0, chunk):
    bsz, L, G, R, P = xdt.shape
    N = bm.shape[-1]
    nc = L // chunk
    x = xdt.reshape(bsz, nc, chunk, G, R, P)
    a = a.reshape(bsz, nc, chunk, G, R)
    bm = bm.reshape(bsz, nc, chunk, G, N)
    cm = cm.reshape(bsz, nc, chunk, G, N)
    acum = jnp.cumsum(a, axis=2)
    acum_t = jnp.moveaxis(acum, 2, -1)
    seg = acum_t[..., :, None] - acum_t[..., None, :]
    causal = jnp.tril(jnp.ones((chunk, chunk), dtype=bool))
    lmat = jnp.exp(jnp.where(causal, seg, -jnp.inf))
    cb = jnp.einsum('bclgn,bcsgn->bcgls', cm, bm)
    y_diag = jnp.einsum('bcgrls,bcsgrp->bclgrp', cb[:, :, :, None] * lmat, x)
    decay = jnp.exp(acum[:, :, -1:] - acum)
    states = jnp.einsum('bclgn,bclgrp->bcgrpn', bm, x * decay[..., None])
    chunk_decay = jnp.exp(acum[:, :, -1])

    def step(h, inp):
        st, dec = inp
        return h * dec[..., None, None] + st, h

    h_last, h_prev = lax.scan(step, h0, (jnp.moveaxis(states, 1, 0), jnp.moveaxis(chunk_decay, 1, 0)))
    h_prev = jnp.moveaxis(h_prev, 0, 1)
    y_off = jnp.einsum('bclgn,bcgrpn->bclgrp', cm, h_prev) * jnp.exp(acum)[..., None]
    return (y_diag + y_off).reshape(bsz, L, G, R, P), h_last


def mamba_mixer(x, w_in, conv_w, conv_b, dt_bias, a_log, d_skip, norm_w, w_out, conv_state, ssm_state):
    bsz, t = x.shape[:2]
    z, xbc, dt = jnp.split(x @ w_in, [D_INNER, D_INNER + CONV_DIM], axis=-1)
    xbc_ext = jnp.concatenate([conv_state.astype(xbc.dtype), xbc], axis=1)
    conv = conv_b
    for w in range(CONV_W):
        conv = conv + xbc_ext[:, w:w + t] * conv_w[w]
    new_conv = xbc_ext[:, t:]
    xbc = jax.nn.silu(conv).astype(F32)
    xs, bm, cm = jnp.split(xbc, [D_INNER, D_INNER + SSM_GROUPS * D_STATE], axis=-1)
    xs = xs.reshape(bsz, t, SSM_GROUPS, HEADS_PER_GROUP, SSM_HEAD_DIM)
    bm = bm.reshape(bsz, t, SSM_GROUPS, D_STATE)
    cm = cm.reshape(bsz, t, SSM_GROUPS, D_STATE)
    dt = jax.nn.softplus(dt.astype(F32) + dt_bias.astype(F32)).reshape(bsz, t, SSM_GROUPS, HEADS_PER_GROUP)
    a = -jnp.exp(a_log.astype(F32)).reshape(SSM_GROUPS, HEADS_PER_GROUP)
    chunk = SSM_CHUNK if t % SSM_CHUNK == 0 else t
    h0 = ssm_state.astype(F32).reshape(bsz, SSM_GROUPS, HEADS_PER_GROUP, SSM_HEAD_DIM, D_STATE)
    y, h_new = ssd_chunked(xs * dt[..., None], dt * a, bm, cm, h0, chunk)
    y = (y + xs * d_skip.astype(F32).reshape(SSM_GROUPS, HEADS_PER_GROUP)[:, :, None]).reshape(bsz, t, D_INNER)
    yg = y * jax.nn.silu(z.astype(F32))
    yg = yg * lax.rsqrt(jnp.mean(jnp.square(yg), -1, keepdims=True) + LN_EPS) * norm_w.astype(F32)
    out = yg.astype(x.dtype) @ w_out
    return out, new_conv, h_new.reshape(bsz, N_SSM_HEADS, SSM_HEAD_DIM, D_STATE).astype(ssm_state.dtype)


def dsa_project(x, w_in, pos):
    bsz, t = x.shape[:2]
    cuts = list(np.cumsum(ATT_SPLITS)[:-1])
    q, k, v, qi, ki, wi = jnp.split(x @ w_in, cuts, axis=-1)
    q = rope(q.reshape(bsz, t, N_ATT_HEADS, ATT_HEAD_DIM), pos)
    k = rope(k.reshape(bsz, t, N_KV_HEADS, ATT_HEAD_DIM), pos)
    v = v.reshape(bsz, t, N_KV_HEADS, ATT_HEAD_DIM)
    qi = rope(qi.reshape(bsz, t, N_IDX_HEADS, IDX_DIM), pos)
    ki = rope(ki[:, :, None, :], pos)[:, :, 0, :]
    return q, k, v, qi, ki, wi


def dsa_select_attend(q, qi, wi, ki_all, q_pos, gather_kv, topk):
    bsz, t = q.shape[:2]
    L = ki_all.shape[1]
    s = jnp.einsum('bthd,bsd->bths', qi.astype(F32), ki_all.astype(F32)) * IDX_DIM ** -0.5
    score = jnp.einsum('bth,bths->bts', wi.astype(F32) * N_IDX_HEADS ** -0.5, jax.nn.relu(s))
    admissible = jnp.arange(L, dtype=jnp.int32)[None, :] <= q_pos[:, None]
    score = jnp.where(admissible[None], score, -jnp.inf)
    _, idx = lax.top_k(score, topk)
    valid = idx <= q_pos[None, :, None]
    k_sel, v_sel = gather_kv(idx)
    qg = q.reshape(bsz, t, N_KV_HEADS, KV_GROUP, ATT_HEAD_DIM)
    logits = jnp.einsum('btjgd,btsjd->btjgs', qg.astype(F32), k_sel.astype(F32)) * ATT_HEAD_DIM ** -0.5
    logits = jnp.where(valid[:, :, None, None, :], logits, -jnp.inf)
    p = jax.nn.softmax(logits, axis=-1)
    o = jnp.einsum('btjgs,btsjd->btjgd', p, v_sel.astype(F32))
    return o.reshape(bsz, t, N_ATT_HEADS * ATT_HEAD_DIM).astype(q.dtype)


def take_rows(a, i):
    return jax.vmap(lambda ab, ib: ab[ib])(a, i)


def dsa_prompt(x, w_in, w_out, pos):
    bsz, t = x.shape[:2]
    q, k, v, qi, ki, wi = dsa_project(x, w_in, pos)
    topk = min(TOPK_MAX, t // 4)
    qb = Q_BLOCK if t % Q_BLOCK == 0 else t
    nb = t // qb

    def gather(idx):
        return take_rows(k, idx), take_rows(v, idx)

    def blk(args):
        q_b, qi_b, wi_b, p_b = args
        return dsa_select_attend(q_b, qi_b, wi_b, ki, p_b, gather, topk)

    def to_blocks(a):
        return jnp.swapaxes(a.reshape((bsz, nb, qb) + a.shape[2:]), 0, 1)

    out = lax.map(blk, (to_blocks(q), to_blocks(qi), to_blocks(wi), pos.reshape(nb, qb)))
    out = jnp.swapaxes(out, 0, 1).reshape(bsz, t, D_MODEL)
    return out @ w_out, k, v, ki


def dsa_sample(x, w_in, w_out, pos, pool_k, pool_v, pool_ki, page_table):
    bsz, t = x.shape[:2]
    q, k, v, qi, ki, wi = dsa_project(x, w_in, pos)
    past = page_table.shape[1] * PAGE_SIZE
    ki_past = pool_ki[page_table].reshape(bsz, past, IDX_DIM)
    ki_all = jnp.concatenate([ki_past.astype(ki.dtype), ki], axis=1)
    topk = min(TOPK_MAX, (past + t) // 4)

    def gather(idx):
        in_past = idx < past
        pidx = jnp.minimum(idx, past - 1)
        phys = jnp.take_along_axis(page_table, (pidx // PAGE_SIZE).reshape(bsz, -1), axis=1).reshape(idx.shape)
        off = pidx % PAGE_SIZE
        nidx = jnp.clip(idx - past, 0, t - 1)
        sel = in_past[..., None, None]
        k_g = jnp.where(sel, pool_k[phys, off].astype(k.dtype), take_rows(k, nidx))
        v_g = jnp.where(sel, pool_v[phys, off].astype(v.dtype), take_rows(v, nidx))
        return k_g, v_g

    out = dsa_select_attend(q, qi, wi, ki_all, pos, gather, topk)
    return out @ w_out, k, v, ki


def run_trunk(x, mem_k, mem_v, conv0, ssm0, attend, p, tm):
    conv_out, ssm_out, k_out, v_out, ki_out = [], [], [], [], []
    bsz, t = x.shape[:2]
    for i in range(DEPTH):
        j = i // N_MIXERS
        if i % N_MIXERS == 0:
            h, cs, ss = mamba_mixer(x, p['w_ssm_in'][j], p['ssm_conv_w'][j], p['ssm_conv_b'][j], p['ssm_dt_bias'][j],
                                    p['ssm_a_log'][j], p['ssm_d'][j], p['ssm_norm_w'][j], p['w_ssm_out'][j],
                                    conv0[j], ssm0[j])
            conv_out.append(cs)
            ssm_out.append(ss)
        else:
            h, k, v, ki = attend(j, x)
            k_out.append(k)
            v_out.append(v)
            ki_out.append(ki)
        x = layer_norm(DEEPNORM_ALPHA * x + h, p['ln_g'][i, 0], p['ln_b'][i, 0])
        x = layer_norm(DEEPNORM_ALPHA * x + mem_attend(x, p['w_mem_q'][i], p['w_mem_out'][i], mem_k[i], mem_v[i]),
                       p['ln_g'][i, 1], p['ln_b'][i, 1])
        x = ffn_ln(x.reshape(bsz * t, D_MODEL), p['w_ffn_in_bf'][i], p['w_ffn_out_bf'][i],
                   p['ln_g'][i, 2], p['ln_b'][i, 2], tm=tm).reshape(bsz, t, D_MODEL)
    return x, jnp.stack(k_out), jnp.stack(v_out), jnp.stack(ki_out), jnp.stack(conv_out), jnp.stack(ssm_out)


def kernel(x_prompt, x_sample, cache_k, cache_v, cache_kidx, cache_mem_k, cache_mem_v, state_conv, state_ssm,
           page_table, mem_prompt, w_ssm_in, ssm_conv_w, ssm_conv_b, ssm_dt_bias, ssm_a_log, ssm_d, ssm_norm_w,
           w_ssm_out, w_att_in, w_att_out, w_mem_q, w_mem_kv, w_mem_out, w_ffn_in, w_ffn_out, ln_g, ln_b):
    params = {'w_ssm_in': w_ssm_in, 'ssm_conv_w': ssm_conv_w, 'ssm_conv_b': ssm_conv_b, 'ssm_dt_bias': ssm_dt_bias,
              'ssm_a_log': ssm_a_log, 'ssm_d': ssm_d, 'ssm_norm_w': ssm_norm_w, 'w_ssm_out': w_ssm_out,
              'w_mem_q': w_mem_q, 'w_mem_out': w_mem_out,
              'w_ffn_in_bf': w_ffn_in.astype(BF16), 'w_ffn_out_bf': w_ffn_out.astype(BF16),
              'ln_g': ln_g, 'ln_b': ln_b}
    bp, seq = x_prompt.shape[:2]
    pos_p = jnp.arange(seq, dtype=jnp.int32)
    mem_kv = jnp.einsum('bmd,lde->lbme', mem_prompt, w_mem_kv)
    memk_flat, memv_flat = jnp.split(mem_kv, 2, axis=-1)
    memk_prompt = memk_flat.reshape(DEPTH, bp, N_MEM, N_MEM_HEADS, MEM_HEAD_DIM)
    memv_prompt = memv_flat.reshape(DEPTH, bp, N_MEM, N_MEM_HEADS, MEM_HEAD_DIM)
    n_ssm = (DEPTH + 1) // 2
    conv0 = jnp.zeros((n_ssm, bp, CONV_W - 1, CONV_DIM), x_prompt.dtype)
    ssm0 = jnp.zeros((n_ssm, bp, N_SSM_HEADS, SSM_HEAD_DIM, D_STATE), x_prompt.dtype)
    attend_p = lambda j, x: dsa_prompt(x, w_att_in[j], w_att_out[j], pos_p)
    y_prompt, k_prompt, v_prompt, kidx_prompt, conv_prompt, ssm_prompt = run_trunk(
        x_prompt, memk_prompt, memv_prompt, conv0, ssm0, attend_p, params, tm=1024)
    past = page_table.shape[1] * PAGE_SIZE
    pos_s = past + jnp.arange(x_sample.shape[1], dtype=jnp.int32)
    attend_s = lambda j, x: dsa_sample(x, w_att_in[j], w_att_out[j], pos_s, cache_k[j], cache_v[j], cache_kidx[j],
                                       page_table)
    y_sample, k_sample, v_sample, kidx_sample, conv_sample, ssm_sample = run_trunk(
        x_sample, cache_mem_k, cache_mem_v, state_conv, state_ssm, attend_s, params, tm=256)
    return (y_prompt, y_sample, k_prompt, v_prompt, kidx_prompt, conv_prompt, ssm_prompt, memk_prompt, memv_prompt,
            k_sample, v_sample, kidx_sample, conv_sample, ssm_sample)
```

```python
import functools
import math

import jax
import jax.numpy as jnp
import numpy as np
from jax import lax
from jax.experimental import pallas as pl
from jax.experimental.pallas import tpu as pltpu

F32 = jnp.float32
BF16 = jnp.bfloat16

DEPTH = 4
N_MIXERS = 2
D_MODEL = 1024
PAGE_SIZE = 128

D_INNER = 2048
SSM_HEAD_DIM = 64
N_SSM_HEADS = 32
SSM_GROUPS = 4
HEADS_PER_GROUP = 8
D_STATE = 128
CONV_W = 4
CONV_DIM = D_INNER + 2 * SSM_GROUPS * D_STATE
SSM_CHUNK = 128

ATT_HEAD_DIM = 64
N_ATT_HEADS = 16
N_KV_HEADS = 4
KV_GROUP = 4
N_IDX_HEADS = 8
IDX_DIM = 64
TOPK_MAX = 256
Q_BLOCK = 128
ATT_SPLITS = [1024, 256, 256, 512, 64, 8]
ROPE_THETA = 10000.0

N_MEM = 256
N_MEM_HEADS = 4
MEM_HEAD_DIM = 256
FFN_HIDDEN = 2816

DEEPNORM_ALPHA = (2 * DEPTH) ** 0.25
LN_EPS = 1e-5

VMEM_LIMIT_BYTES = 48 * 1024 * 1024


def _layer_norm_rows(y, g, b):
    mu = jnp.mean(y, axis=-1, keepdims=True)
    d = y - mu
    var = jnp.mean(d * d, axis=-1, keepdims=True)
    return d * lax.rsqrt(var + LN_EPS) * g + b


def _ffn_ln_kernel(x_ref, wa_ref, wb_ref, wo_ref, g_ref, b_ref, o_ref, xb_ref, acc_ref):
    k = pl.program_id(1)

    @pl.when(k == 0)
    def _():
        xb_ref[...] = x_ref[...].astype(BF16)
        acc_ref[...] = jnp.zeros_like(acc_ref)

    xb = xb_ref[...]
    a = jnp.dot(xb, wa_ref[...], preferred_element_type=F32)
    b = jnp.dot(xb, wb_ref[...], preferred_element_type=F32)
    h = (a * jax.nn.sigmoid(a)) * b
    acc_ref[...] += jnp.dot(h.astype(BF16), wo_ref[...], preferred_element_type=F32)

    @pl.when(k == pl.num_programs(1) - 1)
    def _():
        y = DEEPNORM_ALPHA * x_ref[...] + acc_ref[...]
        o_ref[...] = _layer_norm_rows(y, g_ref[...], b_ref[...])


def ffn_ln(x, w_in, w_out, g, b, *, tm, th=256):
    m, d = x.shape
    hidden = w_out.shape[0]
    nh = hidden // th
    return pl.pallas_call(
        _ffn_ln_kernel,
        grid=(m // tm, nh),
        in_specs=[
            pl.BlockSpec((tm, d), lambda i, k: (i, 0)),
            pl.BlockSpec((d, th), lambda i, k: (0, k)),
            pl.BlockSpec((d, th), lambda i, k: (0, k + nh)),
            pl.BlockSpec((th, d), lambda i, k: (k, 0)),
            pl.BlockSpec((1, d), lambda i, k: (0, 0)),
            pl.BlockSpec((1, d), lambda i, k: (0, 0)),
        ],
        out_specs=pl.BlockSpec((tm, d), lambda i, k: (i, 0)),
        out_shape=jax.ShapeDtypeStruct((m, d), F32),
        scratch_shapes=[pltpu.VMEM((tm, d), BF16), pltpu.VMEM((tm, d), F32)],
        compiler_params=pltpu.CompilerParams(
            dimension_semantics=("parallel", "arbitrary"), vmem_limit_bytes=VMEM_LIMIT_BYTES),
        name="ffn_ln",
    )(x, w_in, w_in, w_out, g.reshape(1, d), b.reshape(1, d))


INT_MIN = -2 ** 31
NEG_BIG = -1e30
_NT = (((1,), (1,)), ((), ()))


def _float_key(x):
    x = jnp.where(x == 0.0, 0.0, x)
    bits = lax.bitcast_convert_type(x, jnp.int32)
    return bits ^ ((bits >> 31) & 0x7FFFFFFF)


def _dsa_prompt_kernel(q_ref, qi_ref, wi_ref, k_ref, vx_ref, ki_ref, o_ref,
                       qs_ref, qis_ref, wib_ref, key_ref, m_ref, acc_ref, *, tq, tk, topk):
    i = pl.program_id(1)
    nkb = ((i + 1) * tq + tk - 1) // tk
    nl = tk // 128

    def lanes(x):
        return jnp.concatenate([x] * nl, axis=1)

    for h in range(N_ATT_HEADS):
        j, g = divmod(h, KV_GROUP)
        qs_ref[j, g * tq:(g + 1) * tq, :] = q_ref[0, :, h * 64:(h + 1) * 64]
    for h in range(N_IDX_HEADS):
        qis_ref[h * tq:(h + 1) * tq, :] = qi_ref[0, :, h * 64:(h + 1) * 64]
        wib_ref[h] = jnp.broadcast_to(wi_ref[0, :, h:h + 1] * (N_IDX_HEADS ** -0.5), (tq, 128))
    q_pos = i * tq + lax.broadcasted_iota(jnp.int32, (tq, 1), 0)

    def score_block(kb, c):
        off = pl.multiple_of(kb * tk, tk)
        s = lax.dot_general(qis_ref[...], ki_ref[0, pl.ds(off, tk), :], _NT, preferred_element_type=F32)
        acc = jnp.zeros((tq, tk), F32)
        for h in range(N_IDX_HEADS):
            acc = acc + lanes(wib_ref[h]) * jnp.maximum(s[h * tq:(h + 1) * tq, :], 0.0)
        kpos = off + lax.broadcasted_iota(jnp.int32, (tq, tk), 1)
        key_ref[kb] = jnp.where(kpos <= q_pos, _float_key(acc), INT_MIN)
        return c

    lax.fori_loop(0, nkb, score_block, 0)

    def count(cand, strict):
        def body(kb, acc):
            blk = key_ref[kb]
            for c in range(nl):
                part = blk[:, c * 128:(c + 1) * 128]
                hit = (part > cand) if strict else (part >= cand)
                acc = acc + jnp.where(hit, 1.0, 0.0)
            return acc
        acc = lax.fori_loop(0, nkb, body, jnp.zeros((tq, 128), F32))
        return jnp.sum(acc, axis=1, keepdims=True)

    kf = float(topk)
    zero = jnp.zeros((tq, 128), jnp.int32)
    thr = jnp.where(count(zero, False) >= kf, zero, jnp.full((tq, 128), INT_MIN, jnp.int32))

    def bit_step(it, thr):
        cand = thr | jnp.left_shift(jnp.int32(1), 30 - it)
        return jnp.where(count(cand, False) >= kf, cand, thr)

    thr = lax.fori_loop(0, 31, bit_step, thr)
    need = jnp.where(thr == INT_MIN, 0.0, kf - count(thr, True))

    m_ref[...] = jnp.full(m_ref.shape, NEG_BIG, F32)
    acc_ref[...] = jnp.zeros(acc_ref.shape, F32)
    upper = jnp.where(lax.broadcasted_iota(jnp.int32, (tk, tk), 0) < lax.broadcasted_iota(jnp.int32, (tk, tk), 1),
                      1.0, 0.0).astype(BF16)
    thr_t, need_t = lanes(thr), lanes(need)

    def attend_block(kb, eq_before):
        off = pl.multiple_of(kb * tk, tk)
        key = key_ref[kb]
        eq = key == thr_t
        eqf = jnp.where(eq, 1.0, 0.0)
        rank = jnp.dot(eqf.astype(BF16), upper, preferred_element_type=F32) + lanes(eq_before)
        bias = jnp.where(key > thr_t, 0.0, jnp.where(eq, jnp.where(rank < need_t, 0.0, NEG_BIG), NEG_BIG))
        bias4 = jnp.concatenate([bias] * KV_GROUP, axis=0)
        for j in range(N_KV_HEADS):
            kblk = k_ref[0, pl.ds(off, tk), j * 64:(j + 1) * 64]
            s = lax.dot_general(qs_ref[j], kblk, _NT, preferred_element_type=F32) + bias4
            m_old = m_ref[j]
            mx = s[:, 0:128]
            for c in range(1, nl):
                mx = jnp.maximum(mx, s[:, c * 128:(c + 1) * 128])
            m_new = jnp.maximum(m_old, jnp.max(mx, axis=1, keepdims=True))
            p = jnp.exp(s - lanes(m_new))
            pv = jnp.dot(p.astype(BF16), vx_ref[0, j, pl.ds(off, tk), :], preferred_element_type=F32)
            acc_ref[j] = jnp.exp(m_old - m_new) * acc_ref[j] + pv
            m_ref[j] = m_new
        return eq_before + jnp.sum(eqf, axis=1, keepdims=True)

    lax.fori_loop(0, nkb, attend_block, jnp.zeros((tq, 128), F32))

    for h in range(N_ATT_HEADS):
        j, g = divmod(h, KV_GROUP)
        a = acc_ref[j, g * tq:(g + 1) * tq, :]
        o_ref[0, :, h * 64:(h + 1) * 64] = a[:, 0:64] / a[:, 64:65]


def dsa_prompt_attend(q, k, v, qi, ki, wi, *, tq=128, tk=256):
    bsz, t = q.shape[:2]
    topk = min(TOPK_MAX, t // 4)
    q_bf = (q * ATT_HEAD_DIM ** -0.5).astype(BF16)
    qi_bf = (qi * IDX_DIM ** -0.5).astype(BF16)
    vh = v.reshape(bsz, t, N_KV_HEADS, ATT_HEAD_DIM).transpose(0, 2, 1, 3)
    vx = jnp.concatenate([vh, jnp.ones_like(vh)], axis=-1).astype(BF16)
    kern = functools.partial(_dsa_prompt_kernel, tq=tq, tk=tk, topk=topk)
    return pl.pallas_call(
        kern,
        grid=(bsz, t // tq),
        in_specs=[
            pl.BlockSpec((1, tq, 1024), lambda b, i: (b, i, 0)),
            pl.BlockSpec((1, tq, 512), lambda b, i: (b, i, 0)),
            pl.BlockSpec((1, tq, N_IDX_HEADS), lambda b, i: (b, i, 0)),
            pl.BlockSpec((1, t, 256), lambda b, i: (b, 0, 0)),
            pl.BlockSpec((1, N_KV_HEADS, t, 128), lambda b, i: (b, 0, 0, 0)),
            pl.BlockSpec((1, t, IDX_DIM), lambda b, i: (b, 0, 0)),
        ],
        out_specs=pl.BlockSpec((1, tq, 1024), lambda b, i: (b, i, 0)),
        out_shape=jax.ShapeDtypeStruct((bsz, t, 1024), F32),
        scratch_shapes=[
            pltpu.VMEM((N_KV_HEADS, KV_GROUP * tq, 64), BF16),
            pltpu.VMEM((N_IDX_HEADS * tq, 64), BF16),
            pltpu.VMEM((N_IDX_HEADS, tq, 128), F32),
            pltpu.VMEM((t // tk, tq, tk), jnp.int32),
            pltpu.VMEM((N_KV_HEADS, KV_GROUP * tq, 128), F32),
            pltpu.VMEM((N_KV_HEADS, KV_GROUP * tq, 128), F32),
        ],
        compiler_params=pltpu.CompilerParams(
            dimension_semantics=("parallel", "arbitrary"), vmem_limit_bytes=VMEM_LIMIT_BYTES),
        name="dsa_prompt_attend",
    )(q_bf, qi_bf, wi, k.astype(BF16), vx, ki.astype(BF16))


def layer_norm(x, g, b):
    xf = x.astype(F32)
    mu = jnp.mean(xf, -1, keepdims=True)
    var = jnp.mean(jnp.square(xf - mu), -1, keepdims=True)
    return ((xf - mu) * lax.rsqrt(var + LN_EPS) * g.astype(F32) + b.astype(F32)).astype(x.dtype)


def rope(x, pos):
    d = x.shape[-1]
    inv_freq = ROPE_THETA ** (-jnp.arange(0, d, 2, dtype=F32) / d)
    ang = pos.astype(F32)[:, None] * inv_freq[None, :]
    cos = jnp.cos(ang)[:, None, :]
    sin = jnp.sin(ang)[:, None, :]
    x1, x2 = jnp.split(x.astype(F32), 2, axis=-1)
    return jnp.concatenate([x1 * cos - x2 * sin, x1 * sin + x2 * cos], -1).astype(x.dtype)


def mem_attend(x, w_q, w_o, mk, mv):
    bsz, t = x.shape[:2]
    q = (x @ w_q).reshape(bsz, t, N_MEM_HEADS, MEM_HEAD_DIM)
    logits = jnp.einsum('bthd,bmhd->bhtm', q.astype(F32), mk.astype(F32)) * MEM_HEAD_DIM ** -0.5
    p = jax.nn.softmax(logits, axis=-1)
    o = jnp.einsum('bhtm,bmhd->bthd', p, mv.astype(F32)).reshape(bsz, t, D_MODEL)
    return o.astype(x.dtype) @ w_o


def ssd_chunked(xdt, a, bm, cm, h0, chunk):
    bsz, L, G, R, P = xdt.shape
    N = bm.shape[-1]
    nc = L // chunk
    x = xdt.reshape(bsz, nc, chunk, G, R, P)
    a = a.reshape(bsz, nc, chunk, G, R)
    bm = bm.reshape(bsz, nc, chunk, G, N)
    cm = cm.reshape(bsz, nc, chunk, G, N)
    acum = jnp.cumsum(a, axis=2)
    acum_t = jnp.moveaxis(acum, 2, -1)
    seg = acum_t[..., :, None] - acum_t[..., None, :]
    causal = jnp.tril(jnp.ones((chunk, chunk), dtype=bool))
    lmat = jnp.exp(jnp.where(causal, seg, -jnp.inf))
    cb = jnp.einsum('bclgn,bcsgn->bcgls', cm, bm)
    y_diag = jnp.einsum('bcgrls,bcsgrp->bclgrp', cb[:, :, :, None] * lmat, x)
    decay = jnp.exp(acum[:, :, -1:] - acum)
    states = jnp.einsum('bclgn,bclgrp->bcgrpn', bm, x * decay[..., None])
    chunk_decay = jnp.exp(acum[:, :, -1])

    def step(h, inp):
        st, dec = inp
        return h * dec[..., None, None] + st, h

    h_last, h_prev = lax.scan(step, h0, (jnp.moveaxis(states, 1, 0), jnp.moveaxis(chunk_decay, 1, 0)))
    h_prev = jnp.moveaxis(h_prev, 0, 1)
    y_off = jnp.einsum('bclgn,bcgrpn->bclgrp', cm, h_prev) * jnp.exp(acum)[..., None]
    return (y_diag + y_off).reshape(bsz, L, G, R, P), h_last


def mamba_mixer(x, w_in, conv_w, conv_b, dt_bias, a_log, d_skip, norm_w, w_out, conv_state, ssm_state):
    bsz, t = x.shape[:2]
    z, xbc, dt = jnp.split(x @ w_in, [D_INNER, D_INNER + CONV_DIM], axis=-1)
    xbc_ext = jnp.concatenate([conv_state.astype(xbc.dtype), xbc], axis=1)
    conv = conv_b
    for w in range(CONV_W):
        conv = conv + xbc_ext[:, w:w + t] * conv_w[w]
    new_conv = xbc_ext[:, t:]
    xbc = jax.nn.silu(conv).astype(F32)
    xs, bm, cm = jnp.split(xbc, [D_INNER, D_INNER + SSM_GROUPS * D_STATE], axis=-1)
    xs = xs.reshape(bsz, t, SSM_GROUPS, HEADS_PER_GROUP, SSM_HEAD_DIM)
    bm = bm.reshape(bsz, t, SSM_GROUPS, D_STATE)
    cm = cm.reshape(bsz, t, SSM_GROUPS, D_STATE)
    dt = jax.nn.softplus(dt.astype(F32) + dt_bias.astype(F32)).reshape(bsz, t, SSM_GROUPS, HEADS_PER_GROUP)
    a = -jnp.exp(a_log.astype(F32)).reshape(SSM_GROUPS, HEADS_PER_GROUP)
    chunk = SSM_CHUNK if t % SSM_CHUNK == 0 else t
    h0 = ssm_state.astype(F32).reshape(bsz, SSM_GROUPS, HEADS_PER_GROUP, SSM_HEAD_DIM, D_STATE)
    y, h_new = ssd_chunked(xs * dt[..., None], dt * a, bm, cm, h0, chunk)
    y = (y + xs * d_skip.astype(F32).reshape(SSM_GROUPS, HEADS_PER_GROUP)[:, :, None]).reshape(bsz, t, D_INNER)
    yg = y * jax.nn.silu(z.astype(F32))
    yg = yg * lax.rsqrt(jnp.mean(jnp.square(yg), -1, keepdims=True) + LN_EPS) * norm_w.astype(F32)
    out = yg.astype(x.dtype) @ w_out
    return out, new_conv, h_new.reshape(bsz, N_SSM_HEADS, SSM_HEAD_DIM, D_STATE).astype(ssm_state.dtype)


def dsa_project(x, w_in, pos):
    bsz, t = x.shape[:2]
    cuts = list(np.cumsum(ATT_SPLITS)[:-1])
    q, k, v, qi, ki, wi = jnp.split(x @ w_in, cuts, axis=-1)
    q = rope(q.reshape(bsz, t, N_ATT_HEADS, ATT_HEAD_DIM), pos)
    k = rope(k.reshape(bsz, t, N_KV_HEADS, ATT_HEAD_DIM), pos)
    v = v.reshape(bsz, t, N_KV_HEADS, ATT_HEAD_DIM)
    qi = rope(qi.reshape(bsz, t, N_IDX_HEADS, IDX_DIM), pos)
    ki = rope(ki[:, :, None, :], pos)[:, :, 0, :]
    return q, k, v, qi, ki, wi


def dsa_select_attend(q, qi, wi, ki_all, q_pos, gather_kv, topk):
    bsz, t = q.shape[:2]
    L = ki_all.shape[1]
    s = jnp.einsum('bthd,bsd->bths', qi.astype(F32), ki_all.astype(F32)) * IDX_DIM ** -0.5
    score = jnp.einsum('bth,bths->bts', wi.astype(F32) * N_IDX_HEADS ** -0.5, jax.nn.relu(s))
    admissible = jnp.arange(L, dtype=jnp.int32)[None, :] <= q_pos[:, None]
    score = jnp.where(admissible[None], score, -jnp.inf)
    _, idx = lax.top_k(score, topk)
    valid = idx <= q_pos[None, :, None]
    k_sel, v_sel = gather_kv(idx)
    qg = q.reshape(bsz, t, N_KV_HEADS, KV_GROUP, ATT_HEAD_DIM)
    logits = jnp.einsum('btjgd,btsjd->btjgs', qg.astype(F32), k_sel.astype(F32)) * ATT_HEAD_DIM ** -0.5
    logits = jnp.where(valid[:, :, None, None, :], logits, -jnp.inf)
    p = jax.nn.softmax(logits, axis=-1)
    o = jnp.einsum('btjgs,btsjd->btjgd', p, v_sel.astype(F32))
    return o.reshape(bsz, t, N_ATT_HEADS * ATT_HEAD_DIM).astype(q.dtype)


def take_rows(a, i):
    return jax.vmap(lambda ab, ib: ab[ib])(a, i)


def dsa_prompt(x, w_in, w_out, pos):
    bsz, t = x.shape[:2]
    q, k, v, qi, ki, wi = dsa_project(x, w_in, pos)
    out = dsa_prompt_attend(q.reshape(bsz, t, -1), k.reshape(bsz, t, -1), v.reshape(bsz, t, -1),
                            qi.reshape(bsz, t, -1), ki, wi)
    return out @ w_out, k, v, ki


def dsa_sample(x, w_in, w_out, pos, pool_k, pool_v, pool_ki, page_table):
    bsz, t = x.shape[:2]
    q, k, v, qi, ki, wi = dsa_project(x, w_in, pos)
    past = page_table.shape[1] * PAGE_SIZE
    ki_past = pool_ki[page_table].reshape(bsz, past, IDX_DIM)
    ki_all = jnp.concatenate([ki_past.astype(ki.dtype), ki], axis=1)
    topk = min(TOPK_MAX, (past + t) // 4)

    def gather(idx):
        in_past = idx < past
        pidx = jnp.minimum(idx, past - 1)
        phys = jnp.take_along_axis(page_table, (pidx // PAGE_SIZE).reshape(bsz, -1), axis=1).reshape(idx.shape)
        off = pidx % PAGE_SIZE
        nidx = jnp.clip(idx - past, 0, t - 1)
        sel = in_past[..., None, None]
        k_g = jnp.where(sel, pool_k[phys, off].astype(k.dtype), take_rows(k, nidx))
        v_g = jnp.where(sel, pool_v[phys, off].astype(v.dtype), take_rows(v, nidx))
        return k_g, v_g

    out = dsa_select_attend(q, qi, wi, ki_all, pos, gather, topk)
    return out @ w_out, k, v, ki


def run_trunk(x, mem_k, mem_v, conv0, ssm0, attend, p, tm):
    conv_out, ssm_out, k_out, v_out, ki_out = [], [], [], [], []
    bsz, t = x.shape[:2]
    for i in range(DEPTH):
        j = i // N_MIXERS
        if i % N_MIXERS == 0:
            h, cs, ss = mamba_mixer(x, p['w_ssm_in'][j], p['ssm_conv_w'][j], p['ssm_conv_b'][j], p['ssm_dt_bias'][j],
                                    p['ssm_a_log'][j], p['ssm_d'][j], p['ssm_norm_w'][j], p['w_ssm_out'][j],
                                    conv0[j], ssm0[j])
            conv_out.append(cs)
            ssm_out.append(ss)
        else:
            h, k, v, ki = attend(j, x)
            k_out.append(k)
            v_out.append(v)
            ki_out.append(ki)
        x = layer_norm(DEEPNORM_ALPHA * x + h, p['ln_g'][i, 0], p['ln_b'][i, 0])
        x = layer_norm(DEEPNORM_ALPHA * x + mem_attend(x, p['w_mem_q'][i], p['w_mem_out'][i], mem_k[i], mem_v[i]),
                       p['ln_g'][i, 1], p['ln_b'][i, 1])
        x = ffn_ln(x.reshape(bsz * t, D_MODEL), p['w_ffn_in_bf'][i], p['w_ffn_out_bf'][i],
                   p['ln_g'][i, 2], p['ln_b'][i, 2], tm=tm).reshape(bsz, t, D_MODEL)
    return x, jnp.stack(k_out), jnp.stack(v_out), jnp.stack(ki_out), jnp.stack(conv_out), jnp.stack(ssm_out)


def kernel(x_prompt, x_sample, cache_k, cache_v, cache_kidx, cache_mem_k, cache_mem_v, state_conv, state_ssm,
           page_table, mem_prompt, w_ssm_in, ssm_conv_w, ssm_conv_b, ssm_dt_bias, ssm_a_log, ssm_d, ssm_norm_w,
           w_ssm_out, w_att_in, w_att_out, w_mem_q, w_mem_kv, w_mem_out, w_ffn_in, w_ffn_out, ln_g, ln_b):
    params = {'w_ssm_in': w_ssm_in, 'ssm_conv_w': ssm_conv_w, 'ssm_conv_b': ssm_conv_b, 'ssm_dt_bias': ssm_dt_bias,
              'ssm_a_log': ssm_a_log, 'ssm_d': ssm_d, 'ssm_norm_w': ssm_norm_w, 'w_ssm_out': w_ssm_out,
              'w_mem_q': w_mem_q, 'w_mem_out': w_mem_out,
              'w_ffn_in_bf': w_ffn_in.astype(BF16), 'w_ffn_out_bf': w_ffn_out.astype(BF16),
              'ln_g': ln_g, 'ln_b': ln_b}
    bp, seq = x_prompt.shape[:2]
    pos_p = jnp.arange(seq, dtype=jnp.int32)
    mem_kv = jnp.einsum('bmd,lde->lbme', mem_prompt, w_mem_kv)
    memk_flat, memv_flat = jnp.split(mem_kv, 2, axis=-1)
    memk_prompt = memk_flat.reshape(DEPTH, bp, N_MEM, N_MEM_HEADS, MEM_HEAD_DIM)
    memv_prompt = memv_flat.reshape(DEPTH, bp, N_MEM, N_MEM_HEADS, MEM_HEAD_DIM)
    n_ssm = (DEPTH + 1) // 2
    conv0 = jnp.zeros((n_ssm, bp, CONV_W - 1, CONV_DIM), x_prompt.dtype)
    ssm0 = jnp.zeros((n_ssm, bp, N_SSM_HEADS, SSM_HEAD_DIM, D_STATE), x_prompt.dtype)
    attend_p = lambda j, x: dsa_prompt(x, w_att_in[j], w_att_out[j], pos_p)
    y_prompt, k_prompt, v_prompt, kidx_prompt, conv_prompt, ssm_prompt = run_trunk(
        x_prompt, memk_prompt, memv_prompt, conv0, ssm0, attend_p, params, tm=1024)
    past = page_table.shape[1] * PAGE_SIZE
    pos_s = past + jnp.arange(x_sample.shape[1], dtype=jnp.int32)
    attend_s = lambda j, x: dsa_sample(x, w_att_in[j], w_att_out[j], pos_s, cache_k[j], cache_v[j], cache_kidx[j],
                                       page_table)
    y_sample, k_sample, v_sample, kidx_sample, conv_sample, ssm_sample = run_trunk(
        x_sample, cache_mem_k, cache_mem_v, state_conv, state_ssm, attend_s, params, tm=256)
    return (y_prompt, y_sample, k_prompt, v_prompt, kidx_prompt, conv_prompt, ssm_prompt, memk_prompt, memv_prompt,
            k_sample, v_sample, kidx_sample, conv_sample, ssm_sample)
```

```python
import functools
import math

import jax
import jax.numpy as jnp
import numpy as np
from jax import lax
from jax.experimental import pallas as pl
from jax.experimental.pallas import tpu as pltpu

F32 = jnp.float32
BF16 = jnp.bfloat16

DEPTH = 4
N_MIXERS = 2
D_MODEL = 1024
PAGE_SIZE = 128

D_INNER = 2048
SSM_HEAD_DIM = 64
N_SSM_HEADS = 32
SSM_GROUPS = 4
HEADS_PER_GROUP = 8
D_STATE = 128
CONV_W = 4
CONV_DIM = D_INNER + 2 * SSM_GROUPS * D_STATE
SSM_CHUNK = 128

ATT_HEAD_DIM = 64
N_ATT_HEADS = 16
N_KV_HEADS = 4
KV_GROUP = 4
N_IDX_HEADS = 8
IDX_DIM = 64
TOPK_MAX = 256
Q_BLOCK = 128
ATT_SPLITS = [1024, 256, 256, 512, 64, 8]
ROPE_THETA = 10000.0

N_MEM = 256
N_MEM_HEADS = 4
MEM_HEAD_DIM = 256
FFN_HIDDEN = 2816

DEEPNORM_ALPHA = (2 * DEPTH) ** 0.25
LN_EPS = 1e-5

VMEM_LIMIT_BYTES = 48 * 1024 * 1024


def _layer_norm_rows(y, g, b):
    mu = jnp.mean(y, axis=-1, keepdims=True)
    d = y - mu
    var = jnp.mean(d * d, axis=-1, keepdims=True)
    return d * lax.rsqrt(var + LN_EPS) * g + b


def _ffn_ln_kernel(x_ref, wa_ref, wb_ref, wo_ref, g_ref, b_ref, o_ref, xb_ref, acc_ref):
    k = pl.program_id(1)

    @pl.when(k == 0)
    def _():
        xb_ref[...] = x_ref[...].astype(BF16)
        acc_ref[...] = jnp.zeros_like(acc_ref)

    xb = xb_ref[...]
    a = jnp.dot(xb, wa_ref[...], preferred_element_type=F32)
    b = jnp.dot(xb, wb_ref[...], preferred_element_type=F32)
    h = (a * jax.nn.sigmoid(a)) * b
    acc_ref[...] += jnp.dot(h.astype(BF16), wo_ref[...], preferred_element_type=F32)

    @pl.when(k == pl.num_programs(1) - 1)
    def _():
        y = DEEPNORM_ALPHA * x_ref[...] + acc_ref[...]
        o_ref[...] = _layer_norm_rows(y, g_ref[...], b_ref[...])


def ffn_ln(x, w_in, w_out, g, b, *, tm, th=256):
    m, d = x.shape
    hidden = w_out.shape[0]
    nh = hidden // th
    return pl.pallas_call(
        _ffn_ln_kernel,
        grid=(m // tm, nh),
        in_specs=[
            pl.BlockSpec((tm, d), lambda i, k: (i, 0)),
            pl.BlockSpec((d, th), lambda i, k: (0, k)),
            pl.BlockSpec((d, th), lambda i, k: (0, k + nh)),
            pl.BlockSpec((th, d), lambda i, k: (k, 0)),
            pl.BlockSpec((1, d), lambda i, k: (0, 0)),
            pl.BlockSpec((1, d), lambda i, k: (0, 0)),
        ],
        out_specs=pl.BlockSpec((tm, d), lambda i, k: (i, 0)),
        out_shape=jax.ShapeDtypeStruct((m, d), F32),
        scratch_shapes=[pltpu.VMEM((tm, d), BF16), pltpu.VMEM((tm, d), F32)],
        compiler_params=pltpu.CompilerParams(
            dimension_semantics=("parallel", "arbitrary"), vmem_limit_bytes=VMEM_LIMIT_BYTES),
        name="ffn_ln",
    )(x, w_in, w_in, w_out, g.reshape(1, d), b.reshape(1, d))


INT_MIN = -2 ** 31
NEG_BIG = -1e30
_NT = (((1,), (1,)), ((), ()))


def _float_key(x):
    x = jnp.where(x == 0.0, 0.0, x)
    bits = lax.bitcast_convert_type(x, jnp.int32)
    return bits ^ ((bits >> 31) & 0x7FFFFFFF)


def _dsa_prompt_kernel(q_ref, qi_ref, wi_ref, k_ref, vx_ref, ki_ref, o_ref,
                       qs_ref, qis_ref, wib_ref, key_ref, m_ref, acc_ref, *, tq, tk, topk):
    i = pl.program_id(1)
    nkb = ((i + 1) * tq + tk - 1) // tk
    nl = tk // 128

    def lanes(x):
        return jnp.concatenate([x] * nl, axis=1)

    for h in range(N_ATT_HEADS):
        j, g = divmod(h, KV_GROUP)
        qs_ref[j, g * tq:(g + 1) * tq, :] = q_ref[0, :, h * 64:(h + 1) * 64]
    for h in range(N_IDX_HEADS):
        qis_ref[h * tq:(h + 1) * tq, :] = qi_ref[0, :, h * 64:(h + 1) * 64]
        wib_ref[h] = jnp.broadcast_to(wi_ref[0, :, h:h + 1] * (N_IDX_HEADS ** -0.5), (tq, 128))
    q_pos = i * tq + lax.broadcasted_iota(jnp.int32, (tq, 1), 0)

    def score_block(kb, c):
        off = pl.multiple_of(kb * tk, tk)
        s = lax.dot_general(qis_ref[...], ki_ref[0, pl.ds(off, tk), :], _NT, preferred_element_type=F32)
        acc = jnp.zeros((tq, tk), F32)
        for h in range(N_IDX_HEADS):
            acc = acc + lanes(wib_ref[h]) * jnp.maximum(s[h * tq:(h + 1) * tq, :], 0.0)
        kpos = off + lax.broadcasted_iota(jnp.int32, (tq, tk), 1)
        key_ref[kb] = jnp.where(kpos <= q_pos, _float_key(acc), INT_MIN)
        return c

    lax.fori_loop(0, nkb, score_block, 0)

    def count(cand, strict):
        def body(kb, acc):
            blk = key_ref[kb]
            for c in range(nl):
                part = blk[:, c * 128:(c + 1) * 128]
                hit = (part > cand) if strict else (part >= cand)
                acc = acc + jnp.where(hit, 1.0, 0.0)
            return acc
        acc = lax.fori_loop(0, nkb, body, jnp.zeros((tq, 128), F32))
        return jnp.sum(acc, axis=1, keepdims=True)

    kf = float(topk)
    zero = jnp.zeros((tq, 128), jnp.int32)
    thr = jnp.where(count(zero, False) >= kf, zero, jnp.full((tq, 128), INT_MIN, jnp.int32))

    def bit_step(it, thr):
        cand = thr | jnp.left_shift(jnp.int32(1), 30 - it)
        return jnp.where(count(cand, False) >= kf, cand, thr)

    thr = lax.fori_loop(0, 31, bit_step, thr)
    need = jnp.where(thr == INT_MIN, 0.0, kf - count(thr, True))

    m_ref[...] = jnp.full(m_ref.shape, NEG_BIG, F32)
    acc_ref[...] = jnp.zeros(acc_ref.shape, F32)
    upper = jnp.where(lax.broadcasted_iota(jnp.int32, (tk, tk), 0) < lax.broadcasted_iota(jnp.int32, (tk, tk), 1),
                      1.0, 0.0).astype(BF16)
    thr_t, need_t = lanes(thr), lanes(need)

    def attend_block(kb, eq_before):
        off = pl.multiple_of(kb * tk, tk)
        key = key_ref[kb]
        eq = key == thr_t
        eqf = jnp.where(eq, 1.0, 0.0)
        rank = jnp.dot(eqf.astype(BF16), upper, preferred_element_type=F32) + lanes(eq_before)
        bias = jnp.where(key > thr_t, 0.0, jnp.where(eq, jnp.where(rank < need_t, 0.0, NEG_BIG), NEG_BIG))
        bias4 = jnp.concatenate([bias] * KV_GROUP, axis=0)
        for j in range(N_KV_HEADS):
            kblk = k_ref[0, pl.ds(off, tk), j * 64:(j + 1) * 64]
            s = lax.dot_general(qs_ref[j], kblk, _NT, preferred_element_type=F32) + bias4
            m_old = m_ref[j]
            mx = s[:, 0:128]
            for c in range(1, nl):
                mx = jnp.maximum(mx, s[:, c * 128:(c + 1) * 128])
            m_new = jnp.maximum(m_old, jnp.max(mx, axis=1, keepdims=True))
            p = jnp.exp(s - lanes(m_new))
            pv = jnp.dot(p.astype(BF16), vx_ref[0, j, pl.ds(off, tk), :], preferred_element_type=F32)
            acc_ref[j] = jnp.exp(m_old - m_new) * acc_ref[j] + pv
            m_ref[j] = m_new
        return eq_before + jnp.sum(eqf, axis=1, keepdims=True)

    lax.fori_loop(0, nkb, attend_block, jnp.zeros((tq, 128), F32))

    for h in range(N_ATT_HEADS):
        j, g = divmod(h, KV_GROUP)
        a = acc_ref[j, g * tq:(g + 1) * tq, :]
        o_ref[0, :, h * 64:(h + 1) * 64] = a[:, 0:64] / a[:, 64:65]


def dsa_prompt_attend(q, k, v, qi, ki, wi, *, tq=128, tk=256):
    bsz, t = q.shape[:2]
    topk = min(TOPK_MAX, t // 4)
    q_bf = (q * ATT_HEAD_DIM ** -0.5).astype(BF16)
    qi_bf = (qi * IDX_DIM ** -0.5).astype(BF16)
    vh = v.reshape(bsz, t, N_KV_HEADS, ATT_HEAD_DIM).transpose(0, 2, 1, 3)
    vx = jnp.concatenate([vh, jnp.ones_like(vh)], axis=-1).astype(BF16)
    kern = functools.partial(_dsa_prompt_kernel, tq=tq, tk=tk, topk=topk)
    return pl.pallas_call(
        kern,
        grid=(bsz, t // tq),
        in_specs=[
            pl.BlockSpec((1, tq, 1024), lambda b, i: (b, i, 0)),
            pl.BlockSpec((1, tq, 512), lambda b, i: (b, i, 0)),
            pl.BlockSpec((1, tq, N_IDX_HEADS), lambda b, i: (b, i, 0)),
            pl.BlockSpec((1, t, 256), lambda b, i: (b, 0, 0)),
            pl.BlockSpec((1, N_KV_HEADS, t, 128), lambda b, i: (b, 0, 0, 0)),
            pl.BlockSpec((1, t, IDX_DIM), lambda b, i: (b, 0, 0)),
        ],
        out_specs=pl.BlockSpec((1, tq, 1024), lambda b, i: (b, i, 0)),
        out_shape=jax.ShapeDtypeStruct((bsz, t, 1024), F32),
        scratch_shapes=[
            pltpu.VMEM((N_KV_HEADS, KV_GROUP * tq, 64), BF16),
            pltpu.VMEM((N_IDX_HEADS * tq, 64), BF16),
            pltpu.VMEM((N_IDX_HEADS, tq, 128), F32),
            pltpu.VMEM((t // tk, tq, tk), jnp.int32),
            pltpu.VMEM((N_KV_HEADS, KV_GROUP * tq, 128), F32),
            pltpu.VMEM((N_KV_HEADS, KV_GROUP * tq, 128), F32),
        ],
        compiler_params=pltpu.CompilerParams(
            dimension_semantics=("parallel", "arbitrary"), vmem_limit_bytes=VMEM_LIMIT_BYTES),
        name="dsa_prompt_attend",
    )(q_bf, qi_bf, wi, k.astype(BF16), vx, ki.astype(BF16))


PAGES_PER_STEP = 8


def _sample_select_kernel(pt_ref, qi_ref, wi_ref, kinew_ref, *rest, t, n_steps, topk):
    ki_refs = rest[:PAGES_PER_STEP]
    bias_ref, qis_ref, wib_ref, key_ref = rest[PAGES_PER_STEP:]
    p = pl.program_id(1)
    w = PAGES_PER_STEP * PAGE_SIZE
    nl = w // 128

    @pl.when(p == 0)
    def _():
        for h in range(N_IDX_HEADS):
            qis_ref[h * t:(h + 1) * t, :] = qi_ref[0, :, h * 64:(h + 1) * 64]
            wib_ref[h] = jnp.broadcast_to(wi_ref[0, :, h:h + 1] * (N_IDX_HEADS ** -0.5), (t, 128))

    def scores(ki_rows):
        s = lax.dot_general(qis_ref[...].astype(BF16), ki_rows.astype(BF16), _NT, preferred_element_type=F32)
        acc = jnp.zeros((t, ki_rows.shape[0]), F32)
        for h in range(N_IDX_HEADS):
            wfull = jnp.concatenate([wib_ref[h]] * (ki_rows.shape[0] // 128), axis=1)
            acc = acc + wfull * jnp.maximum(s[h * t:(h + 1) * t, :], 0.0)
        return acc

    key_ref[p] = _float_key(scores(jnp.concatenate([r[...] for r in ki_refs], axis=0)))

    @pl.when(p == n_steps - 1)
    def _():
        knew = _float_key(scores(kinew_ref[0]))
        n_idx = lax.broadcasted_iota(jnp.int32, (t, 128), 1)
        r_idx = lax.broadcasted_iota(jnp.int32, (t, 128), 0)
        knew = jnp.where(n_idx <= r_idx, knew, INT_MIN)
        key_ref[n_steps] = jnp.concatenate([knew, jnp.full((t, w - 128), INT_MIN, jnp.int32)], axis=1)

        def count(cand, strict):
            def body(blk, acc):
                kb = key_ref[blk]
                for c in range(nl):
                    part = kb[:, c * 128:(c + 1) * 128]
                    hit = (part > cand) if strict else (part >= cand)
                    acc = acc + jnp.where(hit, 1.0, 0.0)
                return acc
            acc = lax.fori_loop(0, n_steps + 1, body, jnp.zeros((t, 128), F32))
            return jnp.sum(acc, axis=1, keepdims=True)

        kf = float(topk)
        zero = jnp.zeros((t, 128), jnp.int32)
        thr = jnp.where(count(zero, False) >= kf, zero, jnp.full((t, 128), INT_MIN, jnp.int32))

        def bit_step(it, thr):
            cand = thr | jnp.left_shift(jnp.int32(1), 30 - it)
            return jnp.where(count(cand, False) >= kf, cand, thr)

        thr = lax.fori_loop(0, 31, bit_step, thr)
        need = jnp.where(thr == INT_MIN, 0.0, kf - count(thr, True))
        upper = jnp.where(lax.broadcasted_iota(jnp.int32, (128, 128), 0)
                          < lax.broadcasted_iota(jnp.int32, (128, 128), 1), 1.0, 0.0).astype(BF16)

        def emit(blk, eq_before):
            kb = key_ref[blk]
            parts = []
            for c in range(nl):
                part = kb[:, c * 128:(c + 1) * 128]
                eq = part == thr
                eqf = jnp.where(eq, 1.0, 0.0)
                rank = jnp.dot(eqf.astype(BF16), upper, preferred_element_type=F32) + eq_before
                parts.append(jnp.where(part > thr, 0.0,
                                       jnp.where(eq, jnp.where(rank < need, 0.0, NEG_BIG), NEG_BIG)))
                eq_before = eq_before + jnp.sum(eqf, axis=1, keepdims=True)
            bias_ref[0, blk] = jnp.concatenate(parts, axis=1)
            return eq_before

        lax.fori_loop(0, n_steps + 1, emit, jnp.zeros((t, 128), F32))


def _sample_attend_kernel(pt_ref, q_ref, bias_ref, biasnew_ref, knew_ref, vnew_ref, *rest, t, n_steps):
    k_refs = rest[:PAGES_PER_STEP]
    v_refs = rest[PAGES_PER_STEP:2 * PAGES_PER_STEP]
    o_ref, qs_ref, m_ref, l_ref, acc_ref = rest[2 * PAGES_PER_STEP:]
    p = pl.program_id(1)
    rows = KV_GROUP * t

    @pl.when(p == 0)
    def _():
        for h in range(N_ATT_HEADS):
            j, g = divmod(h, KV_GROUP)
            qs_ref[j, g * t:(g + 1) * t, :] = q_ref[0, :, h * 64:(h + 1) * 64]
        m_ref[...] = jnp.full(m_ref.shape, NEG_BIG, F32)
        l_ref[...] = jnp.zeros(l_ref.shape, F32)
        acc_ref[...] = jnp.zeros(acc_ref.shape, F32)

    def update(kcat, vcat, bias):
        bias4 = jnp.concatenate([bias] * KV_GROUP, axis=0)
        for j in range(N_KV_HEADS):
            s = lax.dot_general(qs_ref[j].astype(BF16), kcat[:, j * 64:(j + 1) * 64], _NT,
                                preferred_element_type=F32) + bias4
            m_old = m_ref[j]
            m_new = jnp.maximum(m_old, jnp.max(s, axis=1, keepdims=True))
            alpha = jnp.exp(m_old - m_new)
            pr = jnp.exp(s - m_new[:, 0:1])
            l_ref[j] = alpha * l_ref[j] + jnp.sum(pr, axis=1, keepdims=True)
            pv = jnp.dot(pr.astype(BF16), vcat[:, j * 64:(j + 1) * 64], preferred_element_type=F32)
            acc_ref[j] = alpha[:, 0:64] * acc_ref[j] + pv
            m_ref[j] = m_new

    update(jnp.concatenate([r[...] for r in k_refs], axis=0).astype(BF16),
           jnp.concatenate([r[...] for r in v_refs], axis=0).astype(BF16), bias_ref[0, 0])

    @pl.when(p == n_steps - 1)
    def _():
        update(knew_ref[0].astype(BF16), vnew_ref[0].astype(BF16), biasnew_ref[0, 0, :, 0:128])
        for h in range(N_ATT_HEADS):
            j, g = divmod(h, KV_GROUP)
            sl = slice(g * t, (g + 1) * t)
            o_ref[0, :, h * 64:(h + 1) * 64] = acc_ref[j, sl, :] / l_ref[j, sl, 0:64]


def dsa_sample_attend(q, k, v, qi, ki, wi, cache_k, cache_v, cache_kidx, layer, page_table):
    bsz, t = q.shape[:2]
    n_pages = page_table.shape[1]
    past = n_pages * PAGE_SIZE
    n_steps = n_pages // PAGES_PER_STEP
    w = PAGES_PER_STEP * PAGE_SIZE
    topk = min(TOPK_MAX, (past + t) // 4)
    n_pool = cache_k.shape[1]
    ck = cache_k.reshape(cache_k.shape[0], n_pool, PAGE_SIZE, 256)
    cv = cache_v.reshape(cache_v.shape[0], n_pool, PAGE_SIZE, 256)
    pad = lambda a: jnp.pad(a, ((0, 0), (0, PAGE_SIZE - t), (0, 0)))

    def page_spec(width, r):
        return pl.BlockSpec((None, None, PAGE_SIZE, width),
                            lambda b, p, pt: (layer, pt[b, p * PAGES_PER_STEP + r], 0, 0))

    bias = pl.pallas_call(
        functools.partial(_sample_select_kernel, t=t, n_steps=n_steps, topk=topk),
        grid_spec=pltpu.PrefetchScalarGridSpec(
            num_scalar_prefetch=1,
            grid=(bsz, n_steps),
            in_specs=[
                pl.BlockSpec((1, t, 512), lambda b, p, pt: (b, 0, 0)),
                pl.BlockSpec((1, t, N_IDX_HEADS), lambda b, p, pt: (b, 0, 0)),
                pl.BlockSpec((1, PAGE_SIZE, IDX_DIM), lambda b, p, pt: (b, 0, 0)),
            ] + [page_spec(IDX_DIM, r) for r in range(PAGES_PER_STEP)],
            out_specs=pl.BlockSpec((1, n_steps + 1, t, w), lambda b, p, pt: (b, 0, 0, 0)),
            scratch_shapes=[
                pltpu.VMEM((N_IDX_HEADS * t, 64), F32),
                pltpu.VMEM((N_IDX_HEADS, t, 128), F32),
                pltpu.VMEM((n_steps + 1, t, w), jnp.int32),
            ]),
        out_shape=jax.ShapeDtypeStruct((bsz, n_steps + 1, t, w), F32),
        compiler_params=pltpu.CompilerParams(
            dimension_semantics=("parallel", "arbitrary"), vmem_limit_bytes=VMEM_LIMIT_BYTES),
        name="dsa_sample_select",
    )(page_table, qi * IDX_DIM ** -0.5, wi, pad(ki), *([cache_kidx] * PAGES_PER_STEP))

    return pl.pallas_call(
        functools.partial(_sample_attend_kernel, t=t, n_steps=n_steps),
        grid_spec=pltpu.PrefetchScalarGridSpec(
            num_scalar_prefetch=1,
            grid=(bsz, n_steps),
            in_specs=[
                pl.BlockSpec((1, t, 1024), lambda b, p, pt: (b, 0, 0)),
                pl.BlockSpec((1, 1, t, w), lambda b, p, pt: (b, p, 0, 0)),
                pl.BlockSpec((1, 1, t, w), lambda b, p, pt: (b, n_steps, 0, 0)),
                pl.BlockSpec((1, PAGE_SIZE, 256), lambda b, p, pt: (b, 0, 0)),
                pl.BlockSpec((1, PAGE_SIZE, 256), lambda b, p, pt: (b, 0, 0)),
            ] + [page_spec(256, r) for r in range(PAGES_PER_STEP)] * 2,
            out_specs=pl.BlockSpec((1, t, 1024), lambda b, p, pt: (b, 0, 0)),
            scratch_shapes=[
                pltpu.VMEM((N_KV_HEADS, KV_GROUP * t, 64), F32),
                pltpu.VMEM((N_KV_HEADS, KV_GROUP * t, 128), F32),
                pltpu.VMEM((N_KV_HEADS, KV_GROUP * t, 128), F32),
                pltpu.VMEM((N_KV_HEADS, KV_GROUP * t, 64), F32),
            ]),
        out_shape=jax.ShapeDtypeStruct((bsz, t, 1024), F32),
        compiler_params=pltpu.CompilerParams(
            dimension_semantics=("parallel", "arbitrary"), vmem_limit_bytes=VMEM_LIMIT_BYTES),
        name="dsa_sample_attend",
    )(page_table, q * ATT_HEAD_DIM ** -0.5, bias, bias, pad(k), pad(v),
      *([ck] * PAGES_PER_STEP), *([cv] * PAGES_PER_STEP))


def layer_norm(x, g, b):
    xf = x.astype(F32)
    mu = jnp.mean(xf, -1, keepdims=True)
    var = jnp.mean(jnp.square(xf - mu), -1, keepdims=True)
    return ((xf - mu) * lax.rsqrt(var + LN_EPS) * g.astype(F32) + b.astype(F32)).astype(x.dtype)


def rope(x, pos):
    d = x.shape[-1]
    inv_freq = ROPE_THETA ** (-jnp.arange(0, d, 2, dtype=F32) / d)
    ang = pos.astype(F32)[:, None] * inv_freq[None, :]
    cos = jnp.cos(ang)[:, None, :]
    sin = jnp.sin(ang)[:, None, :]
    x1, x2 = jnp.split(x.astype(F32), 2, axis=-1)
    return jnp.concatenate([x1 * cos - x2 * sin, x1 * sin + x2 * cos], -1).astype(x.dtype)


def mem_attend(x, w_q, w_o, mk, mv):
    bsz, t = x.shape[:2]
    q = (x @ w_q).reshape(bsz, t, N_MEM_HEADS, MEM_HEAD_DIM)
    logits = jnp.einsum('bthd,bmhd->bhtm', q.astype(F32), mk.astype(F32)) * MEM_HEAD_DIM ** -0.5
    p = jax.nn.softmax(logits, axis=-1)
    o = jnp.einsum('bhtm,bmhd->bthd', p, mv.astype(F32)).reshape(bsz, t, D_MODEL)
    return o.astype(x.dtype) @ w_o


def ssd_chunked(xdt, a, bm, cm, h0, chunk):
    bsz, L, G, R, P = xdt.shape
    N = bm.shape[-1]
    nc = L // chunk
    x = xdt.reshape(bsz, nc, chunk, G, R, P)
    a = a.reshape(bsz, nc, chunk, G, R)
    bm = bm.reshape(bsz, nc, chunk, G, N)
    cm = cm.reshape(bsz, nc, chunk, G, N)
    acum = jnp.cumsum(a, axis=2)
    acum_t = jnp.moveaxis(acum, 2, -1)
    seg = acum_t[..., :, None] - acum_t[..., None, :]
    causal = jnp.tril(jnp.ones((chunk, chunk), dtype=bool))
    lmat = jnp.exp(jnp.where(causal, seg, -jnp.inf))
    cb = jnp.einsum('bclgn,bcsgn->bcgls', cm, bm)
    y_diag = jnp.einsum('bcgrls,bcsgrp->bclgrp', cb[:, :, :, None] * lmat, x)
    decay = jnp.exp(acum[:, :, -1:] - acum)
    states = jnp.einsum('bclgn,bclgrp->bcgrpn', bm, x * decay[..., None])
    chunk_decay = jnp.exp(acum[:, :, -1])

    def step(h, inp):
        st, dec = inp
        return h * dec[..., None, None] + st, h

    h_last, h_prev = lax.scan(step, h0, (jnp.moveaxis(states, 1, 0), jnp.moveaxis(chunk_decay, 1, 0)))
    h_prev = jnp.moveaxis(h_prev, 0, 1)
    y_off = jnp.einsum('bclgn,bcgrpn->bclgrp', cm, h_prev) * jnp.exp(acum)[..., None]
    return (y_diag + y_off).reshape(bsz, L, G, R, P), h_last


def mamba_mixer(x, w_in, conv_w, conv_b, dt_bias, a_log, d_skip, norm_w, w_out, conv_state, ssm_state):
    bsz, t = x.shape[:2]
    z, xbc, dt = jnp.split(x @ w_in, [D_INNER, D_INNER + CONV_DIM], axis=-1)
    xbc_ext = jnp.concatenate([conv_state.astype(xbc.dtype), xbc], axis=1)
    conv = conv_b
    for w in range(CONV_W):
        conv = conv + xbc_ext[:, w:w + t] * conv_w[w]
    new_conv = xbc_ext[:, t:]
    xbc = jax.nn.silu(conv).astype(F32)
    xs, bm, cm = jnp.split(xbc, [D_INNER, D_INNER + SSM_GROUPS * D_STATE], axis=-1)
    xs = xs.reshape(bsz, t, SSM_GROUPS, HEADS_PER_GROUP, SSM_HEAD_DIM)
    bm = bm.reshape(bsz, t, SSM_GROUPS, D_STATE)
    cm = cm.reshape(bsz, t, SSM_GROUPS, D_STATE)
    dt = jax.nn.softplus(dt.astype(F32) + dt_bias.astype(F32)).reshape(bsz, t, SSM_GROUPS, HEADS_PER_GROUP)
    a = -jnp.exp(a_log.astype(F32)).reshape(SSM_GROUPS, HEADS_PER_GROUP)
    chunk = SSM_CHUNK if t % SSM_CHUNK == 0 else t
    h0 = ssm_state.astype(F32).reshape(bsz, SSM_GROUPS, HEADS_PER_GROUP, SSM_HEAD_DIM, D_STATE)
    y, h_new = ssd_chunked(xs * dt[..., None], dt * a, bm, cm, h0, chunk)
    y = (y + xs * d_skip.astype(F32).reshape(SSM_GROUPS, HEADS_PER_GROUP)[:, :, None]).reshape(bsz, t, D_INNER)
    yg = y * jax.nn.silu(z.astype(F32))
    yg = yg * lax.rsqrt(jnp.mean(jnp.square(yg), -1, keepdims=True) + LN_EPS) * norm_w.astype(F32)
    out = yg.astype(x.dtype) @ w_out
    return out, new_conv, h_new.reshape(bsz, N_SSM_HEADS, SSM_HEAD_DIM, D_STATE).astype(ssm_state.dtype)


def dsa_project(x, w_in, pos):
    bsz, t = x.shape[:2]
    cuts = list(np.cumsum(ATT_SPLITS)[:-1])
    q, k, v, qi, ki, wi = jnp.split(x @ w_in, cuts, axis=-1)
    q = rope(q.reshape(bsz, t, N_ATT_HEADS, ATT_HEAD_DIM), pos)
    k = rope(k.reshape(bsz, t, N_KV_HEADS, ATT_HEAD_DIM), pos)
    v = v.reshape(bsz, t, N_KV_HEADS, ATT_HEAD_DIM)
    qi = rope(qi.reshape(bsz, t, N_IDX_HEADS, IDX_DIM), pos)
    ki = rope(ki[:, :, None, :], pos)[:, :, 0, :]
    return q, k, v, qi, ki, wi


def dsa_select_attend(q, qi, wi, ki_all, q_pos, gather_kv, topk):
    bsz, t = q.shape[:2]
    L = ki_all.shape[1]
    s = jnp.einsum('bthd,bsd->bths', qi.astype(F32), ki_all.astype(F32)) * IDX_DIM ** -0.5
    score = jnp.einsum('bth,bths->bts', wi.astype(F32) * N_IDX_HEADS ** -0.5, jax.nn.relu(s))
    admissible = jnp.arange(L, dtype=jnp.int32)[None, :] <= q_pos[:, None]
    score = jnp.where(admissible[None], score, -jnp.inf)
    _, idx = lax.top_k(score, topk)
    valid = idx <= q_pos[None, :, None]
    k_sel, v_sel = gather_kv(idx)
    qg = q.reshape(bsz, t, N_KV_HEADS, KV_GROUP, ATT_HEAD_DIM)
    logits = jnp.einsum('btjgd,btsjd->btjgs', qg.astype(F32), k_sel.astype(F32)) * ATT_HEAD_DIM ** -0.5
    logits = jnp.where(valid[:, :, None, None, :], logits, -jnp.inf)
    p = jax.nn.softmax(logits, axis=-1)
    o = jnp.einsum('btjgs,btsjd->btjgd', p, v_sel.astype(F32))
    return o.reshape(bsz, t, N_ATT_HEADS * ATT_HEAD_DIM).astype(q.dtype)


def take_rows(a, i):
    return jax.vmap(lambda ab, ib: ab[ib])(a, i)


def dsa_prompt(x, w_in, w_out, pos):
    bsz, t = x.shape[:2]
    q, k, v, qi, ki, wi = dsa_project(x, w_in, pos)
    out = dsa_prompt_attend(q.reshape(bsz, t, -1), k.reshape(bsz, t, -1), v.reshape(bsz, t, -1),
                            qi.reshape(bsz, t, -1), ki, wi)
    return out @ w_out, k, v, ki


def dsa_sample(x, w_in, w_out, pos, cache_k, cache_v, cache_kidx, layer, page_table):
    bsz, t = x.shape[:2]
    q, k, v, qi, ki, wi = dsa_project(x, w_in, pos)
    out = dsa_sample_attend(q.reshape(bsz, t, -1), k.reshape(bsz, t, -1), v.reshape(bsz, t, -1),
                            qi.reshape(bsz, t, -1), ki, wi, cache_k, cache_v, cache_kidx, layer, page_table)
    return out @ w_out, k, v, ki


def run_trunk(x, mem_k, mem_v, conv0, ssm0, attend, p, tm):
    conv_out, ssm_out, k_out, v_out, ki_out = [], [], [], [], []
    bsz, t = x.shape[:2]
    for i in range(DEPTH):
        j = i // N_MIXERS
        if i % N_MIXERS == 0:
            h, cs, ss = mamba_mixer(x, p['w_ssm_in'][j], p['ssm_conv_w'][j], p['ssm_conv_b'][j], p['ssm_dt_bias'][j],
                                    p['ssm_a_log'][j], p['ssm_d'][j], p['ssm_norm_w'][j], p['w_ssm_out'][j],
                                    conv0[j], ssm0[j])
            conv_out.append(cs)
            ssm_out.append(ss)
        else:
            h, k, v, ki = attend(j, x)
            k_out.append(k)
            v_out.append(v)
            ki_out.append(ki)
        x = layer_norm(DEEPNORM_ALPHA * x + h, p['ln_g'][i, 0], p['ln_b'][i, 0])
        x = layer_norm(DEEPNORM_ALPHA * x + mem_attend(x, p['w_mem_q'][i], p['w_mem_out'][i], mem_k[i], mem_v[i]),
                       p['ln_g'][i, 1], p['ln_b'][i, 1])
        x = ffn_ln(x.reshape(bsz * t, D_MODEL), p['w_ffn_in_bf'][i], p['w_ffn_out_bf'][i],
                   p['ln_g'][i, 2], p['ln_b'][i, 2], tm=tm).reshape(bsz, t, D_MODEL)
    return x, jnp.stack(k_out), jnp.stack(v_out), jnp.stack(ki_out), jnp.stack(conv_out), jnp.stack(ssm_out)


def kernel(x_prompt, x_sample, cache_k, cache_v, cache_kidx, cache_mem_k, cache_mem_v, state_conv, state_ssm,
           page_table, mem_prompt, w_ssm_in, ssm_conv_w, ssm_conv_b, ssm_dt_bias, ssm_a_log, ssm_d, ssm_norm_w,
           w_ssm_out, w_att_in, w_att_out, w_mem_q, w_mem_kv, w_mem_out, w_ffn_in, w_ffn_out, ln_g, ln_b):
    params = {'w_ssm_in': w_ssm_in, 'ssm_conv_w': ssm_conv_w, 'ssm_conv_b': ssm_conv_b, 'ssm_dt_bias': ssm_dt_bias,
              'ssm_a_log': ssm_a_log, 'ssm_d': ssm_d, 'ssm_norm_w': ssm_norm_w, 'w_ssm_out': w_ssm_out,
              'w_mem_q': w_mem_q, 'w_mem_out': w_mem_out,
              'w_ffn_in_bf': w_ffn_in.astype(BF16), 'w_ffn_out_bf': w_ffn_out.astype(BF16),
              'ln_g': ln_g, 'ln_b': ln_b}
    bp, seq = x_prompt.shape[:2]
    pos_p = jnp.arange(seq, dtype=jnp.int32)
    mem_kv = jnp.einsum('bmd,lde->lbme', mem_prompt, w_mem_kv)
    memk_flat, memv_flat = jnp.split(mem_kv, 2, axis=-1)
    memk_prompt = memk_flat.reshape(DEPTH, bp, N_MEM, N_MEM_HEADS, MEM_HEAD_DIM)
    memv_prompt = memv_flat.reshape(DEPTH, bp, N_MEM, N_MEM_HEADS, MEM_HEAD_DIM)
    n_ssm = (DEPTH + 1) // 2
    conv0 = jnp.zeros((n_ssm, bp, CONV_W - 1, CONV_DIM), x_prompt.dtype)
    ssm0 = jnp.zeros((n_ssm, bp, N_SSM_HEADS, SSM_HEAD_DIM, D_STATE), x_prompt.dtype)
    attend_p = lambda j, x: dsa_prompt(x, w_att_in[j], w_att_out[j], pos_p)
    y_prompt, k_prompt, v_prompt, kidx_prompt, conv_prompt, ssm_prompt = run_trunk(
        x_prompt, memk_prompt, memv_prompt, conv0, ssm0, attend_p, params, tm=1024)
    past = page_table.shape[1] * PAGE_SIZE
    pos_s = past + jnp.arange(x_sample.shape[1], dtype=jnp.int32)
    attend_s = lambda j, x: dsa_sample(x, w_att_in[j], w_att_out[j], pos_s, cache_k, cache_v, cache_kidx, j,
                                       page_table)
    y_sample, k_sample, v_sample, kidx_sample, conv_sample, ssm_sample = run_trunk(
        x_sample, cache_mem_k, cache_mem_v, state_conv, state_ssm, attend_s, params, tm=256)
    return (y_prompt, y_sample, k_prompt, v_prompt, kidx_prompt, conv_prompt, ssm_prompt, memk_prompt, memv_prompt,
            k_sample, v_sample, kidx_sample, conv_sample, ssm_sample)
```

```python
import functools
import math

import jax
import jax.numpy as jnp
import numpy as np
from jax import lax
from jax.experimental import pallas as pl
from jax.experimental.pallas import tpu as pltpu

F32 = jnp.float32
BF16 = jnp.bfloat16

DEPTH = 4
N_MIXERS = 2
D_MODEL = 1024
PAGE_SIZE = 128

D_INNER = 2048
SSM_HEAD_DIM = 64
N_SSM_HEADS = 32
SSM_GROUPS = 4
HEADS_PER_GROUP = 8
D_STATE = 128
CONV_W = 4
CONV_DIM = D_INNER + 2 * SSM_GROUPS * D_STATE
SSM_CHUNK = 128

ATT_HEAD_DIM = 64
N_ATT_HEADS = 16
N_KV_HEADS = 4
KV_GROUP = 4
N_IDX_HEADS = 8
IDX_DIM = 64
TOPK_MAX = 256
Q_BLOCK = 128
ATT_SPLITS = [1024, 256, 256, 512, 64, 8]
ROPE_THETA = 10000.0

N_MEM = 256
N_MEM_HEADS = 4
MEM_HEAD_DIM = 256
FFN_HIDDEN = 2816

DEEPNORM_ALPHA = (2 * DEPTH) ** 0.25
LN_EPS = 1e-5

VMEM_LIMIT_BYTES = 48 * 1024 * 1024


def _layer_norm_rows(y, g, b):
    mu = jnp.mean(y, axis=-1, keepdims=True)
    d = y - mu
    var = jnp.mean(d * d, axis=-1, keepdims=True)
    return d * lax.rsqrt(var + LN_EPS) * g + b


def _ffn_ln_kernel(x_ref, wa_ref, wb_ref, wo_ref, g_ref, b_ref, o_ref, xb_ref, acc_ref):
    k = pl.program_id(1)

    @pl.when(k == 0)
    def _():
        xb_ref[...] = x_ref[...].astype(BF16)
        acc_ref[...] = jnp.zeros_like(acc_ref)

    xb = xb_ref[...]
    a = jnp.dot(xb, wa_ref[...], preferred_element_type=F32)
    b = jnp.dot(xb, wb_ref[...], preferred_element_type=F32)
    h = (a * jax.nn.sigmoid(a)) * b
    acc_ref[...] += jnp.dot(h.astype(BF16), wo_ref[...], preferred_element_type=F32)

    @pl.when(k == pl.num_programs(1) - 1)
    def _():
        y = DEEPNORM_ALPHA * x_ref[...] + acc_ref[...]
        o_ref[...] = _layer_norm_rows(y, g_ref[...], b_ref[...])


def ffn_ln(x, w_in, w_out, g, b, *, tm, th=256):
    m, d = x.shape
    hidden = w_out.shape[0]
    nh = hidden // th
    return pl.pallas_call(
        _ffn_ln_kernel,
        grid=(m // tm, nh),
        in_specs=[
            pl.BlockSpec((tm, d), lambda i, k: (i, 0)),
            pl.BlockSpec((d, th), lambda i, k: (0, k)),
            pl.BlockSpec((d, th), lambda i, k: (0, k + nh)),
            pl.BlockSpec((th, d), lambda i, k: (k, 0)),
            pl.BlockSpec((1, d), lambda i, k: (0, 0)),
            pl.BlockSpec((1, d), lambda i, k: (0, 0)),
        ],
        out_specs=pl.BlockSpec((tm, d), lambda i, k: (i, 0)),
        out_shape=jax.ShapeDtypeStruct((m, d), F32),
        scratch_shapes=[pltpu.VMEM((tm, d), BF16), pltpu.VMEM((tm, d), F32)],
        compiler_params=pltpu.CompilerParams(
            dimension_semantics=("parallel", "arbitrary"), vmem_limit_bytes=VMEM_LIMIT_BYTES),
        name="ffn_ln",
    )(x, w_in, w_in, w_out, g.reshape(1, d), b.reshape(1, d))


def _proj_kernel(x_ref, w_ref, o_ref, xb_ref):
    @pl.when(pl.program_id(1) == 0)
    def _():
        xb_ref[...] = x_ref[...].astype(BF16)

    o_ref[...] = jnp.dot(xb_ref[...], w_ref[...], preferred_element_type=F32)


def proj(x, w, *, tm, tn):
    m, kd = x.shape
    n = w.shape[1]
    return pl.pallas_call(
        _proj_kernel,
        grid=(m // tm, n // tn),
        in_specs=[pl.BlockSpec((tm, kd), lambda i, j: (i, 0)), pl.BlockSpec((kd, tn), lambda i, j: (0, j))],
        out_specs=pl.BlockSpec((tm, tn), lambda i, j: (i, j)),
        out_shape=jax.ShapeDtypeStruct((m, n), F32),
        scratch_shapes=[pltpu.VMEM((tm, kd), BF16)],
        compiler_params=pltpu.CompilerParams(
            dimension_semantics=("parallel", "arbitrary"), vmem_limit_bytes=VMEM_LIMIT_BYTES),
        name="proj",
    )(x, w)


def _out_ln_kernel(h_ref, w_ref, x_ref, g_ref, b_ref, o_ref):
    y = DEEPNORM_ALPHA * x_ref[...] + jnp.dot(h_ref[...].astype(BF16), w_ref[...], preferred_element_type=F32)
    o_ref[...] = _layer_norm_rows(y, g_ref[...], b_ref[...])


def out_ln(h, w, x, g, b, *, tm):
    m, kd = h.shape
    d = w.shape[1]
    return pl.pallas_call(
        _out_ln_kernel,
        grid=(m // tm,),
        in_specs=[
            pl.BlockSpec((tm, kd), lambda i: (i, 0)),
            pl.BlockSpec((kd, d), lambda i: (0, 0)),
            pl.BlockSpec((tm, d), lambda i: (i, 0)),
            pl.BlockSpec((1, d), lambda i: (0, 0)),
            pl.BlockSpec((1, d), lambda i: (0, 0)),
        ],
        out_specs=pl.BlockSpec((tm, d), lambda i: (i, 0)),
        out_shape=jax.ShapeDtypeStruct((m, d), F32),
        compiler_params=pltpu.CompilerParams(
            dimension_semantics=("parallel",), vmem_limit_bytes=VMEM_LIMIT_BYTES),
        name="out_ln",
    )(h, w, x, g.reshape(1, d), b.reshape(1, d))


def _mem_attn_kernel(q_ref, mk_ref, mv_ref, o_ref):
    for h in range(N_MEM_HEADS):
        sl = slice(h * MEM_HEAD_DIM, (h + 1) * MEM_HEAD_DIM)
        s = lax.dot_general(q_ref[0, :, sl].astype(BF16), mk_ref[:, h, :].astype(BF16), _NT,
                            preferred_element_type=F32) * MEM_HEAD_DIM ** -0.5
        e = jnp.exp(s - jnp.max(s, axis=1, keepdims=True))
        pv = jnp.dot(e.astype(BF16), mv_ref[:, h, :].astype(BF16), preferred_element_type=F32)
        o_ref[0, :, sl] = pv / jnp.sum(e, axis=1, keepdims=True)


def mem_attn(q, mk, mv, layer, *, tm):
    bsz, t, d = q.shape
    mem_spec = pl.BlockSpec((None, None, N_MEM, N_MEM_HEADS, MEM_HEAD_DIM), lambda b, i: (layer, b, 0, 0, 0))
    return pl.pallas_call(
        _mem_attn_kernel,
        grid=(bsz, t // tm),
        in_specs=[pl.BlockSpec((1, tm, d), lambda b, i: (b, i, 0)), mem_spec, mem_spec],
        out_specs=pl.BlockSpec((1, tm, d), lambda b, i: (b, i, 0)),
        out_shape=jax.ShapeDtypeStruct((bsz, t, d), F32),
        compiler_params=pltpu.CompilerParams(
            dimension_semantics=("parallel", "arbitrary"), vmem_limit_bytes=VMEM_LIMIT_BYTES),
        name="mem_attn",
    )(q, mk, mv)


CONV_COLS = 1024


def _conv_kernel(x_ref, st_ref, w_ref, b_ref, act_ref, last_ref, prev_ref, *, tc):
    @pl.when(pl.program_id(2) == 0)
    def _():
        prev_ref[...] = st_ref[0]

    x = x_ref[0]
    prev = prev_ref[...]
    row = lax.broadcasted_iota(jnp.int32, (8, CONV_COLS), 0)
    acc = jnp.broadcast_to(b_ref[...], x.shape)
    for s in (3, 2, 1):
        rolled = pltpu.roll(x, s, axis=0)
        top = jnp.where(row < s, pltpu.roll(prev, s, axis=0), rolled[0:8])
        shifted = top if tc == 8 else jnp.concatenate([top, rolled[8:]], axis=0)
        acc = acc + shifted * w_ref[CONV_W - 1 - s:CONV_W - s, :]
    acc = acc + x * w_ref[CONV_W - 1:CONV_W, :]
    act_ref[0] = acc * jax.nn.sigmoid(acc)
    prev_ref[...] = x[tc - 8:tc]
    last_ref[0] = x[tc - 8:tc]


def ssm_conv(zx, conv_state, conv_w, conv_b, *, tc):
    bsz, t = zx.shape[:2]
    ncb = CONV_DIM // CONV_COLS
    col0 = D_INNER // CONV_COLS
    st8 = jnp.pad(conv_state, ((0, 0), (8 - (CONV_W - 1), 0), (0, 0)))
    act, last = pl.pallas_call(
        functools.partial(_conv_kernel, tc=tc),
        grid=(bsz, ncb, t // tc),
        in_specs=[
            pl.BlockSpec((1, tc, CONV_COLS), lambda b, c, i: (b, i, col0 + c)),
            pl.BlockSpec((1, 8, CONV_COLS), lambda b, c, i: (b, 0, c)),
            pl.BlockSpec((CONV_W, CONV_COLS), lambda b, c, i: (0, c)),
            pl.BlockSpec((1, CONV_COLS), lambda b, c, i: (0, c)),
        ],
        out_specs=[
            pl.BlockSpec((1, tc, CONV_COLS), lambda b, c, i: (b, i, c)),
            pl.BlockSpec((1, 8, CONV_COLS), lambda b, c, i: (b, 0, c)),
        ],
        out_shape=[jax.ShapeDtypeStruct((bsz, t, CONV_DIM), F32), jax.ShapeDtypeStruct((bsz, 8, CONV_DIM), F32)],
        scratch_shapes=[pltpu.VMEM((8, CONV_COLS), F32)],
        compiler_params=pltpu.CompilerParams(
            dimension_semantics=("parallel", "parallel", "arbitrary"), vmem_limit_bytes=VMEM_LIMIT_BYTES),
        name="ssm_conv",
    )(zx, st8, conv_w, conv_b.reshape(1, CONV_DIM))
    return act, last[:, 8 - (CONV_W - 1):, :]


def _split3(v):
    hi = v.astype(BF16)
    r1 = v - hi.astype(F32)
    mid = r1.astype(BF16)
    lo = (r1 - mid.astype(F32)).astype(BF16)
    return hi, mid, lo


def _dot01_right(v, ones_mat):
    return sum(jnp.dot(p, ones_mat, preferred_element_type=F32) for p in _split3(v))


def _dot01_left(ones_mat, v):
    return sum(jnp.dot(ones_mat, p, preferred_element_type=F32) for p in _split3(v))


def _softplus(x):
    return jnp.maximum(x, 0.0) + jnp.log1p(jnp.exp(-jnp.abs(x)))


def _ssd_kernel(xs_ref, b_ref, c_ref, z_ref, dt_ref, dtt_ref, dtb_ref, dtbt_ref, alog_ref, alogt_ref,
                d_ref, nw_ref, h0_ref, y_ref, hT_ref, st_ref, yacc_ref, *, c):
    ci = pl.program_id(1)

    @pl.when(ci == 0)
    def _():
        st_ref[...] = h0_ref[0]

    gw = HEADS_PER_GROUP * SSM_HEAD_DIM
    dt = _softplus(dt_ref[0][:, 0:N_SSM_HEADS] + dtb_ref[...])
    dtt = _softplus(dtt_ref[0] + dtbt_ref[...])
    a = dt * -jnp.exp(alog_ref[...])
    at = dtt * -jnp.exp(alogt_ref[...])
    ri = lax.broadcasted_iota(jnp.int32, (c, c), 0)
    cj = lax.broadcasted_iota(jnp.int32, (c, c), 1)
    causal = ri >= cj
    acum = _dot01_left(jnp.where(causal, 1.0, 0.0).astype(BF16), a)
    acum_t = _dot01_right(at, jnp.where(ri <= cj, 1.0, 0.0).astype(BF16))
    a_last = acum[c - 1:c, :]
    expand = jnp.where(lax.broadcasted_iota(jnp.int32, (N_SSM_HEADS, D_INNER), 1) // SSM_HEAD_DIM
                       == lax.broadcasted_iota(jnp.int32, (N_SSM_HEADS, D_INNER), 0), 1.0, 0.0).astype(BF16)
    e_dt = _dot01_right(dt, expand)
    e_in = _dot01_right(jnp.exp(acum), expand)
    e_out = _dot01_right(jnp.exp(a_last - acum), expand)
    e_chunk = e_in[c - 1:c, :]
    xs = xs_ref[0]
    xdt = xs * e_dt
    head_of_col = lax.broadcasted_iota(jnp.int32, (c, gw), 1) // SSM_HEAD_DIM
    for g in range(SSM_GROUPS):
        gs = slice(g * gw, (g + 1) * gw)
        bg = b_ref[0][:, g * D_STATE:(g + 1) * D_STATE].astype(BF16)
        cg = c_ref[0][:, g * D_STATE:(g + 1) * D_STATE].astype(BF16)
        cb = lax.dot_general(cg, bg, _NT, preferred_element_type=F32)
        xg = xdt[:, gs]
        yg = jnp.zeros((c, gw), F32)
        for r in range(HEADS_PER_GROUP):
            h = g * HEADS_PER_GROUP + r
            seg = acum[:, h:h + 1] - acum_t[h:h + 1, :]
            m = (cb * jnp.exp(jnp.where(causal, seg, -jnp.inf))).astype(BF16)
            xm = jnp.where(head_of_col == r, xg, 0.0).astype(BF16)
            yg = yg + jnp.dot(m, xm, preferred_element_type=F32)
        state = st_ref[g]
        y_off = jnp.dot(cg, state.astype(BF16), preferred_element_type=F32) * e_in[:, gs]
        yacc_ref[:, gs] = yg + y_off + xs[:, gs] * d_ref[:, gs]
        xd = (xg * e_out[:, gs]).astype(BF16)
        st_ref[g] = state * e_chunk[:, gs] + lax.dot_general(bg, xd, (((0,), (0,)), ((), ())),
                                                             preferred_element_type=F32)
    z = z_ref[0]
    yz = yacc_ref[...] * (z * jax.nn.sigmoid(z))
    y_ref[0] = yz * lax.rsqrt(jnp.mean(yz * yz, axis=-1, keepdims=True) + LN_EPS) * nw_ref[...]

    @pl.when(ci == pl.num_programs(1) - 1)
    def _():
        hT_ref[0] = st_ref[...]


def ssd_scan(act, zx, dt_raw, dt_bias, a_log, d_skip, norm_w, ssm_state, *, c):
    bsz, t = act.shape[:2]
    gw = HEADS_PER_GROUP * SSM_HEAD_DIM
    bc_w = SSM_GROUPS * D_STATE
    h0 = ssm_state.reshape(bsz, SSM_GROUPS, HEADS_PER_GROUP, SSM_HEAD_DIM, D_STATE)
    h0 = h0.transpose(0, 1, 4, 2, 3).reshape(bsz, SSM_GROUPS, D_STATE, gw)
    dtt = jnp.swapaxes(dt_raw[..., :N_SSM_HEADS], 1, 2)
    row = lambda v: v.reshape(1, -1)
    col = lambda v: v.reshape(-1, 1)
    full = lambda shape: pl.BlockSpec(shape, lambda b, i: (0,) * len(shape))
    y, h_t = pl.pallas_call(
        functools.partial(_ssd_kernel, c=c),
        grid=(bsz, t // c),
        in_specs=[
            pl.BlockSpec((1, c, D_INNER), lambda b, i: (b, i, 0)),
            pl.BlockSpec((1, c, bc_w), lambda b, i: (b, i, D_INNER // bc_w)),
            pl.BlockSpec((1, c, bc_w), lambda b, i: (b, i, D_INNER // bc_w + 1)),
            pl.BlockSpec((1, c, D_INNER), lambda b, i: (b, i, 0)),
            pl.BlockSpec((1, c, 128), lambda b, i: (b, i, 0)),
            pl.BlockSpec((1, N_SSM_HEADS, c), lambda b, i: (b, 0, i)),
            full((1, N_SSM_HEADS)), full((N_SSM_HEADS, 1)), full((1, N_SSM_HEADS)), full((N_SSM_HEADS, 1)),
            full((1, D_INNER)), full((1, D_INNER)),
            pl.BlockSpec((1, SSM_GROUPS, D_STATE, gw), lambda b, i: (b, 0, 0, 0)),
        ],
        out_specs=[
            pl.BlockSpec((1, c, D_INNER), lambda b, i: (b, i, 0)),
            pl.BlockSpec((1, SSM_GROUPS, D_STATE, gw), lambda b, i: (b, 0, 0, 0)),
        ],
        out_shape=[jax.ShapeDtypeStruct((bsz, t, D_INNER), F32),
                   jax.ShapeDtypeStruct((bsz, SSM_GROUPS, D_STATE, gw), F32)],
        scratch_shapes=[pltpu.VMEM((SSM_GROUPS, D_STATE, gw), F32), pltpu.VMEM((c, D_INNER), F32)],
        compiler_params=pltpu.CompilerParams(
            dimension_semantics=("parallel", "arbitrary"), vmem_limit_bytes=VMEM_LIMIT_BYTES),
        name="ssd_scan",
    )(act, act, act, zx, dt_raw, dtt, row(dt_bias), col(dt_bias), row(a_log), col(a_log),
      row(jnp.repeat(d_skip, SSM_HEAD_DIM)), row(norm_w), h0)
    h_t = h_t.reshape(bsz, SSM_GROUPS, D_STATE, HEADS_PER_GROUP, SSM_HEAD_DIM)
    return y, h_t.transpose(0, 1, 3, 4, 2).reshape(bsz, N_SSM_HEADS, SSM_HEAD_DIM, D_STATE)


INT_MIN = -2 ** 31
NEG_BIG = -1e30
_NT = (((1,), (1,)), ((), ()))


def _float_key(x):
    x = jnp.where(x == 0.0, 0.0, x)
    bits = lax.bitcast_convert_type(x, jnp.int32)
    return bits ^ ((bits >> 31) & 0x7FFFFFFF)


def _dsa_prompt_kernel(q_ref, qi_ref, wi_ref, k_ref, vx_ref, ki_ref, o_ref,
                       qs_ref, qis_ref, wib_ref, key_ref, m_ref, acc_ref, *, tq, tk, topk):
    i = pl.program_id(1)
    nkb = ((i + 1) * tq + tk - 1) // tk
    nl = tk // 128

    def lanes(x):
        return jnp.concatenate([x] * nl, axis=1)

    for h in range(N_ATT_HEADS):
        j, g = divmod(h, KV_GROUP)
        qs_ref[j, g * tq:(g + 1) * tq, :] = q_ref[0, :, h * 64:(h + 1) * 64]
    for h in range(N_IDX_HEADS):
        qis_ref[h * tq:(h + 1) * tq, :] = qi_ref[0, :, h * 64:(h + 1) * 64]
        wib_ref[h] = jnp.broadcast_to(wi_ref[0, :, h:h + 1] * (N_IDX_HEADS ** -0.5), (tq, 128))
    q_pos = i * tq + lax.broadcasted_iota(jnp.int32, (tq, 1), 0)

    def score_block(kb, c):
        off = pl.multiple_of(kb * tk, tk)
        s = lax.dot_general(qis_ref[...], ki_ref[0, pl.ds(off, tk), :], _NT, preferred_element_type=F32)
        acc = jnp.zeros((tq, tk), F32)
        for h in range(N_IDX_HEADS):
            acc = acc + lanes(wib_ref[h]) * jnp.maximum(s[h * tq:(h + 1) * tq, :], 0.0)
        kpos = off + lax.broadcasted_iota(jnp.int32, (tq, tk), 1)
        key_ref[kb] = jnp.where(kpos <= q_pos, _float_key(acc), INT_MIN)
        return c

    lax.fori_loop(0, nkb, score_block, 0)

    def count(cand, strict):
        def body(kb, acc):
            blk = key_ref[kb]
            for c in range(nl):
                part = blk[:, c * 128:(c + 1) * 128]
                hit = (part > cand) if strict else (part >= cand)
                acc = acc + jnp.where(hit, 1.0, 0.0)
            return acc
        acc = lax.fori_loop(0, nkb, body, jnp.zeros((tq, 128), F32))
        return jnp.sum(acc, axis=1, keepdims=True)

    kf = float(topk)
    zero = jnp.zeros((tq, 128), jnp.int32)
    thr = jnp.where(count(zero, False) >= kf, zero, jnp.full((tq, 128), INT_MIN, jnp.int32))

    def bit_step(it, thr):
        cand = thr | jnp.left_shift(jnp.int32(1), 30 - it)
        return jnp.where(count(cand, False) >= kf, cand, thr)

    thr = lax.fori_loop(0, 31, bit_step, thr)
    need = jnp.where(thr == INT_MIN, 0.0, kf - count(thr, True))

    m_ref[...] = jnp.full(m_ref.shape, NEG_BIG, F32)
    acc_ref[...] = jnp.zeros(acc_ref.shape, F32)
    upper = jnp.where(lax.broadcasted_iota(jnp.int32, (tk, tk), 0) < lax.broadcasted_iota(jnp.int32, (tk, tk), 1),
                      1.0, 0.0).astype(BF16)
    thr_t, need_t = lanes(thr), lanes(need)

    def attend_block(kb, eq_before):
        off = pl.multiple_of(kb * tk, tk)
        key = key_ref[kb]
        eq = key == thr_t
        eqf = jnp.where(eq, 1.0, 0.0)
        rank = jnp.dot(eqf.astype(BF16), upper, preferred_element_type=F32) + lanes(eq_before)
        bias = jnp.where(key > thr_t, 0.0, jnp.where(eq, jnp.where(rank < need_t, 0.0, NEG_BIG), NEG_BIG))
        bias4 = jnp.concatenate([bias] * KV_GROUP, axis=0)
        for j in range(N_KV_HEADS):
            kblk = k_ref[0, pl.ds(off, tk), j * 64:(j + 1) * 64]
            s = lax.dot_general(qs_ref[j], kblk, _NT, preferred_element_type=F32) + bias4
            m_old = m_ref[j]
            mx = s[:, 0:128]
            for c in range(1, nl):
                mx = jnp.maximum(mx, s[:, c * 128:(c + 1) * 128])
            m_new = jnp.maximum(m_old, jnp.max(mx, axis=1, keepdims=True))
            p = jnp.exp(s - lanes(m_new))
            pv = jnp.dot(p.astype(BF16), vx_ref[0, j, pl.ds(off, tk), :], preferred_element_type=F32)
            acc_ref[j] = jnp.exp(m_old - m_new) * acc_ref[j] + pv
            m_ref[j] = m_new
        return eq_before + jnp.sum(eqf, axis=1, keepdims=True)

    lax.fori_loop(0, nkb, attend_block, jnp.zeros((tq, 128), F32))

    for h in range(N_ATT_HEADS):
        j, g = divmod(h, KV_GROUP)
        a = acc_ref[j, g * tq:(g + 1) * tq, :]
        o_ref[0, :, h * 64:(h + 1) * 64] = a[:, 0:64] / a[:, 64:65]


def dsa_prompt_attend(q, k, v, qi, ki, wi, *, tq=128, tk=256):
    bsz, t = q.shape[:2]
    topk = min(TOPK_MAX, t // 4)
    q_bf = (q * ATT_HEAD_DIM ** -0.5).astype(BF16)
    qi_bf = (qi * IDX_DIM ** -0.5).astype(BF16)
    vh = v.reshape(bsz, t, N_KV_HEADS, ATT_HEAD_DIM).transpose(0, 2, 1, 3)
    vx = jnp.concatenate([vh, jnp.ones_like(vh)], axis=-1).astype(BF16)
    kern = functools.partial(_dsa_prompt_kernel, tq=tq, tk=tk, topk=topk)
    return pl.pallas_call(
        kern,
        grid=(bsz, t // tq),
        in_specs=[
            pl.BlockSpec((1, tq, 1024), lambda b, i: (b, i, 0)),
            pl.BlockSpec((1, tq, 512), lambda b, i: (b, i, 0)),
            pl.BlockSpec((1, tq, N_IDX_HEADS), lambda b, i: (b, i, 0)),
            pl.BlockSpec((1, t, 256), lambda b, i: (b, 0, 0)),
            pl.BlockSpec((1, N_KV_HEADS, t, 128), lambda b, i: (b, 0, 0, 0)),
            pl.BlockSpec((1, t, IDX_DIM), lambda b, i: (b, 0, 0)),
        ],
        out_specs=pl.BlockSpec((1, tq, 1024), lambda b, i: (b, i, 0)),
        out_shape=jax.ShapeDtypeStruct((bsz, t, 1024), F32),
        scratch_shapes=[
            pltpu.VMEM((N_KV_HEADS, KV_GROUP * tq, 64), BF16),
            pltpu.VMEM((N_IDX_HEADS * tq, 64), BF16),
            pltpu.VMEM((N_IDX_HEADS, tq, 128), F32),
            pltpu.VMEM((t // tk, tq, tk), jnp.int32),
            pltpu.VMEM((N_KV_HEADS, KV_GROUP * tq, 128), F32),
            pltpu.VMEM((N_KV_HEADS, KV_GROUP * tq, 128), F32),
        ],
        compiler_params=pltpu.CompilerParams(
            dimension_semantics=("parallel", "arbitrary"), vmem_limit_bytes=VMEM_LIMIT_BYTES),
        name="dsa_prompt_attend",
    )(q_bf, qi_bf, wi, k.astype(BF16), vx, ki.astype(BF16))


PAGES_PER_STEP = 8


def _sample_select_kernel(pt_ref, qi_ref, wi_ref, kinew_ref, *rest, t, n_steps, topk):
    ki_refs = rest[:PAGES_PER_STEP]
    bias_ref, qis_ref, wib_ref, key_ref = rest[PAGES_PER_STEP:]
    p = pl.program_id(1)
    w = PAGES_PER_STEP * PAGE_SIZE
    nl = w // 128

    @pl.when(p == 0)
    def _():
        for h in range(N_IDX_HEADS):
            qis_ref[h * t:(h + 1) * t, :] = qi_ref[0, :, h * 64:(h + 1) * 64]
            wib_ref[h] = jnp.broadcast_to(wi_ref[0, :, h:h + 1] * (N_IDX_HEADS ** -0.5), (t, 128))

    def scores(ki_rows):
        s = lax.dot_general(qis_ref[...].astype(BF16), ki_rows.astype(BF16), _NT, preferred_element_type=F32)
        acc = jnp.zeros((t, ki_rows.shape[0]), F32)
        for h in range(N_IDX_HEADS):
            wfull = jnp.concatenate([wib_ref[h]] * (ki_rows.shape[0] // 128), axis=1)
            acc = acc + wfull * jnp.maximum(s[h * t:(h + 1) * t, :], 0.0)
        return acc

    key_ref[p] = _float_key(scores(jnp.concatenate([r[...] for r in ki_refs], axis=0)))

    @pl.when(p == n_steps - 1)
    def _():
        knew = _float_key(scores(kinew_ref[0]))
        n_idx = lax.broadcasted_iota(jnp.int32, (t, 128), 1)
        r_idx = lax.broadcasted_iota(jnp.int32, (t, 128), 0)
        knew = jnp.where(n_idx <= r_idx, knew, INT_MIN)
        key_ref[n_steps] = jnp.concatenate([knew, jnp.full((t, w - 128), INT_MIN, jnp.int32)], axis=1)

        def count(cand, strict):
            def body(blk, acc):
                kb = key_ref[blk]
                for c in range(nl):
                    part = kb[:, c * 128:(c + 1) * 128]
                    hit = (part > cand) if strict else (part >= cand)
                    acc = acc + jnp.where(hit, 1.0, 0.0)
                return acc
            acc = lax.fori_loop(0, n_steps + 1, body, jnp.zeros((t, 128), F32))
            return jnp.sum(acc, axis=1, keepdims=True)

        kf = float(topk)
        zero = jnp.zeros((t, 128), jnp.int32)
        thr = jnp.where(count(zero, False) >= kf, zero, jnp.full((t, 128), INT_MIN, jnp.int32))

        def bit_step(it, thr):
            cand = thr | jnp.left_shift(jnp.int32(1), 30 - it)
            return jnp.where(count(cand, False) >= kf, cand, thr)

        thr = lax.fori_loop(0, 31, bit_step, thr)
        need = jnp.where(thr == INT_MIN, 0.0, kf - count(thr, True))
        upper = jnp.where(lax.broadcasted_iota(jnp.int32, (128, 128), 0)
                          < lax.broadcasted_iota(jnp.int32, (128, 128), 1), 1.0, 0.0).astype(BF16)

        def emit(blk, eq_before):
            kb = key_ref[blk]
            parts = []
            for c in range(nl):
                part = kb[:, c * 128:(c + 1) * 128]
                eq = part == thr
                eqf = jnp.where(eq, 1.0, 0.0)
                rank = jnp.dot(eqf.astype(BF16), upper, preferred_element_type=F32) + eq_before
                parts.append(jnp.where(part > thr, 0.0,
                                       jnp.where(eq, jnp.where(rank < need, 0.0, NEG_BIG), NEG_BIG)))
                eq_before = eq_before + jnp.sum(eqf, axis=1, keepdims=True)
            bias_ref[0, blk] = jnp.concatenate(parts, axis=1)
            return eq_before

        lax.fori_loop(0, n_steps + 1, emit, jnp.zeros((t, 128), F32))


def _sample_attend_kernel(pt_ref, q_ref, bias_ref, biasnew_ref, knew_ref, vnew_ref, *rest, t, n_steps):
    k_refs = rest[:PAGES_PER_STEP]
    v_refs = rest[PAGES_PER_STEP:2 * PAGES_PER_STEP]
    o_ref, qs_ref, m_ref, l_ref, acc_ref = rest[2 * PAGES_PER_STEP:]
    p = pl.program_id(1)
    rows = KV_GROUP * t

    @pl.when(p == 0)
    def _():
        for h in range(N_ATT_HEADS):
            j, g = divmod(h, KV_GROUP)
            qs_ref[j, g * t:(g + 1) * t, :] = q_ref[0, :, h * 64:(h + 1) * 64]
        m_ref[...] = jnp.full(m_ref.shape, NEG_BIG, F32)
        l_ref[...] = jnp.zeros(l_ref.shape, F32)
        acc_ref[...] = jnp.zeros(acc_ref.shape, F32)

    def update(k_head, v_head, bias):
        bias4 = jnp.concatenate([bias] * KV_GROUP, axis=0)
        for j in range(N_KV_HEADS):
            s = lax.dot_general(qs_ref[j].astype(BF16), k_head(j).astype(BF16), _NT,
                                preferred_element_type=F32) + bias4
            m_old = m_ref[j]
            m_new = jnp.maximum(m_old, jnp.max(s, axis=1, keepdims=True))
            alpha = jnp.exp(m_old - m_new)
            pr = jnp.exp(s - m_new[:, 0:1])
            l_ref[j] = alpha * l_ref[j] + jnp.sum(pr, axis=1, keepdims=True)
            pv = jnp.dot(pr.astype(BF16), v_head(j).astype(BF16), preferred_element_type=F32)
            acc_ref[j] = alpha[:, 0:64] * acc_ref[j] + pv
            m_ref[j] = m_new

    update(lambda j: jnp.concatenate([r[:, j, :] for r in k_refs], axis=0),
           lambda j: jnp.concatenate([r[:, j, :] for r in v_refs], axis=0), bias_ref[0, 0])

    @pl.when(p == n_steps - 1)
    def _():
        update(lambda j: knew_ref[0, :, j * 64:(j + 1) * 64], lambda j: vnew_ref[0, :, j * 64:(j + 1) * 64],
               biasnew_ref[0, 0, :, 0:128])
        for h in range(N_ATT_HEADS):
            j, g = divmod(h, KV_GROUP)
            sl = slice(g * t, (g + 1) * t)
            o_ref[0, :, h * 64:(h + 1) * 64] = acc_ref[j, sl, :] / l_ref[j, sl, 0:64]


def dsa_sample_attend(q, k, v, qi, ki, wi, cache_k, cache_v, cache_kidx, layer, page_table):
    bsz, t = q.shape[:2]
    n_pages = page_table.shape[1]
    past = n_pages * PAGE_SIZE
    n_steps = n_pages // PAGES_PER_STEP
    w = PAGES_PER_STEP * PAGE_SIZE
    topk = min(TOPK_MAX, (past + t) // 4)
    pad = lambda a: jnp.pad(a, ((0, 0), (0, PAGE_SIZE - t), (0, 0)))

    def page_spec(r, *minor):
        zeros = (0,) * (1 + len(minor))
        return pl.BlockSpec((None, None, PAGE_SIZE) + minor,
                            lambda b, p, pt: (layer, pt[b, p * PAGES_PER_STEP + r]) + zeros)

    bias = pl.pallas_call(
        functools.partial(_sample_select_kernel, t=t, n_steps=n_steps, topk=topk),
        grid_spec=pltpu.PrefetchScalarGridSpec(
            num_scalar_prefetch=1,
            grid=(bsz, n_steps),
            in_specs=[
                pl.BlockSpec((1, t, 512), lambda b, p, pt: (b, 0, 0)),
                pl.BlockSpec((1, t, N_IDX_HEADS), lambda b, p, pt: (b, 0, 0)),
                pl.BlockSpec((1, PAGE_SIZE, IDX_DIM), lambda b, p, pt: (b, 0, 0)),
            ] + [page_spec(r, IDX_DIM) for r in range(PAGES_PER_STEP)],
            out_specs=pl.BlockSpec((1, n_steps + 1, t, w), lambda b, p, pt: (b, 0, 0, 0)),
            scratch_shapes=[
                pltpu.VMEM((N_IDX_HEADS * t, 64), F32),
                pltpu.VMEM((N_IDX_HEADS, t, 128), F32),
                pltpu.VMEM((n_steps + 1, t, w), jnp.int32),
            ]),
        out_shape=jax.ShapeDtypeStruct((bsz, n_steps + 1, t, w), F32),
        compiler_params=pltpu.CompilerParams(
            dimension_semantics=("parallel", "arbitrary"), vmem_limit_bytes=VMEM_LIMIT_BYTES),
        name="dsa_sample_select",
    )(page_table, qi * IDX_DIM ** -0.5, wi, pad(ki), *([cache_kidx] * PAGES_PER_STEP))

    return pl.pallas_call(
        functools.partial(_sample_attend_kernel, t=t, n_steps=n_steps),
        grid_spec=pltpu.PrefetchScalarGridSpec(
            num_scalar_prefetch=1,
            grid=(bsz, n_steps),
            in_specs=[
                pl.BlockSpec((1, t, 1024), lambda b, p, pt: (b, 0, 0)),
                pl.BlockSpec((1, 1, t, w), lambda b, p, pt: (b, p, 0, 0)),
                pl.BlockSpec((1, 1, t, w), lambda b, p, pt: (b, n_steps, 0, 0)),
                pl.BlockSpec((1, PAGE_SIZE, 256), lambda b, p, pt: (b, 0, 0)),
                pl.BlockSpec((1, PAGE_SIZE, 256), lambda b, p, pt: (b, 0, 0)),
            ] + [page_spec(r, N_KV_HEADS, ATT_HEAD_DIM) for r in range(PAGES_PER_STEP)] * 2,
            out_specs=pl.BlockSpec((1, t, 1024), lambda b, p, pt: (b, 0, 0)),
            scratch_shapes=[
                pltpu.VMEM((N_KV_HEADS, KV_GROUP * t, 64), F32),
                pltpu.VMEM((N_KV_HEADS, KV_GROUP * t, 128), F32),
                pltpu.VMEM((N_KV_HEADS, KV_GROUP * t, 128), F32),
                pltpu.VMEM((N_KV_HEADS, KV_GROUP * t, 64), F32),
            ]),
        out_shape=jax.ShapeDtypeStruct((bsz, t, 1024), F32),
        compiler_params=pltpu.CompilerParams(
            dimension_semantics=("parallel", "arbitrary"), vmem_limit_bytes=VMEM_LIMIT_BYTES),
        name="dsa_sample_attend",
    )(page_table, q * ATT_HEAD_DIM ** -0.5, bias, bias, pad(k), pad(v),
      *([cache_k] * PAGES_PER_STEP), *([cache_v] * PAGES_PER_STEP))


def layer_norm(x, g, b):
    xf = x.astype(F32)
    mu = jnp.mean(xf, -1, keepdims=True)
    var = jnp.mean(jnp.square(xf - mu), -1, keepdims=True)
    return ((xf - mu) * lax.rsqrt(var + LN_EPS) * g.astype(F32) + b.astype(F32)).astype(x.dtype)


def rope(x, pos):
    d = x.shape[-1]
    inv_freq = ROPE_THETA ** (-jnp.arange(0, d, 2, dtype=F32) / d)
    ang = pos.astype(F32)[:, None] * inv_freq[None, :]
    cos = jnp.cos(ang)[:, None, :]
    sin = jnp.sin(ang)[:, None, :]
    x1, x2 = jnp.split(x.astype(F32), 2, axis=-1)
    return jnp.concatenate([x1 * cos - x2 * sin, x1 * sin + x2 * cos], -1).astype(x.dtype)


def ssd_chunked(xdt, a, bm, cm, h0, chunk):
    bsz, L, G, R, P = xdt.shape
    N = bm.shape[-1]
    nc = L // chunk
    x = xdt.reshape(bsz, nc, chunk, G, R, P)
    a = a.reshape(bsz, nc, chunk, G, R)
    bm = bm.reshape(bsz, nc, chunk, G, N)
    cm = cm.reshape(bsz, nc, chunk, G, N)
    acum = jnp.cumsum(a, axis=2)
    acum_t = jnp.moveaxis(acum, 2, -1)
    seg = acum_t[..., :, None] - acum_t[..., None, :]
    causal = jnp.tril(jnp.ones((chunk, chunk), dtype=bool))
    lmat = jnp.exp(jnp.where(causal, seg, -jnp.inf))
    cb = jnp.einsum('bclgn,bcsgn->bcgls', cm, bm)
    y_diag = jnp.einsum('bcgrls,bcsgrp->bclgrp', cb[:, :, :, None] * lmat, x)
    decay = jnp.exp(acum[:, :, -1:] - acum)
    states = jnp.einsum('bclgn,bclgrp->bcgrpn', bm, x * decay[..., None])
    chunk_decay = jnp.exp(acum[:, :, -1])

    def step(h, inp):
        st, dec = inp
        return h * dec[..., None, None] + st, h

    h_last, h_prev = lax.scan(step, h0, (jnp.moveaxis(states, 1, 0), jnp.moveaxis(chunk_decay, 1, 0)))
    h_prev = jnp.moveaxis(h_prev, 0, 1)
    y_off = jnp.einsum('bclgn,bcgrpn->bclgrp', cm, h_prev) * jnp.exp(acum)[..., None]
    return (y_diag + y_off).reshape(bsz, L, G, R, P), h_last


def mamba_mixer(z, xbc, dt, conv_w, conv_b, dt_bias, a_log, d_skip, norm_w, conv_state, ssm_state):
    bsz, t = z.shape[:2]
    xbc_ext = jnp.concatenate([conv_state.astype(xbc.dtype), xbc], axis=1)
    conv = conv_b
    for w in range(CONV_W):
        conv = conv + xbc_ext[:, w:w + t] * conv_w[w]
    new_conv = xbc_ext[:, t:]
    xbc = jax.nn.silu(conv).astype(F32)
    xs, bm, cm = jnp.split(xbc, [D_INNER, D_INNER + SSM_GROUPS * D_STATE], axis=-1)
    xs = xs.reshape(bsz, t, SSM_GROUPS, HEADS_PER_GROUP, SSM_HEAD_DIM)
    bm = bm.reshape(bsz, t, SSM_GROUPS, D_STATE)
    cm = cm.reshape(bsz, t, SSM_GROUPS, D_STATE)
    dt = jax.nn.softplus(dt.astype(F32) + dt_bias.astype(F32)).reshape(bsz, t, SSM_GROUPS, HEADS_PER_GROUP)
    a = -jnp.exp(a_log.astype(F32)).reshape(SSM_GROUPS, HEADS_PER_GROUP)
    chunk = SSM_CHUNK if t % SSM_CHUNK == 0 else t
    h0 = ssm_state.astype(F32).reshape(bsz, SSM_GROUPS, HEADS_PER_GROUP, SSM_HEAD_DIM, D_STATE)
    y, h_new = ssd_chunked(xs * dt[..., None], dt * a, bm, cm, h0, chunk)
    y = (y + xs * d_skip.astype(F32).reshape(SSM_GROUPS, HEADS_PER_GROUP)[:, :, None]).reshape(bsz, t, D_INNER)
    yg = y * jax.nn.silu(z.astype(F32))
    yg = yg * lax.rsqrt(jnp.mean(jnp.square(yg), -1, keepdims=True) + LN_EPS) * norm_w.astype(F32)
    return yg, new_conv, h_new.reshape(bsz, N_SSM_HEADS, SSM_HEAD_DIM, D_STATE).astype(ssm_state.dtype)


def dsa_project(x, w_main, w_tail, pos, tm):
    bsz, t = x.shape[:2]
    x2 = x.reshape(bsz * t, D_MODEL)
    main = proj(x2, w_main, tm=tm, tn=512).reshape(bsz, t, -1)
    tail = proj(x2, w_tail, tm=tm, tn=128).reshape(bsz, t, -1)
    main, tail = rope_apply(main, tail, pos, tr=min(t, 512))
    q, k, v, qi = jnp.split(main, [1024, 1280, 1536], axis=-1)
    ki, wi = tail[..., :IDX_DIM], tail[..., IDX_DIM:IDX_DIM + N_IDX_HEADS]
    return (q, k.reshape(bsz, t, N_KV_HEADS, ATT_HEAD_DIM), v.reshape(bsz, t, N_KV_HEADS, ATT_HEAD_DIM), qi, ki, wi)


ROPE_V_BLOCKS = (10, 11)


def _rope_kernel(main_ref, tail_ref, cos_ref, sin_ref, mo_ref, to_ref):
    cos, sin = cos_ref[...], sin_ref[...]
    lane = lax.broadcasted_iota(jnp.int32, cos.shape, 1)
    first_half = lane % ATT_HEAD_DIM < ATT_HEAD_DIM // 2

    def rot(xb):
        partner = jnp.where(first_half, -pltpu.roll(xb, 128 - 32, axis=1), pltpu.roll(xb, 32, axis=1))
        return xb * cos + partner * sin

    for cb in range(main_ref.shape[2] // 128):
        xb = main_ref[0, :, cb * 128:(cb + 1) * 128]
        mo_ref[0, :, cb * 128:(cb + 1) * 128] = xb if cb in ROPE_V_BLOCKS else rot(xb)
    tb = tail_ref[0]
    to_ref[0] = jnp.where(lane < IDX_DIM, rot(tb), tb)


def rope_apply(main, tail, pos, *, tr):
    bsz, t, wm = main.shape
    inv_freq = ROPE_THETA ** (-jnp.arange(0, ATT_HEAD_DIM, 2, dtype=F32) / ATT_HEAD_DIM)
    ang = pos.astype(F32)[:, None] * inv_freq[None, :]
    cos = jnp.tile(jnp.cos(ang), (1, 4))
    sin = jnp.tile(jnp.sin(ang), (1, 4))
    return pl.pallas_call(
        _rope_kernel,
        grid=(bsz, t // tr),
        in_specs=[
            pl.BlockSpec((1, tr, wm), lambda b, i: (b, i, 0)),
            pl.BlockSpec((1, tr, 128), lambda b, i: (b, i, 0)),
            pl.BlockSpec((tr, 128), lambda b, i: (i, 0)),
            pl.BlockSpec((tr, 128), lambda b, i: (i, 0)),
        ],
        out_specs=[pl.BlockSpec((1, tr, wm), lambda b, i: (b, i, 0)), pl.BlockSpec((1, tr, 128), lambda b, i: (b, i, 0))],
        out_shape=[jax.ShapeDtypeStruct(main.shape, F32), jax.ShapeDtypeStruct(tail.shape, F32)],
        compiler_params=pltpu.CompilerParams(
            dimension_semantics=("parallel", "parallel"), vmem_limit_bytes=VMEM_LIMIT_BYTES),
        name="rope",
    )(main, tail, cos, sin)


def dsa_select_attend(q, qi, wi, ki_all, q_pos, gather_kv, topk):
    bsz, t = q.shape[:2]
    L = ki_all.shape[1]
    s = jnp.einsum('bthd,bsd->bths', qi.astype(F32), ki_all.astype(F32)) * IDX_DIM ** -0.5
    score = jnp.einsum('bth,bths->bts', wi.astype(F32) * N_IDX_HEADS ** -0.5, jax.nn.relu(s))
    admissible = jnp.arange(L, dtype=jnp.int32)[None, :] <= q_pos[:, None]
    score = jnp.where(admissible[None], score, -jnp.inf)
    _, idx = lax.top_k(score, topk)
    valid = idx <= q_pos[None, :, None]
    k_sel, v_sel = gather_kv(idx)
    qg = q.reshape(bsz, t, N_KV_HEADS, KV_GROUP, ATT_HEAD_DIM)
    logits = jnp.einsum('btjgd,btsjd->btjgs', qg.astype(F32), k_sel.astype(F32)) * ATT_HEAD_DIM ** -0.5
    logits = jnp.where(valid[:, :, None, None, :], logits, -jnp.inf)
    p = jax.nn.softmax(logits, axis=-1)
    o = jnp.einsum('btjgs,btsjd->btjgd', p, v_sel.astype(F32))
    return o.reshape(bsz, t, N_ATT_HEADS * ATT_HEAD_DIM).astype(q.dtype)


def take_rows(a, i):
    return jax.vmap(lambda ab, ib: ab[ib])(a, i)


def run_trunk(x, mem_k, mem_v, conv0, ssm0, attend, pos, p, *, tm, tm_mem):
    conv_out, ssm_out, k_out, v_out, ki_out = [], [], [], [], []
    bsz, t = x.shape[:2]
    m = bsz * t
    x = x.reshape(m, D_MODEL)
    for i in range(DEPTH):
        j = i // N_MIXERS
        g, b = p['ln_g'][i], p['ln_b'][i]
        if i % N_MIXERS == 0:
            zx = proj(x, p['w_ssm_main'][j], tm=tm, tn=512).reshape(bsz, t, -1)
            dt_raw = proj(x, p['w_ssm_dt'][j], tm=tm, tn=128).reshape(bsz, t, -1)
            act, cs = ssm_conv(zx, conv0[j], p['ssm_conv_w'][j], p['ssm_conv_b'][j], tc=min(t, 512))
            h, ss = ssd_scan(act, zx, dt_raw, p['ssm_dt_bias'][j], p['ssm_a_log'][j], p['ssm_d'][j],
                             p['ssm_norm_w'][j], ssm0[j], c=SSM_CHUNK if t % SSM_CHUNK == 0 else t)
            conv_out.append(cs)
            ssm_out.append(ss)
            w_o = p['w_ssm_out'][j]
        else:
            q, k, v, qi, ki, wi = dsa_project(x.reshape(bsz, t, D_MODEL), p['w_att_main'][j], p['w_att_tail'][j],
                                              pos, tm)
            h = attend(j, q.reshape(bsz, t, -1), k.reshape(bsz, t, -1), v.reshape(bsz, t, -1),
                       qi.reshape(bsz, t, -1), ki, wi)
            k_out.append(k)
            v_out.append(v)
            ki_out.append(ki)
            w_o = p['w_att_out'][j]
        x = out_ln(h.reshape(m, -1), w_o, x, g[0], b[0], tm=tm_mem)
        q_mem = proj(x, p['w_mem_q'][i], tm=tm, tn=512).reshape(bsz, t, D_MODEL)
        o_mem = mem_attn(q_mem, mem_k, mem_v, i, tm=min(t, tm_mem))
        x = out_ln(o_mem.reshape(m, D_MODEL), p['w_mem_out'][i], x, g[1], b[1], tm=tm_mem)
        x = ffn_ln(x, p['w_ffn_in'][i], p['w_ffn_out'][i], g[2], b[2], tm=tm)
    return (x.reshape(bsz, t, D_MODEL), jnp.stack(k_out), jnp.stack(v_out), jnp.stack(ki_out),
            jnp.stack(conv_out), jnp.stack(ssm_out))


def kernel(x_prompt, x_sample, cache_k, cache_v, cache_kidx, cache_mem_k, cache_mem_v, state_conv, state_ssm,
           page_table, mem_prompt, w_ssm_in, ssm_conv_w, ssm_conv_b, ssm_dt_bias, ssm_a_log, ssm_d, ssm_norm_w,
           w_ssm_out, w_att_in, w_att_out, w_mem_q, w_mem_kv, w_mem_out, w_ffn_in, w_ffn_out, ln_g, ln_b):
    zx_cols = D_INNER + CONV_DIM
    att_cols = sum(ATT_SPLITS[:4])
    pad128 = lambda w: jnp.pad(w, ((0, 0), (0, 0), (0, 128 - w.shape[-1])))
    params = {'w_ssm_main': w_ssm_in[..., :zx_cols].astype(BF16), 'w_ssm_dt': pad128(w_ssm_in[..., zx_cols:]).astype(BF16),
              'ssm_conv_w': ssm_conv_w, 'ssm_conv_b': ssm_conv_b, 'ssm_dt_bias': ssm_dt_bias,
              'ssm_a_log': ssm_a_log, 'ssm_d': ssm_d, 'ssm_norm_w': ssm_norm_w, 'w_ssm_out': w_ssm_out.astype(BF16),
              'w_att_main': w_att_in[..., :att_cols].astype(BF16), 'w_att_tail': pad128(w_att_in[..., att_cols:]).astype(BF16),
              'w_att_out': w_att_out.astype(BF16),
              'w_mem_q': w_mem_q.astype(BF16), 'w_mem_out': w_mem_out.astype(BF16),
              'w_ffn_in': w_ffn_in.astype(BF16), 'w_ffn_out': w_ffn_out.astype(BF16),
              'ln_g': ln_g, 'ln_b': ln_b}
    bp, seq = x_prompt.shape[:2]
    pos_p = jnp.arange(seq, dtype=jnp.int32)
    w_mem_kv_bf = w_mem_kv.astype(BF16)
    mem_rows = mem_prompt.reshape(bp * N_MEM, D_MODEL)
    mem_kv = jnp.stack([proj(mem_rows, w_mem_kv_bf[l], tm=bp * N_MEM, tn=512) for l in range(DEPTH)])
    memk_prompt = mem_kv[..., :D_MODEL].reshape(DEPTH, bp, N_MEM, N_MEM_HEADS, MEM_HEAD_DIM)
    memv_prompt = mem_kv[..., D_MODEL:].reshape(DEPTH, bp, N_MEM, N_MEM_HEADS, MEM_HEAD_DIM)
    n_ssm = (DEPTH + 1) // 2
    conv0 = jnp.zeros((n_ssm, bp, CONV_W - 1, CONV_DIM), x_prompt.dtype)
    ssm0 = jnp.zeros((n_ssm, bp, N_SSM_HEADS, SSM_HEAD_DIM, D_STATE), x_prompt.dtype)
    attend_p = lambda j, q, k, v, qi, ki, wi: dsa_prompt_attend(q, k, v, qi, ki, wi)
    y_prompt, k_prompt, v_prompt, kidx_prompt, conv_prompt, ssm_prompt = run_trunk(
        x_prompt, memk_prompt, memv_prompt, conv0, ssm0, attend_p, pos_p, params, tm=min(1024, bp * seq), tm_mem=min(512, seq))
    past = page_table.shape[1] * PAGE_SIZE
    bs, ts = x_sample.shape[:2]
    pos_s = past + jnp.arange(ts, dtype=jnp.int32)
    attend_s = lambda j, q, k, v, qi, ki, wi: dsa_sample_attend(q, k, v, qi, ki, wi, cache_k, cache_v, cache_kidx, j,
                                                                page_table)
    y_sample, k_sample, v_sample, kidx_sample, conv_sample, ssm_sample = run_trunk(
        x_sample, cache_mem_k, cache_mem_v, state_conv, state_ssm, attend_s, pos_s, params, tm=bs * ts, tm_mem=bs * ts)
    return (y_prompt, y_sample, k_prompt, v_prompt, kidx_prompt, conv_prompt, ssm_prompt, memk_prompt, memv_prompt,
            k_sample, v_sample, kidx_sample, conv_sample, ssm_sample)
```

```python
import functools
import math

import jax
import jax.numpy as jnp
import numpy as np
from jax import lax
from jax.experimental import pallas as pl
from jax.experimental.pallas import tpu as pltpu

F32 = jnp.float32
BF16 = jnp.bfloat16

DEPTH = 4
N_MIXERS = 2
D_MODEL = 1024
PAGE_SIZE = 128

D_INNER = 2048
SSM_HEAD_DIM = 64
N_SSM_HEADS = 32
SSM_GROUPS = 4
HEADS_PER_GROUP = 8
D_STATE = 128
CONV_W = 4
CONV_DIM = D_INNER + 2 * SSM_GROUPS * D_STATE
SSM_CHUNK = 128

ATT_HEAD_DIM = 64
N_ATT_HEADS = 16
N_KV_HEADS = 4
KV_GROUP = 4
N_IDX_HEADS = 8
IDX_DIM = 64
TOPK_MAX = 256
Q_BLOCK = 128
ATT_SPLITS = [1024, 256, 256, 512, 64, 8]
ROPE_THETA = 10000.0

N_MEM = 256
N_MEM_HEADS = 4
MEM_HEAD_DIM = 256
FFN_HIDDEN = 2816

DEEPNORM_ALPHA = (2 * DEPTH) ** 0.25
LN_EPS = 1e-5

VMEM_LIMIT_BYTES = 48 * 1024 * 1024


def _layer_norm_rows(y, g, b):
    mu = jnp.mean(y, axis=-1, keepdims=True)
    d = y - mu
    var = jnp.mean(d * d, axis=-1, keepdims=True)
    return d * lax.rsqrt(var + LN_EPS) * g + b


def _ffn_ln_kernel(x_ref, wa_ref, wb_ref, wo_ref, g_ref, b_ref, o_ref, xb_ref, acc_ref):
    k = pl.program_id(1)

    @pl.when(k == 0)
    def _():
        xb_ref[...] = x_ref[...].astype(BF16)
        acc_ref[...] = jnp.zeros_like(acc_ref)

    xb = xb_ref[...]
    a = jnp.dot(xb, wa_ref[...], preferred_element_type=F32)
    b = jnp.dot(xb, wb_ref[...], preferred_element_type=F32)
    h = (a * jax.nn.sigmoid(a)) * b
    acc_ref[...] += jnp.dot(h.astype(BF16), wo_ref[...], preferred_element_type=F32)

    @pl.when(k == pl.num_programs(1) - 1)
    def _():
        y = DEEPNORM_ALPHA * x_ref[...] + acc_ref[...]
        o_ref[...] = _layer_norm_rows(y, g_ref[...], b_ref[...])


def ffn_ln(x, w_in, w_out, g, b, *, tm, th=256):
    m, d = x.shape
    hidden = w_out.shape[0]
    nh = hidden // th
    return pl.pallas_call(
        _ffn_ln_kernel,
        grid=(m // tm, nh),
        in_specs=[
            pl.BlockSpec((tm, d), lambda i, k: (i, 0)),
            pl.BlockSpec((d, th), lambda i, k: (0, k)),
            pl.BlockSpec((d, th), lambda i, k: (0, k + nh)),
            pl.BlockSpec((th, d), lambda i, k: (k, 0)),
            pl.BlockSpec((1, d), lambda i, k: (0, 0)),
            pl.BlockSpec((1, d), lambda i, k: (0, 0)),
        ],
        out_specs=pl.BlockSpec((tm, d), lambda i, k: (i, 0)),
        out_shape=jax.ShapeDtypeStruct((m, d), F32),
        scratch_shapes=[pltpu.VMEM((tm, d), BF16), pltpu.VMEM((tm, d), F32)],
        compiler_params=pltpu.CompilerParams(
            dimension_semantics=("parallel", "arbitrary"), vmem_limit_bytes=VMEM_LIMIT_BYTES),
        name="ffn_ln",
    )(x, w_in, w_in, w_out, g.reshape(1, d), b.reshape(1, d))


def _proj_kernel(x_ref, w_ref, o_ref, xb_ref):
    @pl.when(pl.program_id(1) == 0)
    def _():
        xb_ref[...] = x_ref[...].astype(BF16)

    o_ref[...] = jnp.dot(xb_ref[...], w_ref[...], preferred_element_type=F32)


def proj(x, w, *, tm, tn):
    m, kd = x.shape
    n = w.shape[1]
    return pl.pallas_call(
        _proj_kernel,
        grid=(m // tm, n // tn),
        in_specs=[pl.BlockSpec((tm, kd), lambda i, j: (i, 0)), pl.BlockSpec((kd, tn), lambda i, j: (0, j))],
        out_specs=pl.BlockSpec((tm, tn), lambda i, j: (i, j)),
        out_shape=jax.ShapeDtypeStruct((m, n), F32),
        scratch_shapes=[pltpu.VMEM((tm, kd), BF16)],
        compiler_params=pltpu.CompilerParams(
            dimension_semantics=("parallel", "arbitrary"), vmem_limit_bytes=VMEM_LIMIT_BYTES),
        name="proj",
    )(x, w)


def _out_ln_kernel(h_ref, w_ref, x_ref, g_ref, b_ref, o_ref):
    y = DEEPNORM_ALPHA * x_ref[...] + jnp.dot(h_ref[...].astype(BF16), w_ref[...], preferred_element_type=F32)
    o_ref[...] = _layer_norm_rows(y, g_ref[...], b_ref[...])


def out_ln(h, w, x, g, b, *, tm):
    m, kd = h.shape
    d = w.shape[1]
    return pl.pallas_call(
        _out_ln_kernel,
        grid=(m // tm,),
        in_specs=[
            pl.BlockSpec((tm, kd), lambda i: (i, 0)),
            pl.BlockSpec((kd, d), lambda i: (0, 0)),
            pl.BlockSpec((tm, d), lambda i: (i, 0)),
            pl.BlockSpec((1, d), lambda i: (0, 0)),
            pl.BlockSpec((1, d), lambda i: (0, 0)),
        ],
        out_specs=pl.BlockSpec((tm, d), lambda i: (i, 0)),
        out_shape=jax.ShapeDtypeStruct((m, d), F32),
        compiler_params=pltpu.CompilerParams(
            dimension_semantics=("parallel",), vmem_limit_bytes=VMEM_LIMIT_BYTES),
        name="out_ln",
    )(h, w, x, g.reshape(1, d), b.reshape(1, d))


def _mem_attn_kernel(q_ref, mk_ref, mv_ref, o_ref):
    for h in range(N_MEM_HEADS):
        sl = slice(h * MEM_HEAD_DIM, (h + 1) * MEM_HEAD_DIM)
        s = lax.dot_general(q_ref[0, :, sl].astype(BF16), mk_ref[:, h, :].astype(BF16), _NT,
                            preferred_element_type=F32) * MEM_HEAD_DIM ** -0.5
        e = jnp.exp(s - jnp.max(s, axis=1, keepdims=True))
        pv = jnp.dot(e.astype(BF16), mv_ref[:, h, :].astype(BF16), preferred_element_type=F32)
        o_ref[0, :, sl] = pv / jnp.sum(e, axis=1, keepdims=True)


def mem_attn(q, mk, mv, layer, *, tm):
    bsz, t, d = q.shape
    mem_spec = pl.BlockSpec((None, None, N_MEM, N_MEM_HEADS, MEM_HEAD_DIM), lambda b, i: (layer, b, 0, 0, 0))
    return pl.pallas_call(
        _mem_attn_kernel,
        grid=(bsz, t // tm),
        in_specs=[pl.BlockSpec((1, tm, d), lambda b, i: (b, i, 0)), mem_spec, mem_spec],
        out_specs=pl.BlockSpec((1, tm, d), lambda b, i: (b, i, 0)),
        out_shape=jax.ShapeDtypeStruct((bsz, t, d), F32),
        compiler_params=pltpu.CompilerParams(
            dimension_semantics=("parallel", "arbitrary"), vmem_limit_bytes=VMEM_LIMIT_BYTES),
        name="mem_attn",
    )(q, mk, mv)


CONV_COLS = 1024


def _conv_kernel(x_ref, st_ref, w_ref, b_ref, act_ref, last_ref, prev_ref, *, tc):
    @pl.when(pl.program_id(2) == 0)
    def _():
        prev_ref[...] = st_ref[0]

    x = x_ref[0]
    prev = prev_ref[...]
    row = lax.broadcasted_iota(jnp.int32, (8, CONV_COLS), 0)
    acc = jnp.broadcast_to(b_ref[...], x.shape)
    for s in (3, 2, 1):
        rolled = pltpu.roll(x, s, axis=0)
        top = jnp.where(row < s, pltpu.roll(prev, s, axis=0), rolled[0:8])
        shifted = top if tc == 8 else jnp.concatenate([top, rolled[8:]], axis=0)
        acc = acc + shifted * w_ref[CONV_W - 1 - s:CONV_W - s, :]
    acc = acc + x * w_ref[CONV_W - 1:CONV_W, :]
    act_ref[0] = acc * jax.nn.sigmoid(acc)
    prev_ref[...] = x[tc - 8:tc]
    last_ref[0] = x[tc - 8:tc]


def ssm_conv(zx, conv_state, conv_w, conv_b, *, tc):
    bsz, t = zx.shape[:2]
    ncb = CONV_DIM // CONV_COLS
    col0 = D_INNER // CONV_COLS
    st8 = jnp.pad(conv_state, ((0, 0), (8 - (CONV_W - 1), 0), (0, 0)))
    act, last = pl.pallas_call(
        functools.partial(_conv_kernel, tc=tc),
        grid=(bsz, ncb, t // tc),
        in_specs=[
            pl.BlockSpec((1, tc, CONV_COLS), lambda b, c, i: (b, i, col0 + c)),
            pl.BlockSpec((1, 8, CONV_COLS), lambda b, c, i: (b, 0, c)),
            pl.BlockSpec((CONV_W, CONV_COLS), lambda b, c, i: (0, c)),
            pl.BlockSpec((1, CONV_COLS), lambda b, c, i: (0, c)),
        ],
        out_specs=[
            pl.BlockSpec((1, tc, CONV_COLS), lambda b, c, i: (b, i, c)),
            pl.BlockSpec((1, 8, CONV_COLS), lambda b, c, i: (b, 0, c)),
        ],
        out_shape=[jax.ShapeDtypeStruct((bsz, t, CONV_DIM), F32), jax.ShapeDtypeStruct((bsz, 8, CONV_DIM), F32)],
        scratch_shapes=[pltpu.VMEM((8, CONV_COLS), F32)],
        compiler_params=pltpu.CompilerParams(
            dimension_semantics=("parallel", "parallel", "arbitrary"), vmem_limit_bytes=VMEM_LIMIT_BYTES),
        name="ssm_conv",
    )(zx, st8, conv_w, conv_b.reshape(1, CONV_DIM))
    return act, last[:, 8 - (CONV_W - 1):, :]


def _split3(v):
    hi = v.astype(BF16)
    r1 = v - hi.astype(F32)
    mid = r1.astype(BF16)
    lo = (r1 - mid.astype(F32)).astype(BF16)
    return hi, mid, lo


def _dot01_right(v, ones_mat):
    return sum(jnp.dot(p, ones_mat, preferred_element_type=F32) for p in _split3(v))


def _dot01_left(ones_mat, v):
    return sum(jnp.dot(ones_mat, p, preferred_element_type=F32) for p in _split3(v))


def _softplus(x):
    return jnp.maximum(x, 0.0) + jnp.log1p(jnp.exp(-jnp.abs(x)))


def _ssd_kernel(xs_ref, b_ref, c_ref, z_ref, dt_ref, dtt_ref, dtb_ref, dtbt_ref, alog_ref, alogt_ref,
                d_ref, nw_ref, h0_ref, y_ref, hT_ref, st_ref, yacc_ref, *, c):
    ci = pl.program_id(1)

    @pl.when(ci == 0)
    def _():
        st_ref[...] = h0_ref[0]

    gw = HEADS_PER_GROUP * SSM_HEAD_DIM
    dt = _softplus(dt_ref[0][:, 0:N_SSM_HEADS] + dtb_ref[...])
    dtt = _softplus(dtt_ref[0] + dtbt_ref[...])
    a = dt * -jnp.exp(alog_ref[...])
    at = dtt * -jnp.exp(alogt_ref[...])
    ri = lax.broadcasted_iota(jnp.int32, (c, c), 0)
    cj = lax.broadcasted_iota(jnp.int32, (c, c), 1)
    causal = ri >= cj
    acum = _dot01_left(jnp.where(causal, 1.0, 0.0).astype(BF16), a)
    acum_t = _dot01_right(at, jnp.where(ri <= cj, 1.0, 0.0).astype(BF16))
    a_last = acum[c - 1:c, :]
    expand = jnp.where(lax.broadcasted_iota(jnp.int32, (N_SSM_HEADS, D_INNER), 1) // SSM_HEAD_DIM
                       == lax.broadcasted_iota(jnp.int32, (N_SSM_HEADS, D_INNER), 0), 1.0, 0.0).astype(BF16)
    e_dt = _dot01_right(dt, expand)
    e_in = _dot01_right(jnp.exp(acum), expand)
    e_out = _dot01_right(jnp.exp(a_last - acum), expand)
    e_chunk = e_in[c - 1:c, :]
    xs = xs_ref[0]
    xdt = xs * e_dt
    head_of_col = lax.broadcasted_iota(jnp.int32, (c, gw), 1) // SSM_HEAD_DIM
    for g in range(SSM_GROUPS):
        gs = slice(g * gw, (g + 1) * gw)
        bg = b_ref[0][:, g * D_STATE:(g + 1) * D_STATE].astype(BF16)
        cg = c_ref[0][:, g * D_STATE:(g + 1) * D_STATE].astype(BF16)
        cb = lax.dot_general(cg, bg, _NT, preferred_element_type=F32)
        xg = xdt[:, gs]
        yg = jnp.zeros((c, gw), F32)
        for r in range(HEADS_PER_GROUP):
            h = g * HEADS_PER_GROUP + r
            seg = acum[:, h:h + 1] - acum_t[h:h + 1, :]
            m = (cb * jnp.exp(jnp.where(causal, seg, -jnp.inf))).astype(BF16)
            xm = jnp.where(head_of_col == r, xg, 0.0).astype(BF16)
            yg = yg + jnp.dot(m, xm, preferred_element_type=F32)
        state = st_ref[g]
        y_off = jnp.dot(cg, state.astype(BF16), preferred_element_type=F32) * e_in[:, gs]
        yacc_ref[:, gs] = yg + y_off + xs[:, gs] * d_ref[:, gs]
        xd = (xg * e_out[:, gs]).astype(BF16)
        st_ref[g] = state * e_chunk[:, gs] + lax.dot_general(bg, xd, (((0,), (0,)), ((), ())),
                                                             preferred_element_type=F32)
    z = z_ref[0]
    yz = yacc_ref[...] * (z * jax.nn.sigmoid(z))
    y_ref[0] = yz * lax.rsqrt(jnp.mean(yz * yz, axis=-1, keepdims=True) + LN_EPS) * nw_ref[...]

    @pl.when(ci == pl.num_programs(1) - 1)
    def _():
        hT_ref[0] = st_ref[...]


def ssd_scan(act, zx, dt_raw, dt_bias, a_log, d_skip, norm_w, ssm_state, *, c):
    bsz, t = act.shape[:2]
    gw = HEADS_PER_GROUP * SSM_HEAD_DIM
    bc_w = SSM_GROUPS * D_STATE
    h0 = ssm_state.reshape(bsz, SSM_GROUPS, HEADS_PER_GROUP, SSM_HEAD_DIM, D_STATE)
    h0 = h0.transpose(0, 1, 4, 2, 3).reshape(bsz, SSM_GROUPS, D_STATE, gw)
    dtt = jnp.swapaxes(dt_raw[..., :N_SSM_HEADS], 1, 2)
    row = lambda v: v.reshape(1, -1)
    col = lambda v: v.reshape(-1, 1)
    full = lambda shape: pl.BlockSpec(shape, lambda b, i: (0,) * len(shape))
    y, h_t = pl.pallas_call(
        functools.partial(_ssd_kernel, c=c),
        grid=(bsz, t // c),
        in_specs=[
            pl.BlockSpec((1, c, D_INNER), lambda b, i: (b, i, 0)),
            pl.BlockSpec((1, c, bc_w), lambda b, i: (b, i, D_INNER // bc_w)),
            pl.BlockSpec((1, c, bc_w), lambda b, i: (b, i, D_INNER // bc_w + 1)),
            pl.BlockSpec((1, c, D_INNER), lambda b, i: (b, i, 0)),
            pl.BlockSpec((1, c, 128), lambda b, i: (b, i, 0)),
            pl.BlockSpec((1, N_SSM_HEADS, c), lambda b, i: (b, 0, i)),
            full((1, N_SSM_HEADS)), full((N_SSM_HEADS, 1)), full((1, N_SSM_HEADS)), full((N_SSM_HEADS, 1)),
            full((1, D_INNER)), full((1, D_INNER)),
            pl.BlockSpec((1, SSM_GROUPS, D_STATE, gw), lambda b, i: (b, 0, 0, 0)),
        ],
        out_specs=[
            pl.BlockSpec((1, c, D_INNER), lambda b, i: (b, i, 0)),
            pl.BlockSpec((1, SSM_GROUPS, D_STATE, gw), lambda b, i: (b, 0, 0, 0)),
        ],
        out_shape=[jax.ShapeDtypeStruct((bsz, t, D_INNER), F32),
                   jax.ShapeDtypeStruct((bsz, SSM_GROUPS, D_STATE, gw), F32)],
        scratch_shapes=[pltpu.VMEM((SSM_GROUPS, D_STATE, gw), F32), pltpu.VMEM((c, D_INNER), F32)],
        compiler_params=pltpu.CompilerParams(
            dimension_semantics=("parallel", "arbitrary"), vmem_limit_bytes=VMEM_LIMIT_BYTES),
        name="ssd_scan",
    )(act, act, act, zx, dt_raw, dtt, row(dt_bias), col(dt_bias), row(a_log), col(a_log),
      row(jnp.repeat(d_skip, SSM_HEAD_DIM)), row(norm_w), h0)
    h_t = h_t.reshape(bsz, SSM_GROUPS, D_STATE, HEADS_PER_GROUP, SSM_HEAD_DIM)
    return y, h_t.transpose(0, 1, 3, 4, 2).reshape(bsz, N_SSM_HEADS, SSM_HEAD_DIM, D_STATE)


INT_MIN = -2 ** 31
NEG_BIG = -1e30
_NT = (((1,), (1,)), ((), ()))


def _float_key(x):
    x = jnp.where(x == 0.0, 0.0, x)
    bits = lax.bitcast_convert_type(x, jnp.int32)
    return bits ^ ((bits >> 31) & 0x7FFFFFFF)


def _dsa_prompt_kernel(q_ref, qi_ref, wi_ref, k_ref, vx_ref, ki_ref, o_ref,
                       qs_ref, qis_ref, wib_ref, key_ref, m_ref, acc_ref, *, tq, tk, topk):
    i = pl.program_id(1)
    nkb = ((i + 1) * tq + tk - 1) // tk
    nl = tk // 128

    def lanes(x):
        return jnp.concatenate([x] * nl, axis=1)

    for h in range(N_ATT_HEADS):
        j, g = divmod(h, KV_GROUP)
        qs_ref[j, g * tq:(g + 1) * tq, :] = q_ref[0, :, h * 64:(h + 1) * 64]
    for h in range(N_IDX_HEADS):
        qis_ref[h * tq:(h + 1) * tq, :] = qi_ref[0, :, h * 64:(h + 1) * 64]
        wib_ref[h] = jnp.broadcast_to(wi_ref[0, :, h:h + 1] * (N_IDX_HEADS ** -0.5), (tq, 128))
    q_pos = i * tq + lax.broadcasted_iota(jnp.int32, (tq, 1), 0)

    def score_block(kb, c):
        off = pl.multiple_of(kb * tk, tk)
        s = lax.dot_general(qis_ref[...], ki_ref[0, pl.ds(off, tk), :], _NT, preferred_element_type=F32)
        acc = jnp.zeros((tq, tk), F32)
        for h in range(N_IDX_HEADS):
            acc = acc + lanes(wib_ref[h]) * jnp.maximum(s[h * tq:(h + 1) * tq, :], 0.0)
        kpos = off + lax.broadcasted_iota(jnp.int32, (tq, tk), 1)
        key_ref[kb] = jnp.where(kpos <= q_pos, _float_key(acc), INT_MIN)
        return c

    lax.fori_loop(0, nkb, score_block, 0)

    def count(cand, strict):
        def body(kb, acc):
            blk = key_ref[kb]
            for c in range(nl):
                part = blk[:, c * 128:(c + 1) * 128]
                hit = (part > cand) if strict else (part >= cand)
                acc = acc + jnp.where(hit, 1.0, 0.0)
            return acc
        acc = lax.fori_loop(0, nkb, body, jnp.zeros((tq, 128), F32))
        return jnp.sum(acc, axis=1, keepdims=True)

    kf = float(topk)
    zero = jnp.zeros((tq, 128), jnp.int32)
    thr = jnp.where(count(zero, False) >= kf, zero, jnp.full((tq, 128), INT_MIN, jnp.int32))

    def bit_step(it, thr):
        cand = thr | jnp.left_shift(jnp.int32(1), 30 - it)
        return jnp.where(count(cand, False) >= kf, cand, thr)

    thr = lax.fori_loop(0, 31, bit_step, thr)
    need = jnp.where(thr == INT_MIN, 0.0, kf - count(thr, True))

    m_ref[...] = jnp.full(m_ref.shape, NEG_BIG, F32)
    acc_ref[...] = jnp.zeros(acc_ref.shape, F32)
    upper = jnp.where(lax.broadcasted_iota(jnp.int32, (tk, tk), 0) < lax.broadcasted_iota(jnp.int32, (tk, tk), 1),
                      1.0, 0.0).astype(BF16)
    thr_t, need_t = lanes(thr), lanes(need)

    def attend_block(kb, eq_before):
        off = pl.multiple_of(kb * tk, tk)
        key = key_ref[kb]
        eq = key == thr_t
        eqf = jnp.where(eq, 1.0, 0.0)
        rank = jnp.dot(eqf.astype(BF16), upper, preferred_element_type=F32) + lanes(eq_before)
        bias = jnp.where(key > thr_t, 0.0, jnp.where(eq, jnp.where(rank < need_t, 0.0, NEG_BIG), NEG_BIG))
        bias4 = jnp.concatenate([bias] * KV_GROUP, axis=0)
        for j in range(N_KV_HEADS):
            kblk = k_ref[0, pl.ds(off, tk), j * 64:(j + 1) * 64]
            s = lax.dot_general(qs_ref[j], kblk, _NT, preferred_element_type=F32) + bias4
            m_old = m_ref[j]
            mx = s[:, 0:128]
            for c in range(1, nl):
                mx = jnp.maximum(mx, s[:, c * 128:(c + 1) * 128])
            m_new = jnp.maximum(m_old, jnp.max(mx, axis=1, keepdims=True))
            p = jnp.exp(s - lanes(m_new))
            pv = jnp.dot(p.astype(BF16), vx_ref[0, j, pl.ds(off, tk), :], preferred_element_type=F32)
            acc_ref[j] = jnp.exp(m_old - m_new) * acc_ref[j] + pv
            m_ref[j] = m_new
        return eq_before + jnp.sum(eqf, axis=1, keepdims=True)

    lax.fori_loop(0, nkb, attend_block, jnp.zeros((tq, 128), F32))

    for h in range(N_ATT_HEADS):
        j, g = divmod(h, KV_GROUP)
        a = acc_ref[j, g * tq:(g + 1) * tq, :]
        o_ref[0, :, h * 64:(h + 1) * 64] = a[:, 0:64] / a[:, 64:65]


def dsa_prompt_attend(q, k, v, qi, ki, wi, *, tq=128, tk=256):
    bsz, t = q.shape[:2]
    topk = min(TOPK_MAX, t // 4)
    q_bf = (q * ATT_HEAD_DIM ** -0.5).astype(BF16)
    qi_bf = (qi * IDX_DIM ** -0.5).astype(BF16)
    vh = v.reshape(bsz, t, N_KV_HEADS, ATT_HEAD_DIM).transpose(0, 2, 1, 3)
    vx = jnp.concatenate([vh, jnp.ones_like(vh)], axis=-1).astype(BF16)
    kern = functools.partial(_dsa_prompt_kernel, tq=tq, tk=tk, topk=topk)
    return pl.pallas_call(
        kern,
        grid=(bsz, t // tq),
        in_specs=[
            pl.BlockSpec((1, tq, 1024), lambda b, i: (b, i, 0)),
            pl.BlockSpec((1, tq, 512), lambda b, i: (b, i, 0)),
            pl.BlockSpec((1, tq, N_IDX_HEADS), lambda b, i: (b, i, 0)),
            pl.BlockSpec((1, t, 256), lambda b, i: (b, 0, 0)),
            pl.BlockSpec((1, N_KV_HEADS, t, 128), lambda b, i: (b, 0, 0, 0)),
            pl.BlockSpec((1, t, IDX_DIM), lambda b, i: (b, 0, 0)),
        ],
        out_specs=pl.BlockSpec((1, tq, 1024), lambda b, i: (b, i, 0)),
        out_shape=jax.ShapeDtypeStruct((bsz, t, 1024), F32),
        scratch_shapes=[
            pltpu.VMEM((N_KV_HEADS, KV_GROUP * tq, 64), BF16),
            pltpu.VMEM((N_IDX_HEADS * tq, 64), BF16),
            pltpu.VMEM((N_IDX_HEADS, tq, 128), F32),
            pltpu.VMEM((t // tk, tq, tk), jnp.int32),
            pltpu.VMEM((N_KV_HEADS, KV_GROUP * tq, 128), F32),
            pltpu.VMEM((N_KV_HEADS, KV_GROUP * tq, 128), F32),
        ],
        compiler_params=pltpu.CompilerParams(
            dimension_semantics=("parallel", "arbitrary"), vmem_limit_bytes=VMEM_LIMIT_BYTES),
        name="dsa_prompt_attend",
    )(q_bf, qi_bf, wi, k.astype(BF16), vx, ki.astype(BF16))


PAGES_PER_STEP = 8


def _sample_select_kernel(pt_ref, qi_ref, wi_ref, kinew_ref, *rest, t, n_steps, topk):
    ki_refs = rest[:PAGES_PER_STEP]
    bias_ref, qis_ref, wib_ref, key_ref = rest[PAGES_PER_STEP:]
    p = pl.program_id(1)
    w = PAGES_PER_STEP * PAGE_SIZE
    nl = w // 128

    @pl.when(p == 0)
    def _():
        for h in range(N_IDX_HEADS):
            qis_ref[h * t:(h + 1) * t, :] = qi_ref[0, :, h * 64:(h + 1) * 64]
            wib_ref[h] = jnp.broadcast_to(wi_ref[0, :, h:h + 1] * (N_IDX_HEADS ** -0.5), (t, 128))

    def scores(s):
        acc = jnp.zeros((t, s.shape[1]), F32)
        for h in range(N_IDX_HEADS):
            wfull = jnp.concatenate([wib_ref[h]] * (s.shape[1] // 128), axis=1)
            acc = acc + wfull * jnp.maximum(s[h * t:(h + 1) * t, :], 0.0)
        return acc

    ki_t = jnp.concatenate([r[...] for r in ki_refs], axis=1).astype(BF16)
    key_ref[p] = _float_key(scores(jnp.dot(qis_ref[...].astype(BF16), ki_t, preferred_element_type=F32)))

    @pl.when(p == n_steps - 1)
    def _():
        knew = _float_key(scores(lax.dot_general(qis_ref[...].astype(BF16), kinew_ref[0].astype(BF16), _NT,
                                                 preferred_element_type=F32)))
        n_idx = lax.broadcasted_iota(jnp.int32, (t, 128), 1)
        r_idx = lax.broadcasted_iota(jnp.int32, (t, 128), 0)
        knew = jnp.where(n_idx <= r_idx, knew, INT_MIN)
        key_ref[n_steps] = jnp.concatenate([knew, jnp.full((t, w - 128), INT_MIN, jnp.int32)], axis=1)

        def count(cand, strict):
            def body(blk, acc):
                kb = key_ref[blk]
                for c in range(nl):
                    part = kb[:, c * 128:(c + 1) * 128]
                    hit = (part > cand) if strict else (part >= cand)
                    acc = acc + jnp.where(hit, 1.0, 0.0)
                return acc
            acc = lax.fori_loop(0, n_steps + 1, body, jnp.zeros((t, 128), F32))
            return jnp.sum(acc, axis=1, keepdims=True)

        kf = float(topk)
        zero = jnp.zeros((t, 128), jnp.int32)
        thr = jnp.where(count(zero, False) >= kf, zero, jnp.full((t, 128), INT_MIN, jnp.int32))

        def bit_step(it, thr):
            cand = thr | jnp.left_shift(jnp.int32(1), 30 - it)
            return jnp.where(count(cand, False) >= kf, cand, thr)

        thr = lax.fori_loop(0, 31, bit_step, thr)
        need = jnp.where(thr == INT_MIN, 0.0, kf - count(thr, True))
        upper = jnp.where(lax.broadcasted_iota(jnp.int32, (128, 128), 0)
                          < lax.broadcasted_iota(jnp.int32, (128, 128), 1), 1.0, 0.0).astype(BF16)

        def emit(blk, eq_before):
            kb = key_ref[blk]
            parts = []
            for c in range(nl):
                part = kb[:, c * 128:(c + 1) * 128]
                eq = part == thr
                eqf = jnp.where(eq, 1.0, 0.0)
                rank = jnp.dot(eqf.astype(BF16), upper, preferred_element_type=F32) + eq_before
                parts.append(jnp.where(part > thr, 0.0,
                                       jnp.where(eq, jnp.where(rank < need, 0.0, NEG_BIG), NEG_BIG)))
                eq_before = eq_before + jnp.sum(eqf, axis=1, keepdims=True)
            bias_ref[0, blk] = jnp.concatenate(parts, axis=1)
            return eq_before

        lax.fori_loop(0, n_steps + 1, emit, jnp.zeros((t, 128), F32))


def _sample_attend_kernel(pt_ref, q_ref, bias_ref, biasnew_ref, knew_ref, vnew_ref, *rest, t, n_steps):
    k_refs = rest[:PAGES_PER_STEP]
    v_refs = rest[PAGES_PER_STEP:2 * PAGES_PER_STEP]
    o_ref, qs_ref, m_ref, l_ref, acc_ref = rest[2 * PAGES_PER_STEP:]
    p = pl.program_id(1)
    rows = KV_GROUP * t

    @pl.when(p == 0)
    def _():
        for h in range(N_ATT_HEADS):
            j, g = divmod(h, KV_GROUP)
            qs_ref[j, g * t:(g + 1) * t, :] = q_ref[0, :, h * 64:(h + 1) * 64]
        m_ref[...] = jnp.full(m_ref.shape, NEG_BIG, F32)
        l_ref[...] = jnp.zeros(l_ref.shape, F32)
        acc_ref[...] = jnp.zeros(acc_ref.shape, F32)

    def update(k_t, v_t, bias):
        bias4 = jnp.concatenate([bias] * KV_GROUP, axis=0)
        for j in range(N_KV_HEADS):
            s = jnp.dot(qs_ref[j].astype(BF16), k_t(j).astype(BF16), preferred_element_type=F32) + bias4
            m_old = m_ref[j]
            m_new = jnp.maximum(m_old, jnp.max(s, axis=1, keepdims=True))
            alpha = jnp.exp(m_old - m_new)
            pr = jnp.exp(s - m_new[:, 0:1])
            l_ref[j] = alpha * l_ref[j] + jnp.sum(pr, axis=1, keepdims=True)
            pv = lax.dot_general(pr.astype(BF16), v_t(j).astype(BF16), _NT, preferred_element_type=F32)
            acc_ref[j] = alpha[:, 0:64] * acc_ref[j] + pv
            m_ref[j] = m_new

    update(lambda j: jnp.concatenate([r[j] for r in k_refs], axis=1),
           lambda j: jnp.concatenate([r[j] for r in v_refs], axis=1), bias_ref[0, 0])

    @pl.when(p == n_steps - 1)
    def _():
        update(lambda j: knew_ref[0, j * 64:(j + 1) * 64, :], lambda j: vnew_ref[0, j * 64:(j + 1) * 64, :],
               biasnew_ref[0, 0, :, 0:128])
        for h in range(N_ATT_HEADS):
            j, g = divmod(h, KV_GROUP)
            sl = slice(g * t, (g + 1) * t)
            o_ref[0, :, h * 64:(h + 1) * 64] = acc_ref[j, sl, :] / l_ref[j, sl, 0:64]


def dsa_sample_attend(q, k, v, qi, ki, wi, cache_k, cache_v, cache_kidx, layer, page_table):
    bsz, t = q.shape[:2]
    n_pages = page_table.shape[1]
    past = n_pages * PAGE_SIZE
    n_steps = n_pages // PAGES_PER_STEP
    w = PAGES_PER_STEP * PAGE_SIZE
    topk = min(TOPK_MAX, (past + t) // 4)
    pad = lambda a: jnp.pad(a, ((0, 0), (0, PAGE_SIZE - t), (0, 0)))
    pool_k = cache_k.transpose(0, 1, 3, 4, 2)
    pool_v = cache_v.transpose(0, 1, 3, 4, 2)
    pool_ki = cache_kidx.transpose(0, 1, 3, 2)
    knew_t = jnp.swapaxes(pad(k), 1, 2)
    vnew_t = jnp.swapaxes(pad(v), 1, 2)

    def page_spec(r, *major):
        zeros = (0,) * (1 + len(major))
        return pl.BlockSpec((None, None) + major + (PAGE_SIZE,),
                            lambda b, p, pt: (layer, pt[b, p * PAGES_PER_STEP + r]) + zeros)

    bias = pl.pallas_call(
        functools.partial(_sample_select_kernel, t=t, n_steps=n_steps, topk=topk),
        grid_spec=pltpu.PrefetchScalarGridSpec(
            num_scalar_prefetch=1,
            grid=(bsz, n_steps),
            in_specs=[
                pl.BlockSpec((1, t, 512), lambda b, p, pt: (b, 0, 0)),
                pl.BlockSpec((1, t, N_IDX_HEADS), lambda b, p, pt: (b, 0, 0)),
                pl.BlockSpec((1, PAGE_SIZE, IDX_DIM), lambda b, p, pt: (b, 0, 0)),
            ] + [page_spec(r, IDX_DIM) for r in range(PAGES_PER_STEP)],
            out_specs=pl.BlockSpec((1, n_steps + 1, t, w), lambda b, p, pt: (b, 0, 0, 0)),
            scratch_shapes=[
                pltpu.VMEM((N_IDX_HEADS * t, 64), F32),
                pltpu.VMEM((N_IDX_HEADS, t, 128), F32),
                pltpu.VMEM((n_steps + 1, t, w), jnp.int32),
            ]),
        out_shape=jax.ShapeDtypeStruct((bsz, n_steps + 1, t, w), F32),
        compiler_params=pltpu.CompilerParams(
            dimension_semantics=("parallel", "arbitrary"), vmem_limit_bytes=VMEM_LIMIT_BYTES),
        name="dsa_sample_select",
    )(page_table, qi * IDX_DIM ** -0.5, wi, pad(ki), *([pool_ki] * PAGES_PER_STEP))

    return pl.pallas_call(
        functools.partial(_sample_attend_kernel, t=t, n_steps=n_steps),
        grid_spec=pltpu.PrefetchScalarGridSpec(
            num_scalar_prefetch=1,
            grid=(bsz, n_steps),
            in_specs=[
                pl.BlockSpec((1, t, 1024), lambda b, p, pt: (b, 0, 0)),
                pl.BlockSpec((1, 1, t, w), lambda b, p, pt: (b, p, 0, 0)),
                pl.BlockSpec((1, 1, t, w), lambda b, p, pt: (b, n_steps, 0, 0)),
                pl.BlockSpec((1, 256, PAGE_SIZE), lambda b, p, pt: (b, 0, 0)),
                pl.BlockSpec((1, 256, PAGE_SIZE), lambda b, p, pt: (b, 0, 0)),
            ] + [page_spec(r, N_KV_HEADS, ATT_HEAD_DIM) for r in range(PAGES_PER_STEP)] * 2,
            out_specs=pl.BlockSpec((1, t, 1024), lambda b, p, pt: (b, 0, 0)),
            scratch_shapes=[
                pltpu.VMEM((N_KV_HEADS, KV_GROUP * t, 64), F32),
                pltpu.VMEM((N_KV_HEADS, KV_GROUP * t, 128), F32),
                pltpu.VMEM((N_KV_HEADS, KV_GROUP * t, 128), F32),
                pltpu.VMEM((N_KV_HEADS, KV_GROUP * t, 64), F32),
            ]),
        out_shape=jax.ShapeDtypeStruct((bsz, t, 1024), F32),
        compiler_params=pltpu.CompilerParams(
            dimension_semantics=("parallel", "arbitrary"), vmem_limit_bytes=VMEM_LIMIT_BYTES),
        name="dsa_sample_attend",
    )(page_table, q * ATT_HEAD_DIM ** -0.5, bias, bias, knew_t, vnew_t,
      *([pool_k] * PAGES_PER_STEP), *([pool_v] * PAGES_PER_STEP))


def layer_norm(x, g, b):
    xf = x.astype(F32)
    mu = jnp.mean(xf, -1, keepdims=True)
    var = jnp.mean(jnp.square(xf - mu), -1, keepdims=True)
    return ((xf - mu) * lax.rsqrt(var + LN_EPS) * g.astype(F32) + b.astype(F32)).astype(x.dtype)


def rope(x, pos):
    d = x.shape[-1]
    inv_freq = ROPE_THETA ** (-jnp.arange(0, d, 2, dtype=F32) / d)
    ang = pos.astype(F32)[:, None] * inv_freq[None, :]
    cos = jnp.cos(ang)[:, None, :]
    sin = jnp.sin(ang)[:, None, :]
    x1, x2 = jnp.split(x.astype(F32), 2, axis=-1)
    return jnp.concatenate([x1 * cos - x2 * sin, x1 * sin + x2 * cos], -1).astype(x.dtype)


def ssd_chunked(xdt, a, bm, cm, h0, chunk):
    bsz, L, G, R, P = xdt.shape
    N = bm.shape[-1]
    nc = L // chunk
    x = xdt.reshape(bsz, nc, chunk, G, R, P)
    a = a.reshape(bsz, nc, chunk, G, R)
    bm = bm.reshape(bsz, nc, chunk, G, N)
    cm = cm.reshape(bsz, nc, chunk, G, N)
    acum = jnp.cumsum(a, axis=2)
    acum_t = jnp.moveaxis(acum, 2, -1)
    seg = acum_t[..., :, None] - acum_t[..., None, :]
    causal = jnp.tril(jnp.ones((chunk, chunk), dtype=bool))
    lmat = jnp.exp(jnp.where(causal, seg, -jnp.inf))
    cb = jnp.einsum('bclgn,bcsgn->bcgls', cm, bm)
    y_diag = jnp.einsum('bcgrls,bcsgrp->bclgrp', cb[:, :, :, None] * lmat, x)
    decay = jnp.exp(acum[:, :, -1:] - acum)
    states = jnp.einsum('bclgn,bclgrp->bcgrpn', bm, x * decay[..., None])
    chunk_decay = jnp.exp(acum[:, :, -1])

    def step(h, inp):
        st, dec = inp
        return h * dec[..., None, None] + st, h

    h_last, h_prev = lax.scan(step, h0, (jnp.moveaxis(states, 1, 0), jnp.moveaxis(chunk_decay, 1, 0)))
    h_prev = jnp.moveaxis(h_prev, 0, 1)
    y_off = jnp.einsum('bclgn,bcgrpn->bclgrp', cm, h_prev) * jnp.exp(acum)[..., None]
    return (y_diag + y_off).reshape(bsz, L, G, R, P), h_last


def mamba_mixer(z, xbc, dt, conv_w, conv_b, dt_bias, a_log, d_skip, norm_w, conv_state, ssm_state):
    bsz, t = z.shape[:2]
    xbc_ext = jnp.concatenate([conv_state.astype(xbc.dtype), xbc], axis=1)
    conv = conv_b
    for w in range(CONV_W):
        conv = conv + xbc_ext[:, w:w + t] * conv_w[w]
    new_conv = xbc_ext[:, t:]
    xbc = jax.nn.silu(conv).astype(F32)
    xs, bm, cm = jnp.split(xbc, [D_INNER, D_INNER + SSM_GROUPS * D_STATE], axis=-1)
    xs = xs.reshape(bsz, t, SSM_GROUPS, HEADS_PER_GROUP, SSM_HEAD_DIM)
    bm = bm.reshape(bsz, t, SSM_GROUPS, D_STATE)
    cm = cm.reshape(bsz, t, SSM_GROUPS, D_STATE)
    dt = jax.nn.softplus(dt.astype(F32) + dt_bias.astype(F32)).reshape(bsz, t, SSM_GROUPS, HEADS_PER_GROUP)
    a = -jnp.exp(a_log.astype(F32)).reshape(SSM_GROUPS, HEADS_PER_GROUP)
    chunk = SSM_CHUNK if t % SSM_CHUNK == 0 else t
    h0 = ssm_state.astype(F32).reshape(bsz, SSM_GROUPS, HEADS_PER_GROUP, SSM_HEAD_DIM, D_STATE)
    y, h_new = ssd_chunked(xs * dt[..., None], dt * a, bm, cm, h0, chunk)
    y = (y + xs * d_skip.astype(F32).reshape(SSM_GROUPS, HEADS_PER_GROUP)[:, :, None]).reshape(bsz, t, D_INNER)
    yg = y * jax.nn.silu(z.astype(F32))
    yg = yg * lax.rsqrt(jnp.mean(jnp.square(yg), -1, keepdims=True) + LN_EPS) * norm_w.astype(F32)
    return yg, new_conv, h_new.reshape(bsz, N_SSM_HEADS, SSM_HEAD_DIM, D_STATE).astype(ssm_state.dtype)


def dsa_project(x, w_main, w_tail, pos, tm):
    bsz, t = x.shape[:2]
    x2 = x.reshape(bsz * t, D_MODEL)
    main = proj(x2, w_main, tm=tm, tn=512).reshape(bsz, t, -1)
    tail = proj(x2, w_tail, tm=tm, tn=128).reshape(bsz, t, -1)
    main, tail = rope_apply(main, tail, pos, tr=min(t, 512))
    q, k, v, qi = jnp.split(main, [1024, 1280, 1536], axis=-1)
    ki, wi = tail[..., :IDX_DIM], tail[..., IDX_DIM:IDX_DIM + N_IDX_HEADS]
    return (q, k.reshape(bsz, t, N_KV_HEADS, ATT_HEAD_DIM), v.reshape(bsz, t, N_KV_HEADS, ATT_HEAD_DIM), qi, ki, wi)


ROPE_V_BLOCKS = (10, 11)


def _rope_kernel(main_ref, tail_ref, cos_ref, sin_ref, mo_ref, to_ref):
    cos, sin = cos_ref[...], sin_ref[...]
    lane = lax.broadcasted_iota(jnp.int32, cos.shape, 1)
    first_half = lane % ATT_HEAD_DIM < ATT_HEAD_DIM // 2

    def rot(xb):
        partner = jnp.where(first_half, -pltpu.roll(xb, 128 - 32, axis=1), pltpu.roll(xb, 32, axis=1))
        return xb * cos + partner * sin

    for cb in range(main_ref.shape[2] // 128):
        xb = main_ref[0, :, cb * 128:(cb + 1) * 128]
        mo_ref[0, :, cb * 128:(cb + 1) * 128] = xb if cb in ROPE_V_BLOCKS else rot(xb)
    tb = tail_ref[0]
    to_ref[0] = jnp.where(lane < IDX_DIM, rot(tb), tb)


def rope_apply(main, tail, pos, *, tr):
    bsz, t, wm = main.shape
    inv_freq = ROPE_THETA ** (-jnp.arange(0, ATT_HEAD_DIM, 2, dtype=F32) / ATT_HEAD_DIM)
    ang = pos.astype(F32)[:, None] * inv_freq[None, :]
    cos = jnp.tile(jnp.cos(ang), (1, 4))
    sin = jnp.tile(jnp.sin(ang), (1, 4))
    return pl.pallas_call(
        _rope_kernel,
        grid=(bsz, t // tr),
        in_specs=[
            pl.BlockSpec((1, tr, wm), lambda b, i: (b, i, 0)),
            pl.BlockSpec((1, tr, 128), lambda b, i: (b, i, 0)),
            pl.BlockSpec((tr, 128), lambda b, i: (i, 0)),
            pl.BlockSpec((tr, 128), lambda b, i: (i, 0)),
        ],
        out_specs=[pl.BlockSpec((1, tr, wm), lambda b, i: (b, i, 0)), pl.BlockSpec((1, tr, 128), lambda b, i: (b, i, 0))],
        out_shape=[jax.ShapeDtypeStruct(main.shape, F32), jax.ShapeDtypeStruct(tail.shape, F32)],
        compiler_params=pltpu.CompilerParams(
            dimension_semantics=("parallel", "parallel"), vmem_limit_bytes=VMEM_LIMIT_BYTES),
        name="rope",
    )(main, tail, cos, sin)


def dsa_select_attend(q, qi, wi, ki_all, q_pos, gather_kv, topk):
    bsz, t = q.shape[:2]
    L = ki_all.shape[1]
    s = jnp.einsum('bthd,bsd->bths', qi.astype(F32), ki_all.astype(F32)) * IDX_DIM ** -0.5
    score = jnp.einsum('bth,bths->bts', wi.astype(F32) * N_IDX_HEADS ** -0.5, jax.nn.relu(s))
    admissible = jnp.arange(L, dtype=jnp.int32)[None, :] <= q_pos[:, None]
    score = jnp.where(admissible[None], score, -jnp.inf)
    _, idx = lax.top_k(score, topk)
    valid = idx <= q_pos[None, :, None]
    k_sel, v_sel = gather_kv(idx)
    qg = q.reshape(bsz, t, N_KV_HEADS, KV_GROUP, ATT_HEAD_DIM)
    logits = jnp.einsum('btjgd,btsjd->btjgs', qg.astype(F32), k_sel.astype(F32)) * ATT_HEAD_DIM ** -0.5
    logits = jnp.where(valid[:, :, None, None, :], logits, -jnp.inf)
    p = jax.nn.softmax(logits, axis=-1)
    o = jnp.einsum('btjgs,btsjd->btjgd', p, v_sel.astype(F32))
    return o.reshape(bsz, t, N_ATT_HEADS * ATT_HEAD_DIM).astype(q.dtype)


def take_rows(a, i):
    return jax.vmap(lambda ab, ib: ab[ib])(a, i)


def run_trunk(x, mem_k, mem_v, conv0, ssm0, attend, pos, p, *, tm, tm_mem):
    conv_out, ssm_out, k_out, v_out, ki_out = [], [], [], [], []
    bsz, t = x.shape[:2]
    m = bsz * t
    x = x.reshape(m, D_MODEL)
    for i in range(DEPTH):
        j = i // N_MIXERS
        g, b = p['ln_g'][i], p['ln_b'][i]
        if i % N_MIXERS == 0:
            zx = proj(x, p['w_ssm_main'][j], tm=tm, tn=512).reshape(bsz, t, -1)
            dt_raw = proj(x, p['w_ssm_dt'][j], tm=tm, tn=128).reshape(bsz, t, -1)
            act, cs = ssm_conv(zx, conv0[j], p['ssm_conv_w'][j], p['ssm_conv_b'][j], tc=min(t, 512))
            h, ss = ssd_scan(act, zx, dt_raw, p['ssm_dt_bias'][j], p['ssm_a_log'][j], p['ssm_d'][j],
                             p['ssm_norm_w'][j], ssm0[j], c=SSM_CHUNK if t % SSM_CHUNK == 0 else t)
            conv_out.append(cs)
            ssm_out.append(ss)
            w_o = p['w_ssm_out'][j]
        else:
            q, k, v, qi, ki, wi = dsa_project(x.reshape(bsz, t, D_MODEL), p['w_att_main'][j], p['w_att_tail'][j],
                                              pos, tm)
            h = attend(j, q.reshape(bsz, t, -1), k.reshape(bsz, t, -1), v.reshape(bsz, t, -1),
                       qi.reshape(bsz, t, -1), ki, wi)
            k_out.append(k)
            v_out.append(v)
            ki_out.append(ki)
            w_o = p['w_att_out'][j]
        x = out_ln(h.reshape(m, -1), w_o, x, g[0], b[0], tm=tm_mem)
        q_mem = proj(x, p['w_mem_q'][i], tm=tm, tn=512).reshape(bsz, t, D_MODEL)
        o_mem = mem_attn(q_mem, mem_k, mem_v, i, tm=min(t, tm_mem))
        x = out_ln(o_mem.reshape(m, D_MODEL), p['w_mem_out'][i], x, g[1], b[1], tm=tm_mem)
        x = ffn_ln(x, p['w_ffn_in'][i], p['w_ffn_out'][i], g[2], b[2], tm=tm)
    return (x.reshape(bsz, t, D_MODEL), jnp.stack(k_out), jnp.stack(v_out), jnp.stack(ki_out),
            jnp.stack(conv_out), jnp.stack(ssm_out))


def kernel(x_prompt, x_sample, cache_k, cache_v, cache_kidx, cache_mem_k, cache_mem_v, state_conv, state_ssm,
           page_table, mem_prompt, w_ssm_in, ssm_conv_w, ssm_conv_b, ssm_dt_bias, ssm_a_log, ssm_d, ssm_norm_w,
           w_ssm_out, w_att_in, w_att_out, w_mem_q, w_mem_kv, w_mem_out, w_ffn_in, w_ffn_out, ln_g, ln_b):
    zx_cols = D_INNER + CONV_DIM
    att_cols = sum(ATT_SPLITS[:4])
    pad128 = lambda w: jnp.pad(w, ((0, 0), (0, 0), (0, 128 - w.shape[-1])))
    params = {'w_ssm_main': w_ssm_in[..., :zx_cols].astype(BF16), 'w_ssm_dt': pad128(w_ssm_in[..., zx_cols:]).astype(BF16),
              'ssm_conv_w': ssm_conv_w, 'ssm_conv_b': ssm_conv_b, 'ssm_dt_bias': ssm_dt_bias,
              'ssm_a_log': ssm_a_log, 'ssm_d': ssm_d, 'ssm_norm_w': ssm_norm_w, 'w_ssm_out': w_ssm_out.astype(BF16),
              'w_att_main': w_att_in[..., :att_cols].astype(BF16), 'w_att_tail': pad128(w_att_in[..., att_cols:]).astype(BF16),
              'w_att_out': w_att_out.astype(BF16),
              'w_mem_q': w_mem_q.astype(BF16), 'w_mem_out': w_mem_out.astype(BF16),
              'w_ffn_in': w_ffn_in.astype(BF16), 'w_ffn_out': w_ffn_out.astype(BF16),
              'ln_g': ln_g, 'ln_b': ln_b}
    bp, seq = x_prompt.shape[:2]
    pos_p = jnp.arange(seq, dtype=jnp.int32)
    w_mem_kv_bf = w_mem_kv.astype(BF16)
    mem_rows = mem_prompt.reshape(bp * N_MEM, D_MODEL)
    mem_kv = jnp.stack([proj(mem_rows, w_mem_kv_bf[l], tm=bp * N_MEM, tn=512) for l in range(DEPTH)])
    memk_prompt = mem_kv[..., :D_MODEL].reshape(DEPTH, bp, N_MEM, N_MEM_HEADS, MEM_HEAD_DIM)
    memv_prompt = mem_kv[..., D_MODEL:].reshape(DEPTH, bp, N_MEM, N_MEM_HEADS, MEM_HEAD_DIM)
    n_ssm = (DEPTH + 1) // 2
    conv0 = jnp.zeros((n_ssm, bp, CONV_W - 1, CONV_DIM), x_prompt.dtype)
    ssm0 = jnp.zeros((n_ssm, bp, N_SSM_HEADS, SSM_HEAD_DIM, D_STATE), x_prompt.dtype)
    attend_p = lambda j, q, k, v, qi, ki, wi: dsa_prompt_attend(q, k, v, qi, ki, wi)
    y_prompt, k_prompt, v_prompt, kidx_prompt, conv_prompt, ssm_prompt = run_trunk(
        x_prompt, memk_prompt, memv_prompt, conv0, ssm0, attend_p, pos_p, params, tm=min(1024, bp * seq), tm_mem=min(512, seq))
    past = page_table.shape[1] * PAGE_SIZE
    bs, ts = x_sample.shape[:2]
    pos_s = past + jnp.arange(ts, dtype=jnp.int32)
    attend_s = lambda j, q, k, v, qi, ki, wi: dsa_sample_attend(q, k, v, qi, ki, wi, cache_k, cache_v, cache_kidx, j,
                                                                page_table)
    y_sample, k_sample, v_sample, kidx_sample, conv_sample, ssm_sample = run_trunk(
        x_sample, cache_mem_k, cache_mem_v, state_conv, state_ssm, attend_s, pos_s, params, tm=bs * ts, tm_mem=bs * ts)
    return (y_prompt, y_sample, k_prompt, v_prompt, kidx_prompt, conv_prompt, ssm_prompt, memk_prompt, memv_prompt,
            k_sample, v_sample, kidx_sample, conv_sample, ssm_sample)
```

```python
import functools
import math

import jax
import jax.numpy as jnp
import numpy as np
from jax import lax
from jax.experimental import pallas as pl
from jax.experimental.pallas import tpu as pltpu

F32 = jnp.float32
BF16 = jnp.bfloat16

DEPTH = 4
N_MIXERS = 2
D_MODEL = 1024
PAGE_SIZE = 128

D_INNER = 2048
SSM_HEAD_DIM = 64
N_SSM_HEADS = 32
SSM_GROUPS = 4
HEADS_PER_GROUP = 8
D_STATE = 128
CONV_W = 4
CONV_DIM = D_INNER + 2 * SSM_GROUPS * D_STATE
SSM_CHUNK = 128

ATT_HEAD_DIM = 64
N_ATT_HEADS = 16
N_KV_HEADS = 4
KV_GROUP = 4
N_IDX_HEADS = 8
IDX_DIM = 64
TOPK_MAX = 256
Q_BLOCK = 128
ATT_SPLITS = [1024, 256, 256, 512, 64, 8]
ROPE_THETA = 10000.0

N_MEM = 256
N_MEM_HEADS = 4
MEM_HEAD_DIM = 256
FFN_HIDDEN = 2816

DEEPNORM_ALPHA = (2 * DEPTH) ** 0.25
LN_EPS = 1e-5

VMEM_LIMIT_BYTES = 48 * 1024 * 1024


def _layer_norm_rows(y, g, b):
    mu = jnp.mean(y, axis=-1, keepdims=True)
    d = y - mu
    var = jnp.mean(d * d, axis=-1, keepdims=True)
    return d * lax.rsqrt(var + LN_EPS) * g + b


def _ffn_ln_kernel(x_ref, wa_ref, wb_ref, wo_ref, g_ref, b_ref, o_ref, xb_ref, acc_ref):
    k = pl.program_id(1)

    @pl.when(k == 0)
    def _():
        xb_ref[...] = x_ref[...].astype(BF16)
        acc_ref[...] = jnp.zeros_like(acc_ref)

    xb = xb_ref[...]
    a = jnp.dot(xb, wa_ref[...], preferred_element_type=F32)
    b = jnp.dot(xb, wb_ref[...], preferred_element_type=F32)
    h = (a * jax.nn.sigmoid(a)) * b
    acc_ref[...] += jnp.dot(h.astype(BF16), wo_ref[...], preferred_element_type=F32)

    @pl.when(k == pl.num_programs(1) - 1)
    def _():
        y = DEEPNORM_ALPHA * x_ref[...] + acc_ref[...]
        o_ref[...] = _layer_norm_rows(y, g_ref[...], b_ref[...])


def ffn_ln(x, w_in, w_out, g, b, *, tm, th=256):
    m, d = x.shape
    hidden = w_out.shape[0]
    nh = hidden // th
    return pl.pallas_call(
        _ffn_ln_kernel,
        grid=(m // tm, nh),
        in_specs=[
            pl.BlockSpec((tm, d), lambda i, k: (i, 0)),
            pl.BlockSpec((d, th), lambda i, k: (0, k)),
            pl.BlockSpec((d, th), lambda i, k: (0, k + nh)),
            pl.BlockSpec((th, d), lambda i, k: (k, 0)),
            pl.BlockSpec((1, d), lambda i, k: (0, 0)),
            pl.BlockSpec((1, d), lambda i, k: (0, 0)),
        ],
        out_specs=pl.BlockSpec((tm, d), lambda i, k: (i, 0)),
        out_shape=jax.ShapeDtypeStruct((m, d), F32),
        scratch_shapes=[pltpu.VMEM((tm, d), BF16), pltpu.VMEM((tm, d), F32)],
        compiler_params=pltpu.CompilerParams(
            dimension_semantics=("parallel", "arbitrary"), vmem_limit_bytes=VMEM_LIMIT_BYTES),
        name="ffn_ln",
    )(x, w_in, w_in, w_out, g.reshape(1, d), b.reshape(1, d))


def _proj_kernel(x_ref, w_ref, o_ref, xb_ref):
    @pl.when(pl.program_id(1) == 0)
    def _():
        xb_ref[...] = x_ref[...].astype(BF16)

    o_ref[...] = jnp.dot(xb_ref[...], w_ref[...], preferred_element_type=F32)


def proj(x, w, *, tm, tn):
    m, kd = x.shape
    n = w.shape[1]
    return pl.pallas_call(
        _proj_kernel,
        grid=(m // tm, n // tn),
        in_specs=[pl.BlockSpec((tm, kd), lambda i, j: (i, 0)), pl.BlockSpec((kd, tn), lambda i, j: (0, j))],
        out_specs=pl.BlockSpec((tm, tn), lambda i, j: (i, j)),
        out_shape=jax.ShapeDtypeStruct((m, n), F32),
        scratch_shapes=[pltpu.VMEM((tm, kd), BF16)],
        compiler_params=pltpu.CompilerParams(
            dimension_semantics=("parallel", "arbitrary"), vmem_limit_bytes=VMEM_LIMIT_BYTES),
        name="proj",
    )(x, w)


def _out_ln_kernel(h_ref, w_ref, x_ref, g_ref, b_ref, o_ref):
    y = DEEPNORM_ALPHA * x_ref[...] + jnp.dot(h_ref[...].astype(BF16), w_ref[...], preferred_element_type=F32)
    o_ref[...] = _layer_norm_rows(y, g_ref[...], b_ref[...])


def out_ln(h, w, x, g, b, *, tm):
    m, kd = h.shape
    d = w.shape[1]
    return pl.pallas_call(
        _out_ln_kernel,
        grid=(m // tm,),
        in_specs=[
            pl.BlockSpec((tm, kd), lambda i: (i, 0)),
            pl.BlockSpec((kd, d), lambda i: (0, 0)),
            pl.BlockSpec((tm, d), lambda i: (i, 0)),
            pl.BlockSpec((1, d), lambda i: (0, 0)),
            pl.BlockSpec((1, d), lambda i: (0, 0)),
        ],
        out_specs=pl.BlockSpec((tm, d), lambda i: (i, 0)),
        out_shape=jax.ShapeDtypeStruct((m, d), F32),
        compiler_params=pltpu.CompilerParams(
            dimension_semantics=("parallel",), vmem_limit_bytes=VMEM_LIMIT_BYTES),
        name="out_ln",
    )(h, w, x, g.reshape(1, d), b.reshape(1, d))


def _mem_attn_kernel(q_ref, mk_ref, mv_ref, o_ref):
    for h in range(N_MEM_HEADS):
        sl = slice(h * MEM_HEAD_DIM, (h + 1) * MEM_HEAD_DIM)
        s = lax.dot_general(q_ref[0, :, sl].astype(BF16), mk_ref[:, h, :].astype(BF16), _NT,
                            preferred_element_type=F32) * MEM_HEAD_DIM ** -0.5
        e = jnp.exp(s - jnp.max(s, axis=1, keepdims=True))
        pv = jnp.dot(e.astype(BF16), mv_ref[:, h, :].astype(BF16), preferred_element_type=F32)
        o_ref[0, :, sl] = pv / jnp.sum(e, axis=1, keepdims=True)


def mem_attn(q, mk, mv, layer, *, tm):
    bsz, t, d = q.shape
    mem_spec = pl.BlockSpec((None, None, N_MEM, N_MEM_HEADS, MEM_HEAD_DIM), lambda b, i: (layer, b, 0, 0, 0))
    return pl.pallas_call(
        _mem_attn_kernel,
        grid=(bsz, t // tm),
        in_specs=[pl.BlockSpec((1, tm, d), lambda b, i: (b, i, 0)), mem_spec, mem_spec],
        out_specs=pl.BlockSpec((1, tm, d), lambda b, i: (b, i, 0)),
        out_shape=jax.ShapeDtypeStruct((bsz, t, d), F32),
        compiler_params=pltpu.CompilerParams(
            dimension_semantics=("parallel", "arbitrary"), vmem_limit_bytes=VMEM_LIMIT_BYTES),
        name="mem_attn",
    )(q, mk, mv)


CONV_COLS = 1024


def _conv_kernel(x_ref, st_ref, w_ref, b_ref, act_ref, last_ref, prev_ref, *, tc):
    @pl.when(pl.program_id(2) == 0)
    def _():
        prev_ref[...] = st_ref[0]

    x = x_ref[0]
    prev = prev_ref[...]
    row = lax.broadcasted_iota(jnp.int32, (8, CONV_COLS), 0)
    acc = jnp.broadcast_to(b_ref[...], x.shape)
    for s in (3, 2, 1):
        rolled = pltpu.roll(x, s, axis=0)
        top = jnp.where(row < s, pltpu.roll(prev, s, axis=0), rolled[0:8])
        shifted = top if tc == 8 else jnp.concatenate([top, rolled[8:]], axis=0)
        acc = acc + shifted * w_ref[CONV_W - 1 - s:CONV_W - s, :]
    acc = acc + x * w_ref[CONV_W - 1:CONV_W, :]
    act_ref[0] = acc * jax.nn.sigmoid(acc)
    prev_ref[...] = x[tc - 8:tc]
    last_ref[0] = x[tc - 8:tc]


def ssm_conv(zx, conv_state, conv_w, conv_b, *, tc):
    bsz, t = zx.shape[:2]
    ncb = CONV_DIM // CONV_COLS
    col0 = D_INNER // CONV_COLS
    st8 = jnp.pad(conv_state, ((0, 0), (8 - (CONV_W - 1), 0), (0, 0)))
    act, last = pl.pallas_call(
        functools.partial(_conv_kernel, tc=tc),
        grid=(bsz, ncb, t // tc),
        in_specs=[
            pl.BlockSpec((1, tc, CONV_COLS), lambda b, c, i: (b, i, col0 + c)),
            pl.BlockSpec((1, 8, CONV_COLS), lambda b, c, i: (b, 0, c)),
            pl.BlockSpec((CONV_W, CONV_COLS), lambda b, c, i: (0, c)),
            pl.BlockSpec((1, CONV_COLS), lambda b, c, i: (0, c)),
        ],
        out_specs=[
            pl.BlockSpec((1, tc, CONV_COLS), lambda b, c, i: (b, i, c)),
            pl.BlockSpec((1, 8, CONV_COLS), lambda b, c, i: (b, 0, c)),
        ],
        out_shape=[jax.ShapeDtypeStruct((bsz, t, CONV_DIM), F32), jax.ShapeDtypeStruct((bsz, 8, CONV_DIM), F32)],
        scratch_shapes=[pltpu.VMEM((8, CONV_COLS), F32)],
        compiler_params=pltpu.CompilerParams(
            dimension_semantics=("parallel", "parallel", "arbitrary"), vmem_limit_bytes=VMEM_LIMIT_BYTES),
        name="ssm_conv",
    )(zx, st8, conv_w, conv_b.reshape(1, CONV_DIM))
    return act, last[:, 8 - (CONV_W - 1):, :]


def _split3(v):
    hi = v.astype(BF16)
    r1 = v - hi.astype(F32)
    mid = r1.astype(BF16)
    lo = (r1 - mid.astype(F32)).astype(BF16)
    return hi, mid, lo


def _dot01_right(v, ones_mat):
    return sum(jnp.dot(p, ones_mat, preferred_element_type=F32) for p in _split3(v))


def _dot01_left(ones_mat, v):
    return sum(jnp.dot(ones_mat, p, preferred_element_type=F32) for p in _split3(v))


def _softplus(x):
    return jnp.maximum(x, 0.0) + jnp.log1p(jnp.exp(-jnp.abs(x)))


def _ssd_kernel(xs_ref, b_ref, c_ref, z_ref, dt_ref, dtt_ref, dtb_ref, dtbt_ref, alog_ref, alogt_ref,
                d_ref, nw_ref, h0_ref, y_ref, hT_ref, st_ref, yacc_ref, *, c):
    ci = pl.program_id(1)

    @pl.when(ci == 0)
    def _():
        st_ref[...] = h0_ref[0]

    gw = HEADS_PER_GROUP * SSM_HEAD_DIM
    dt = _softplus(dt_ref[0][:, 0:N_SSM_HEADS] + dtb_ref[...])
    dtt = _softplus(dtt_ref[0] + dtbt_ref[...])
    a = dt * -jnp.exp(alog_ref[...])
    at = dtt * -jnp.exp(alogt_ref[...])
    ri = lax.broadcasted_iota(jnp.int32, (c, c), 0)
    cj = lax.broadcasted_iota(jnp.int32, (c, c), 1)
    causal = ri >= cj
    acum = _dot01_left(jnp.where(causal, 1.0, 0.0).astype(BF16), a)
    acum_t = _dot01_right(at, jnp.where(ri <= cj, 1.0, 0.0).astype(BF16))
    a_last = acum[c - 1:c, :]
    expand = jnp.where(lax.broadcasted_iota(jnp.int32, (N_SSM_HEADS, D_INNER), 1) // SSM_HEAD_DIM
                       == lax.broadcasted_iota(jnp.int32, (N_SSM_HEADS, D_INNER), 0), 1.0, 0.0).astype(BF16)
    e_dt = _dot01_right(dt, expand)
    e_in = _dot01_right(jnp.exp(acum), expand)
    e_out = _dot01_right(jnp.exp(a_last - acum), expand)
    e_chunk = e_in[c - 1:c, :]
    xs = xs_ref[0]
    xdt = xs * e_dt
    head_of_col = lax.broadcasted_iota(jnp.int32, (c, gw), 1) // SSM_HEAD_DIM
    for g in range(SSM_GROUPS):
        gs = slice(g * gw, (g + 1) * gw)
        bg = b_ref[0][:, g * D_STATE:(g + 1) * D_STATE].astype(BF16)
        cg = c_ref[0][:, g * D_STATE:(g + 1) * D_STATE].astype(BF16)
        cb = lax.dot_general(cg, bg, _NT, preferred_element_type=F32)
        xg = xdt[:, gs]
        yg = jnp.zeros((c, gw), F32)
        for r in range(HEADS_PER_GROUP):
            h = g * HEADS_PER_GROUP + r
            seg = acum[:, h:h + 1] - acum_t[h:h + 1, :]
            m = (cb * jnp.exp(jnp.where(causal, seg, -jnp.inf))).astype(BF16)
            xm = jnp.where(head_of_col == r, xg, 0.0).astype(BF16)
            yg = yg + jnp.dot(m, xm, preferred_element_type=F32)
        state = st_ref[g]
        y_off = jnp.dot(cg, state.astype(BF16), preferred_element_type=F32) * e_in[:, gs]
        yacc_ref[:, gs] = yg + y_off + xs[:, gs] * d_ref[:, gs]
        xd = (xg * e_out[:, gs]).astype(BF16)
        st_ref[g] = state * e_chunk[:, gs] + lax.dot_general(bg, xd, (((0,), (0,)), ((), ())),
                                                             preferred_element_type=F32)
    z = z_ref[0]
    yz = yacc_ref[...] * (z * jax.nn.sigmoid(z))
    y_ref[0] = yz * lax.rsqrt(jnp.mean(yz * yz, axis=-1, keepdims=True) + LN_EPS) * nw_ref[...]

    @pl.when(ci == pl.num_programs(1) - 1)
    def _():
        hT_ref[0] = st_ref[...]


def ssd_scan(act, zx, dt_raw, dt_bias, a_log, d_skip, norm_w, ssm_state, *, c):
    bsz, t = act.shape[:2]
    gw = HEADS_PER_GROUP * SSM_HEAD_DIM
    bc_w = SSM_GROUPS * D_STATE
    h0 = ssm_state.reshape(bsz, SSM_GROUPS, HEADS_PER_GROUP, SSM_HEAD_DIM, D_STATE)
    h0 = h0.transpose(0, 1, 4, 2, 3).reshape(bsz, SSM_GROUPS, D_STATE, gw)
    dtt = jnp.swapaxes(dt_raw[..., :N_SSM_HEADS], 1, 2)
    row = lambda v: v.reshape(1, -1)
    col = lambda v: v.reshape(-1, 1)
    full = lambda shape: pl.BlockSpec(shape, lambda b, i: (0,) * len(shape))
    y, h_t = pl.pallas_call(
        functools.partial(_ssd_kernel, c=c),
        grid=(bsz, t // c),
        in_specs=[
            pl.BlockSpec((1, c, D_INNER), lambda b, i: (b, i, 0)),
            pl.BlockSpec((1, c, bc_w), lambda b, i: (b, i, D_INNER // bc_w)),
            pl.BlockSpec((1, c, bc_w), lambda b, i: (b, i, D_INNER // bc_w + 1)),
            pl.BlockSpec((1, c, D_INNER), lambda b, i: (b, i, 0)),
            pl.BlockSpec((1, c, 128), lambda b, i: (b, i, 0)),
            pl.BlockSpec((1, N_SSM_HEADS, c), lambda b, i: (b, 0, i)),
            full((1, N_SSM_HEADS)), full((N_SSM_HEADS, 1)), full((1, N_SSM_HEADS)), full((N_SSM_HEADS, 1)),
            full((1, D_INNER)), full((1, D_INNER)),
            pl.BlockSpec((1, SSM_GROUPS, D_STATE, gw), lambda b, i: (b, 0, 0, 0)),
        ],
        out_specs=[
            pl.BlockSpec((1, c, D_INNER), lambda b, i: (b, i, 0)),
            pl.BlockSpec((1, SSM_GROUPS, D_STATE, gw), lambda b, i: (b, 0, 0, 0)),
        ],
        out_shape=[jax.ShapeDtypeStruct((bsz, t, D_INNER), F32),
                   jax.ShapeDtypeStruct((bsz, SSM_GROUPS, D_STATE, gw), F32)],
        scratch_shapes=[pltpu.VMEM((SSM_GROUPS, D_STATE, gw), F32), pltpu.VMEM((c, D_INNER), F32)],
        compiler_params=pltpu.CompilerParams(
            dimension_semantics=("parallel", "arbitrary"), vmem_limit_bytes=VMEM_LIMIT_BYTES),
        name="ssd_scan",
    )(act, act, act, zx, dt_raw, dtt, row(dt_bias), col(dt_bias), row(a_log), col(a_log),
      row(jnp.repeat(d_skip, SSM_HEAD_DIM)), row(norm_w), h0)
    h_t = h_t.reshape(bsz, SSM_GROUPS, D_STATE, HEADS_PER_GROUP, SSM_HEAD_DIM)
    return y, h_t.transpose(0, 1, 3, 4, 2).reshape(bsz, N_SSM_HEADS, SSM_HEAD_DIM, D_STATE)


INT_MIN = -2 ** 31
NEG_BIG = -1e30
_NT = (((1,), (1,)), ((), ()))


def _float_key(x):
    x = jnp.where(x == 0.0, 0.0, x)
    bits = lax.bitcast_convert_type(x, jnp.int32)
    return bits ^ ((bits >> 31) & 0x7FFFFFFF)


def _dsa_prompt_kernel(q_ref, qi_ref, wi_ref, k_ref, vx_ref, ki_ref, o_ref,
                       qs_ref, qis_ref, wib_ref, key_ref, m_ref, acc_ref, *, tq, tk, topk):
    i = pl.program_id(1)
    nkb = ((i + 1) * tq + tk - 1) // tk
    nl = tk // 128

    def lanes(x):
        return jnp.concatenate([x] * nl, axis=1)

    for h in range(N_ATT_HEADS):
        j, g = divmod(h, KV_GROUP)
        qs_ref[j, g * tq:(g + 1) * tq, :] = q_ref[0, :, h * 64:(h + 1) * 64]
    for h in range(N_IDX_HEADS):
        qis_ref[h * tq:(h + 1) * tq, :] = qi_ref[0, :, h * 64:(h + 1) * 64]
        wib_ref[h] = jnp.broadcast_to(wi_ref[0, :, h:h + 1] * (N_IDX_HEADS ** -0.5), (tq, 128))
    q_pos = i * tq + lax.broadcasted_iota(jnp.int32, (tq, 1), 0)

    def score_block(kb, c):
        off = pl.multiple_of(kb * tk, tk)
        s = lax.dot_general(qis_ref[...], ki_ref[0, pl.ds(off, tk), :], _NT, preferred_element_type=F32)
        acc = jnp.zeros((tq, tk), F32)
        for h in range(N_IDX_HEADS):
            acc = acc + lanes(wib_ref[h]) * jnp.maximum(s[h * tq:(h + 1) * tq, :], 0.0)
        kpos = off + lax.broadcasted_iota(jnp.int32, (tq, tk), 1)
        key_ref[kb] = jnp.where(kpos <= q_pos, _float_key(acc), INT_MIN)
        return c

    lax.fori_loop(0, nkb, score_block, 0)

    def count(cand, strict):
        def body(kb, acc):
            blk = key_ref[kb]
            for c in range(nl):
                part = blk[:, c * 128:(c + 1) * 128]
                hit = (part > cand) if strict else (part >= cand)
                acc = acc + jnp.where(hit, 1.0, 0.0)
            return acc
        acc = lax.fori_loop(0, nkb, body, jnp.zeros((tq, 128), F32))
        return jnp.sum(acc, axis=1, keepdims=True)

    kf = float(topk)
    zero = jnp.zeros((tq, 128), jnp.int32)
    thr = jnp.where(count(zero, False) >= kf, zero, jnp.full((tq, 128), INT_MIN, jnp.int32))

    def bit_step(it, thr):
        cand = thr | jnp.left_shift(jnp.int32(1), 30 - it)
        return jnp.where(count(cand, False) >= kf, cand, thr)

    thr = lax.fori_loop(0, 31, bit_step, thr)
    need = jnp.where(thr == INT_MIN, 0.0, kf - count(thr, True))

    m_ref[...] = jnp.full(m_ref.shape, NEG_BIG, F32)
    acc_ref[...] = jnp.zeros(acc_ref.shape, F32)
    upper = jnp.where(lax.broadcasted_iota(jnp.int32, (tk, tk), 0) < lax.broadcasted_iota(jnp.int32, (tk, tk), 1),
                      1.0, 0.0).astype(BF16)
    thr_t, need_t = lanes(thr), lanes(need)

    def attend_block(kb, eq_before):
        off = pl.multiple_of(kb * tk, tk)
        key = key_ref[kb]
        eq = key == thr_t
        eqf = jnp.where(eq, 1.0, 0.0)
        rank = jnp.dot(eqf.astype(BF16), upper, preferred_element_type=F32) + lanes(eq_before)
        bias = jnp.where(key > thr_t, 0.0, jnp.where(eq, jnp.where(rank < need_t, 0.0, NEG_BIG), NEG_BIG))
        bias4 = jnp.concatenate([bias] * KV_GROUP, axis=0)
        for j in range(N_KV_HEADS):
            kblk = k_ref[0, pl.ds(off, tk), j * 64:(j + 1) * 64]
            s = lax.dot_general(qs_ref[j], kblk, _NT, preferred_element_type=F32) + bias4
            m_old = m_ref[j]
            mx = s[:, 0:128]
            for c in range(1, nl):
                mx = jnp.maximum(mx, s[:, c * 128:(c + 1) * 128])
            m_new = jnp.maximum(m_old, jnp.max(mx, axis=1, keepdims=True))
            p = jnp.exp(s - lanes(m_new))
            pv = jnp.dot(p.astype(BF16), vx_ref[0, j, pl.ds(off, tk), :], preferred_element_type=F32)
            acc_ref[j] = jnp.exp(m_old - m_new) * acc_ref[j] + pv
            m_ref[j] = m_new
        return eq_before + jnp.sum(eqf, axis=1, keepdims=True)

    lax.fori_loop(0, nkb, attend_block, jnp.zeros((tq, 128), F32))

    for h in range(N_ATT_HEADS):
        j, g = divmod(h, KV_GROUP)
        a = acc_ref[j, g * tq:(g + 1) * tq, :]
        o_ref[0, :, h * 64:(h + 1) * 64] = a[:, 0:64] / a[:, 64:65]


def dsa_prompt_attend(q, k, v, qi, ki, wi, *, tq=128, tk=256):
    bsz, t = q.shape[:2]
    topk = min(TOPK_MAX, t // 4)
    q_bf = (q * ATT_HEAD_DIM ** -0.5).astype(BF16)
    qi_bf = (qi * IDX_DIM ** -0.5).astype(BF16)
    vh = v.reshape(bsz, t, N_KV_HEADS, ATT_HEAD_DIM).transpose(0, 2, 1, 3)
    vx = jnp.concatenate([vh, jnp.ones_like(vh)], axis=-1).astype(BF16)
    kern = functools.partial(_dsa_prompt_kernel, tq=tq, tk=tk, topk=topk)
    return pl.pallas_call(
        kern,
        grid=(bsz, t // tq),
        in_specs=[
            pl.BlockSpec((1, tq, 1024), lambda b, i: (b, i, 0)),
            pl.BlockSpec((1, tq, 512), lambda b, i: (b, i, 0)),
            pl.BlockSpec((1, tq, N_IDX_HEADS), lambda b, i: (b, i, 0)),
            pl.BlockSpec((1, t, 256), lambda b, i: (b, 0, 0)),
            pl.BlockSpec((1, N_KV_HEADS, t, 128), lambda b, i: (b, 0, 0, 0)),
            pl.BlockSpec((1, t, IDX_DIM), lambda b, i: (b, 0, 0)),
        ],
        out_specs=pl.BlockSpec((1, tq, 1024), lambda b, i: (b, i, 0)),
        out_shape=jax.ShapeDtypeStruct((bsz, t, 1024), F32),
        scratch_shapes=[
            pltpu.VMEM((N_KV_HEADS, KV_GROUP * tq, 64), BF16),
            pltpu.VMEM((N_IDX_HEADS * tq, 64), BF16),
            pltpu.VMEM((N_IDX_HEADS, tq, 128), F32),
            pltpu.VMEM((t // tk, tq, tk), jnp.int32),
            pltpu.VMEM((N_KV_HEADS, KV_GROUP * tq, 128), F32),
            pltpu.VMEM((N_KV_HEADS, KV_GROUP * tq, 128), F32),
        ],
        compiler_params=pltpu.CompilerParams(
            dimension_semantics=("parallel", "arbitrary"), vmem_limit_bytes=VMEM_LIMIT_BYTES),
        name="dsa_prompt_attend",
    )(q_bf, qi_bf, wi, k.astype(BF16), vx, ki.astype(BF16))


def _dsa_prompt_kernel_t(qt_ref, qit_ref, wit_ref, k_ref, vxt_ref, ki_ref, o_ref, key_ref, m_ref, acc_ref,
                         s_ref, p_ref, *, tq, tk, topk):
    i = pl.program_id(1)
    nkb = ((i + 1) * tq + tk - 1) // tk
    gq = KV_GROUP * tq
    rep8 = lambda v: jnp.broadcast_to(v, (8, tq))
    as3 = lambda x: x.reshape(tk // 8, 8, tq)
    w_heads = wit_ref[0, 0] * (N_IDX_HEADS ** -0.5)
    q_pos = i * tq + lax.broadcasted_iota(jnp.int32, (tk, tq), 1)

    def score_block(kb, c):
        off = pl.multiple_of(kb * tk, tk)
        s = jnp.dot(ki_ref[0, pl.ds(off, tk), :], qit_ref[0, 0], preferred_element_type=F32)
        acc = jnp.zeros((tk, tq), F32)
        for h in range(N_IDX_HEADS):
            acc = acc + w_heads[h:h + 1, :] * jnp.maximum(s[:, h * tq:(h + 1) * tq], 0.0)
        kpos = off + lax.broadcasted_iota(jnp.int32, (tk, tq), 0)
        key_ref[kb] = jnp.where(kpos <= q_pos, _float_key(acc), INT_MIN)
        return c

    lax.fori_loop(0, nkb, score_block, 0)

    def count(cand, strict):
        def body(kb, accs):
            blk = key_ref[kb]
            accs = list(accs)
            for r in range(tk // 8):
                part = blk[r * 8:(r + 1) * 8, :]
                hit = (part > cand) if strict else (part >= cand)
                accs[r % 4] = accs[r % 4] + jnp.where(hit, 1.0, 0.0)
            return tuple(accs)
        a0, a1, a2, a3 = lax.fori_loop(0, nkb, body, (jnp.zeros((8, tq), F32),) * 4)
        return jnp.sum((a0 + a1) + (a2 + a3), axis=0, keepdims=True)

    kf = float(topk)
    zero = jnp.zeros((8, tq), jnp.int32)
    thr = jnp.where(rep8(count(zero, False)) >= kf, zero, jnp.full((8, tq), INT_MIN, jnp.int32))

    def bit_step(it, thr):
        cand = thr | jnp.left_shift(jnp.int32(1), 30 - it)
        return jnp.where(rep8(count(cand, False)) >= kf, cand, thr)

    thr = lax.fori_loop(0, 31, bit_step, thr)
    need = jnp.where(thr == INT_MIN, 0.0, kf - rep8(count(thr, True)))

    m_ref[...] = jnp.full(m_ref.shape, NEG_BIG, F32)
    acc_ref[...] = jnp.zeros(acc_ref.shape, F32)
    lower = jnp.where(lax.broadcasted_iota(jnp.int32, (tk, tk), 1) < lax.broadcasted_iota(jnp.int32, (tk, tk), 0),
                      1.0, 0.0).astype(BF16)

    def attend_block(kb, eq_before):
        off = pl.multiple_of(kb * tk, tk)
        key3 = as3(key_ref[kb])
        eq3 = key3 == thr[None]
        eqf = jnp.where(eq3, 1.0, 0.0)
        rank3 = as3(jnp.dot(lower, eqf.reshape(tk, tq).astype(BF16), preferred_element_type=F32)) + eq_before[None]
        bias = jnp.where(key3 > thr[None], 0.0,
                         jnp.where(eq3, jnp.where(rank3 < need[None], 0.0, NEG_BIG), NEG_BIG)).reshape(tk, tq)
        bias4 = jnp.concatenate([bias] * KV_GROUP, axis=1)
        col_max = []
        for j in range(N_KV_HEADS):
            kblk = k_ref[0, pl.ds(off, tk), j * 64:(j + 1) * 64]
            s = jnp.dot(kblk, qt_ref[0, 0, j], preferred_element_type=F32) + bias4
            s_ref[j] = s
            col_max.append(jnp.max(s, axis=0, keepdims=True))
        alpha = []
        for j in range(N_KV_HEADS):
            m_old = m_ref[j]
            m_new = jnp.maximum(m_old, col_max[j])
            alpha.append(jnp.exp(m_old - m_new))
            p_ref[j] = jnp.exp(s_ref[j] - m_new).astype(BF16)
            m_ref[j] = m_new
        for j in range(N_KV_HEADS):
            pv = jnp.dot(vxt_ref[0, j, kb], p_ref[j], preferred_element_type=F32)
            acc_ref[j] = alpha[j] * acc_ref[j] + pv
        return eq_before + rep8(jnp.sum(jnp.sum(eqf, axis=0), axis=0, keepdims=True))

    lax.fori_loop(0, nkb, attend_block, jnp.zeros((8, tq), F32))

    for j in range(N_KV_HEADS):
        a = acc_ref[j]
        o_t = a[0:64, :] / a[64:65, :]
        for g in range(KV_GROUP):
            h = j * KV_GROUP + g
            o_ref[0, :, h * 64:(h + 1) * 64] = o_t[:, g * tq:(g + 1) * tq].T


def dsa_prompt_attend_t(q, k, v, qi, ki, wi, *, tq=128, tk=256):
    bsz, t = q.shape[:2]
    nq, nk = t // tq, t // tk
    topk = min(TOPK_MAX, t // 4)
    qt = (q * ATT_HEAD_DIM ** -0.5).astype(BF16).reshape(bsz, nq, tq, N_KV_HEADS, KV_GROUP, 64)
    qt = qt.transpose(0, 1, 3, 5, 4, 2).reshape(bsz, nq, N_KV_HEADS, 64, KV_GROUP * tq)
    qit = (qi * IDX_DIM ** -0.5).astype(BF16).reshape(bsz, nq, tq, N_IDX_HEADS, 64)
    qit = qit.transpose(0, 1, 4, 3, 2).reshape(bsz, nq, 64, N_IDX_HEADS * tq)
    wit = wi.reshape(bsz, nq, tq, N_IDX_HEADS).transpose(0, 1, 3, 2)
    vt = v.astype(BF16).reshape(bsz, nk, tk, N_KV_HEADS, 64).transpose(0, 3, 1, 4, 2)
    vxt = jnp.concatenate([vt, jnp.ones_like(vt)], axis=3)
    kern = functools.partial(_dsa_prompt_kernel_t, tq=tq, tk=tk, topk=topk)
    return pl.pallas_call(
        kern,
        grid=(bsz, nq),
        in_specs=[
            pl.BlockSpec((1, 1, N_KV_HEADS, 64, KV_GROUP * tq), lambda b, i: (b, i, 0, 0, 0)),
            pl.BlockSpec((1, 1, 64, N_IDX_HEADS * tq), lambda b, i: (b, i, 0, 0)),
            pl.BlockSpec((1, 1, N_IDX_HEADS, tq), lambda b, i: (b, i, 0, 0)),
            pl.BlockSpec((1, t, 256), lambda b, i: (b, 0, 0)),
            pl.BlockSpec((1, N_KV_HEADS, nk, 128, tk), lambda b, i: (b, 0, 0, 0, 0)),
            pl.BlockSpec((1, t, IDX_DIM), lambda b, i: (b, 0, 0)),
        ],
        out_specs=pl.BlockSpec((1, tq, 1024), lambda b, i: (b, i, 0)),
        out_shape=jax.ShapeDtypeStruct((bsz, t, 1024), F32),
        scratch_shapes=[
            pltpu.VMEM((nk, tk, tq), jnp.int32),
            pltpu.VMEM((N_KV_HEADS, 1, KV_GROUP * tq), F32),
            pltpu.VMEM((N_KV_HEADS, 128, KV_GROUP * tq), F32),
            pltpu.VMEM((N_KV_HEADS, tk, KV_GROUP * tq), F32),
            pltpu.VMEM((N_KV_HEADS, tk, KV_GROUP * tq), BF16),
        ],
        compiler_params=pltpu.CompilerParams(
            dimension_semantics=("parallel", "arbitrary"), vmem_limit_bytes=VMEM_LIMIT_BYTES),
        name="dsa_prompt_attend",
    )(qt, qit, wit, k.astype(BF16), vxt, ki.astype(BF16))


PAGES_PER_STEP = 8


def _sample_select_kernel(pt_ref, qi_ref, wi_ref, kinew_ref, *rest, t, n_steps, topk):
    ki_refs = rest[:PAGES_PER_STEP]
    bias_ref, qis_ref, wib_ref, key_ref = rest[PAGES_PER_STEP:]
    p = pl.program_id(1)
    w = PAGES_PER_STEP * PAGE_SIZE
    nl = w // 128

    @pl.when(p == 0)
    def _():
        for h in range(N_IDX_HEADS):
            qis_ref[h * t:(h + 1) * t, :] = qi_ref[0, :, h * 64:(h + 1) * 64]
            wib_ref[h] = jnp.broadcast_to(wi_ref[0, :, h:h + 1] * (N_IDX_HEADS ** -0.5), (t, 128))

    def scores(s):
        acc = jnp.zeros((t, s.shape[1]), F32)
        for h in range(N_IDX_HEADS):
            wfull = jnp.concatenate([wib_ref[h]] * (s.shape[1] // 128), axis=1)
            acc = acc + wfull * jnp.maximum(s[h * t:(h + 1) * t, :], 0.0)
        return acc

    ki_t = jnp.concatenate([r[...] for r in ki_refs], axis=1).astype(BF16)
    key_ref[p] = _float_key(scores(jnp.dot(qis_ref[...].astype(BF16), ki_t, preferred_element_type=F32)))

    @pl.when(p == n_steps - 1)
    def _():
        knew = _float_key(scores(lax.dot_general(qis_ref[...].astype(BF16), kinew_ref[0].astype(BF16), _NT,
                                                 preferred_element_type=F32)))
        n_idx = lax.broadcasted_iota(jnp.int32, (t, 128), 1)
        r_idx = lax.broadcasted_iota(jnp.int32, (t, 128), 0)
        knew = jnp.where(n_idx <= r_idx, knew, INT_MIN)
        key_ref[n_steps] = jnp.concatenate([knew, jnp.full((t, w - 128), INT_MIN, jnp.int32)], axis=1)

        def count(cand, strict):
            def body(blk, acc):
                kb = key_ref[blk]
                for c in range(nl):
                    part = kb[:, c * 128:(c + 1) * 128]
                    hit = (part > cand) if strict else (part >= cand)
                    acc = acc + jnp.where(hit, 1.0, 0.0)
                return acc
            acc = lax.fori_loop(0, n_steps + 1, body, jnp.zeros((t, 128), F32))
            return jnp.sum(acc, axis=1, keepdims=True)

        kf = float(topk)
        zero = jnp.zeros((t, 128), jnp.int32)
        thr = jnp.where(count(zero, False) >= kf, zero, jnp.full((t, 128), INT_MIN, jnp.int32))

        def bit_step(it, thr):
            cand = thr | jnp.left_shift(jnp.int32(1), 30 - it)
            return jnp.where(count(cand, False) >= kf, cand, thr)

        thr = lax.fori_loop(0, 31, bit_step, thr)
        need = jnp.where(thr == INT_MIN, 0.0, kf - count(thr, True))
        upper = jnp.where(lax.broadcasted_iota(jnp.int32, (128, 128), 0)
                          < lax.broadcasted_iota(jnp.int32, (128, 128), 1), 1.0, 0.0).astype(BF16)

        def emit(blk, eq_before):
            kb = key_ref[blk]
            parts = []
            for c in range(nl):
                part = kb[:, c * 128:(c + 1) * 128]
                eq = part == thr
                eqf = jnp.where(eq, 1.0, 0.0)
                rank = jnp.dot(eqf.astype(BF16), upper, preferred_element_type=F32) + eq_before
                parts.append(jnp.where(part > thr, 0.0,
                                       jnp.where(eq, jnp.where(rank < need, 0.0, NEG_BIG), NEG_BIG)))
                eq_before = eq_before + jnp.sum(eqf, axis=1, keepdims=True)
            bias_ref[0, blk] = jnp.concatenate(parts, axis=1)
            return eq_before

        lax.fori_loop(0, n_steps + 1, emit, jnp.zeros((t, 128), F32))


def _sample_attend_kernel(pt_ref, q_ref, bias_ref, biasnew_ref, knew_ref, vnew_ref, *rest, t, n_steps):
    k_refs = rest[:PAGES_PER_STEP]
    v_refs = rest[PAGES_PER_STEP:2 * PAGES_PER_STEP]
    o_ref, qs_ref, m_ref, l_ref, acc_ref = rest[2 * PAGES_PER_STEP:]
    p = pl.program_id(1)
    rows = KV_GROUP * t

    @pl.when(p == 0)
    def _():
        for h in range(N_ATT_HEADS):
            j, g = divmod(h, KV_GROUP)
            qs_ref[j, g * t:(g + 1) * t, :] = q_ref[0, :, h * 64:(h + 1) * 64]
        m_ref[...] = jnp.full(m_ref.shape, NEG_BIG, F32)
        l_ref[...] = jnp.zeros(l_ref.shape, F32)
        acc_ref[...] = jnp.zeros(acc_ref.shape, F32)

    def update(k_t, v_t, bias):
        bias4 = jnp.concatenate([bias] * KV_GROUP, axis=0)
        for j in range(N_KV_HEADS):
            s = jnp.dot(qs_ref[j].astype(BF16), k_t(j).astype(BF16), preferred_element_type=F32) + bias4
            m_old = m_ref[j]
            m_new = jnp.maximum(m_old, jnp.max(s, axis=1, keepdims=True))
            alpha = jnp.exp(m_old - m_new)
            pr = jnp.exp(s - m_new[:, 0:1])
            l_ref[j] = alpha * l_ref[j] + jnp.sum(pr, axis=1, keepdims=True)
            pv = lax.dot_general(pr.astype(BF16), v_t(j).astype(BF16), _NT, preferred_element_type=F32)
            acc_ref[j] = alpha[:, 0:64] * acc_ref[j] + pv
            m_ref[j] = m_new

    update(lambda j: jnp.concatenate([r[j] for r in k_refs], axis=1),
           lambda j: jnp.concatenate([r[j] for r in v_refs], axis=1), bias_ref[0, 0])

    @pl.when(p == n_steps - 1)
    def _():
        update(lambda j: knew_ref[0, j * 64:(j + 1) * 64, :], lambda j: vnew_ref[0, j * 64:(j + 1) * 64, :],
               biasnew_ref[0, 0, :, 0:128])
        for h in range(N_ATT_HEADS):
            j, g = divmod(h, KV_GROUP)
            sl = slice(g * t, (g + 1) * t)
            o_ref[0, :, h * 64:(h + 1) * 64] = acc_ref[j, sl, :] / l_ref[j, sl, 0:64]


def dsa_sample_attend(q, k, v, qi, ki, wi, cache_k, cache_v, cache_kidx, layer, page_table):
    bsz, t = q.shape[:2]
    n_pages = page_table.shape[1]
    past = n_pages * PAGE_SIZE
    n_steps = n_pages // PAGES_PER_STEP
    w = PAGES_PER_STEP * PAGE_SIZE
    topk = min(TOPK_MAX, (past + t) // 4)
    pad = lambda a: jnp.pad(a, ((0, 0), (0, PAGE_SIZE - t), (0, 0)))
    pool_k = cache_k.transpose(0, 1, 3, 4, 2)
    pool_v = cache_v.transpose(0, 1, 3, 4, 2)
    pool_ki = cache_kidx.transpose(0, 1, 3, 2)
    knew_t = jnp.swapaxes(pad(k), 1, 2)
    vnew_t = jnp.swapaxes(pad(v), 1, 2)

    def page_spec(r, *major):
        zeros = (0,) * (1 + len(major))
        return pl.BlockSpec((None, None) + major + (PAGE_SIZE,),
                            lambda b, p, pt: (layer, pt[b, p * PAGES_PER_STEP + r]) + zeros)

    bias = pl.pallas_call(
        functools.partial(_sample_select_kernel, t=t, n_steps=n_steps, topk=topk),
        grid_spec=pltpu.PrefetchScalarGridSpec(
            num_scalar_prefetch=1,
            grid=(bsz, n_steps),
            in_specs=[
                pl.BlockSpec((1, t, 512), lambda b, p, pt: (b, 0, 0)),
                pl.BlockSpec((1, t, N_IDX_HEADS), lambda b, p, pt: (b, 0, 0)),
                pl.BlockSpec((1, PAGE_SIZE, IDX_DIM), lambda b, p, pt: (b, 0, 0)),
            ] + [page_spec(r, IDX_DIM) for r in range(PAGES_PER_STEP)],
            out_specs=pl.BlockSpec((1, n_steps + 1, t, w), lambda b, p, pt: (b, 0, 0, 0)),
            scratch_shapes=[
                pltpu.VMEM((N_IDX_HEADS * t, 64), F32),
                pltpu.VMEM((N_IDX_HEADS, t, 128), F32),
                pltpu.VMEM((n_steps + 1, t, w), jnp.int32),
            ]),
        out_shape=jax.ShapeDtypeStruct((bsz, n_steps + 1, t, w), F32),
        compiler_params=pltpu.CompilerParams(
            dimension_semantics=("parallel", "arbitrary"), vmem_limit_bytes=VMEM_LIMIT_BYTES),
        name="dsa_sample_select",
    )(page_table, qi * IDX_DIM ** -0.5, wi, pad(ki), *([pool_ki] * PAGES_PER_STEP))

    return pl.pallas_call(
        functools.partial(_sample_attend_kernel, t=t, n_steps=n_steps),
        grid_spec=pltpu.PrefetchScalarGridSpec(
            num_scalar_prefetch=1,
            grid=(bsz, n_steps),
            in_specs=[
                pl.BlockSpec((1, t, 1024), lambda b, p, pt: (b, 0, 0)),
                pl.BlockSpec((1, 1, t, w), lambda b, p, pt: (b, p, 0, 0)),
                pl.BlockSpec((1, 1, t, w), lambda b, p, pt: (b, n_steps, 0, 0)),
                pl.BlockSpec((1, 256, PAGE_SIZE), lambda b, p, pt: (b, 0, 0)),
                pl.BlockSpec((1, 256, PAGE_SIZE), lambda b, p, pt: (b, 0, 0)),
            ] + [page_spec(r, N_KV_HEADS, ATT_HEAD_DIM) for r in range(PAGES_PER_STEP)] * 2,
            out_specs=pl.BlockSpec((1, t, 1024), lambda b, p, pt: (b, 0, 0)),
            scratch_shapes=[
                pltpu.VMEM((N_KV_HEADS, KV_GROUP * t, 64), F32),
                pltpu.VMEM((N_KV_HEADS, KV_GROUP * t, 128), F32),
                pltpu.VMEM((N_KV_HEADS, KV_GROUP * t, 128), F32),
                pltpu.VMEM((N_KV_HEADS, KV_GROUP * t, 64), F32),
            ]),
        out_shape=jax.ShapeDtypeStruct((bsz, t, 1024), F32),
        compiler_params=pltpu.CompilerParams(
            dimension_semantics=("parallel", "arbitrary"), vmem_limit_bytes=VMEM_LIMIT_BYTES),
        name="dsa_sample_attend",
    )(page_table, q * ATT_HEAD_DIM ** -0.5, bias, bias, knew_t, vnew_t,
      *([pool_k] * PAGES_PER_STEP), *([pool_v] * PAGES_PER_STEP))


def layer_norm(x, g, b):
    xf = x.astype(F32)
    mu = jnp.mean(xf, -1, keepdims=True)
    var = jnp.mean(jnp.square(xf - mu), -1, keepdims=True)
    return ((xf - mu) * lax.rsqrt(var + LN_EPS) * g.astype(F32) + b.astype(F32)).astype(x.dtype)


def rope(x, pos):
    d = x.shape[-1]
    inv_freq = ROPE_THETA ** (-jnp.arange(0, d, 2, dtype=F32) / d)
    ang = pos.astype(F32)[:, None] * inv_freq[None, :]
    cos = jnp.cos(ang)[:, None, :]
    sin = jnp.sin(ang)[:, None, :]
    x1, x2 = jnp.split(x.astype(F32), 2, axis=-1)
    return jnp.concatenate([x1 * cos - x2 * sin, x1 * sin + x2 * cos], -1).astype(x.dtype)


def ssd_chunked(xdt, a, bm, cm, h0, chunk):
    bsz, L, G, R, P = xdt.shape
    N = bm.shape[-1]
    nc = L // chunk
    x = xdt.reshape(bsz, nc, chunk, G, R, P)
    a = a.reshape(bsz, nc, chunk, G, R)
    bm = bm.reshape(bsz, nc, chunk, G, N)
    cm = cm.reshape(bsz, nc, chunk, G, N)
    acum = jnp.cumsum(a, axis=2)
    acum_t = jnp.moveaxis(acum, 2, -1)
    seg = acum_t[..., :, None] - acum_t[..., None, :]
    causal = jnp.tril(jnp.ones((chunk, chunk), dtype=bool))
    lmat = jnp.exp(jnp.where(causal, seg, -jnp.inf))
    cb = jnp.einsum('bclgn,bcsgn->bcgls', cm, bm)
    y_diag = jnp.einsum('bcgrls,bcsgrp->bclgrp', cb[:, :, :, None] * lmat, x)
    decay = jnp.exp(acum[:, :, -1:] - acum)
    states = jnp.einsum('bclgn,bclgrp->bcgrpn', bm, x * decay[..., None])
    chunk_decay = jnp.exp(acum[:, :, -1])

    def step(h, inp):
        st, dec = inp
        return h * dec[..., None, None] + st, h

    h_last, h_prev = lax.scan(step, h0, (jnp.moveaxis(states, 1, 0), jnp.moveaxis(chunk_decay, 1, 0)))
    h_prev = jnp.moveaxis(h_prev, 0, 1)
    y_off = jnp.einsum('bclgn,bcgrpn->bclgrp', cm, h_prev) * jnp.exp(acum)[..., None]
    return (y_diag + y_off).reshape(bsz, L, G, R, P), h_last


def mamba_mixer(z, xbc, dt, conv_w, conv_b, dt_bias, a_log, d_skip, norm_w, conv_state, ssm_state):
    bsz, t = z.shape[:2]
    xbc_ext = jnp.concatenate([conv_state.astype(xbc.dtype), xbc], axis=1)
    conv = conv_b
    for w in range(CONV_W):
        conv = conv + xbc_ext[:, w:w + t] * conv_w[w]
    new_conv = xbc_ext[:, t:]
    xbc = jax.nn.silu(conv).astype(F32)
    xs, bm, cm = jnp.split(xbc, [D_INNER, D_INNER + SSM_GROUPS * D_STATE], axis=-1)
    xs = xs.reshape(bsz, t, SSM_GROUPS, HEADS_PER_GROUP, SSM_HEAD_DIM)
    bm = bm.reshape(bsz, t, SSM_GROUPS, D_STATE)
    cm = cm.reshape(bsz, t, SSM_GROUPS, D_STATE)
    dt = jax.nn.softplus(dt.astype(F32) + dt_bias.astype(F32)).reshape(bsz, t, SSM_GROUPS, HEADS_PER_GROUP)
    a = -jnp.exp(a_log.astype(F32)).reshape(SSM_GROUPS, HEADS_PER_GROUP)
    chunk = SSM_CHUNK if t % SSM_CHUNK == 0 else t
    h0 = ssm_state.astype(F32).reshape(bsz, SSM_GROUPS, HEADS_PER_GROUP, SSM_HEAD_DIM, D_STATE)
    y, h_new = ssd_chunked(xs * dt[..., None], dt * a, bm, cm, h0, chunk)
    y = (y + xs * d_skip.astype(F32).reshape(SSM_GROUPS, HEADS_PER_GROUP)[:, :, None]).reshape(bsz, t, D_INNER)
    yg = y * jax.nn.silu(z.astype(F32))
    yg = yg * lax.rsqrt(jnp.mean(jnp.square(yg), -1, keepdims=True) + LN_EPS) * norm_w.astype(F32)
    return yg, new_conv, h_new.reshape(bsz, N_SSM_HEADS, SSM_HEAD_DIM, D_STATE).astype(ssm_state.dtype)


def dsa_project(x, w_main, w_tail, pos, tm):
    bsz, t = x.shape[:2]
    x2 = x.reshape(bsz * t, D_MODEL)
    main = proj(x2, w_main, tm=tm, tn=512).reshape(bsz, t, -1)
    tail = proj(x2, w_tail, tm=tm, tn=128).reshape(bsz, t, -1)
    main, tail = rope_apply(main, tail, pos, tr=min(t, 512))
    q, k, v, qi = jnp.split(main, [1024, 1280, 1536], axis=-1)
    ki, wi = tail[..., :IDX_DIM], tail[..., IDX_DIM:IDX_DIM + N_IDX_HEADS]
    return (q, k.reshape(bsz, t, N_KV_HEADS, ATT_HEAD_DIM), v.reshape(bsz, t, N_KV_HEADS, ATT_HEAD_DIM), qi, ki, wi)


ROPE_V_BLOCKS = (10, 11)


def _rope_kernel(main_ref, tail_ref, cos_ref, sin_ref, mo_ref, to_ref):
    cos, sin = cos_ref[...], sin_ref[...]
    lane = lax.broadcasted_iota(jnp.int32, cos.shape, 1)
    first_half = lane % ATT_HEAD_DIM < ATT_HEAD_DIM // 2

    def rot(xb):
        partner = jnp.where(first_half, -pltpu.roll(xb, 128 - 32, axis=1), pltpu.roll(xb, 32, axis=1))
        return xb * cos + partner * sin

    for cb in range(main_ref.shape[2] // 128):
        xb = main_ref[0, :, cb * 128:(cb + 1) * 128]
        mo_ref[0, :, cb * 128:(cb + 1) * 128] = xb if cb in ROPE_V_BLOCKS else rot(xb)
    tb = tail_ref[0]
    to_ref[0] = jnp.where(lane < IDX_DIM, rot(tb), tb)


def rope_apply(main, tail, pos, *, tr):
    bsz, t, wm = main.shape
    inv_freq = ROPE_THETA ** (-jnp.arange(0, ATT_HEAD_DIM, 2, dtype=F32) / ATT_HEAD_DIM)
    ang = pos.astype(F32)[:, None] * inv_freq[None, :]
    cos = jnp.tile(jnp.cos(ang), (1, 4))
    sin = jnp.tile(jnp.sin(ang), (1, 4))
    return pl.pallas_call(
        _rope_kernel,
        grid=(bsz, t // tr),
        in_specs=[
            pl.BlockSpec((1, tr, wm), lambda b, i: (b, i, 0)),
            pl.BlockSpec((1, tr, 128), lambda b, i: (b, i, 0)),
            pl.BlockSpec((tr, 128), lambda b, i: (i, 0)),
            pl.BlockSpec((tr, 128), lambda b, i: (i, 0)),
        ],
        out_specs=[pl.BlockSpec((1, tr, wm), lambda b, i: (b, i, 0)), pl.BlockSpec((1, tr, 128), lambda b, i: (b, i, 0))],
        out_shape=[jax.ShapeDtypeStruct(main.shape, F32), jax.ShapeDtypeStruct(tail.shape, F32)],
        compiler_params=pltpu.CompilerParams(
            dimension_semantics=("parallel", "parallel"), vmem_limit_bytes=VMEM_LIMIT_BYTES),
        name="rope",
    )(main, tail, cos, sin)


def dsa_select_attend(q, qi, wi, ki_all, q_pos, gather_kv, topk):
    bsz, t = q.shape[:2]
    L = ki_all.shape[1]
    s = jnp.einsum('bthd,bsd->bths', qi.astype(F32), ki_all.astype(F32)) * IDX_DIM ** -0.5
    score = jnp.einsum('bth,bths->bts', wi.astype(F32) * N_IDX_HEADS ** -0.5, jax.nn.relu(s))
    admissible = jnp.arange(L, dtype=jnp.int32)[None, :] <= q_pos[:, None]
    score = jnp.where(admissible[None], score, -jnp.inf)
    _, idx = lax.top_k(score, topk)
    valid = idx <= q_pos[None, :, None]
    k_sel, v_sel = gather_kv(idx)
    qg = q.reshape(bsz, t, N_KV_HEADS, KV_GROUP, ATT_HEAD_DIM)
    logits = jnp.einsum('btjgd,btsjd->btjgs', qg.astype(F32), k_sel.astype(F32)) * ATT_HEAD_DIM ** -0.5
    logits = jnp.where(valid[:, :, None, None, :], logits, -jnp.inf)
    p = jax.nn.softmax(logits, axis=-1)
    o = jnp.einsum('btjgs,btsjd->btjgd', p, v_sel.astype(F32))
    return o.reshape(bsz, t, N_ATT_HEADS * ATT_HEAD_DIM).astype(q.dtype)


def take_rows(a, i):
    return jax.vmap(lambda ab, ib: ab[ib])(a, i)


def run_trunk(x, mem_k, mem_v, conv0, ssm0, attend, pos, p, *, tm, tm_mem):
    conv_out, ssm_out, k_out, v_out, ki_out = [], [], [], [], []
    bsz, t = x.shape[:2]
    m = bsz * t
    x = x.reshape(m, D_MODEL)
    for i in range(DEPTH):
        j = i // N_MIXERS
        g, b = p['ln_g'][i], p['ln_b'][i]
        if i % N_MIXERS == 0:
            zx = proj(x, p['w_ssm_main'][j], tm=tm, tn=512).reshape(bsz, t, -1)
            dt_raw = proj(x, p['w_ssm_dt'][j], tm=tm, tn=128).reshape(bsz, t, -1)
            act, cs = ssm_conv(zx, conv0[j], p['ssm_conv_w'][j], p['ssm_conv_b'][j], tc=min(t, 512))
            h, ss = ssd_scan(act, zx, dt_raw, p['ssm_dt_bias'][j], p['ssm_a_log'][j], p['ssm_d'][j],
                             p['ssm_norm_w'][j], ssm0[j], c=SSM_CHUNK if t % SSM_CHUNK == 0 else t)
            conv_out.append(cs)
            ssm_out.append(ss)
            w_o = p['w_ssm_out'][j]
        else:
            q, k, v, qi, ki, wi = dsa_project(x.reshape(bsz, t, D_MODEL), p['w_att_main'][j], p['w_att_tail'][j],
                                              pos, tm)
            h = attend(j, q.reshape(bsz, t, -1), k.reshape(bsz, t, -1), v.reshape(bsz, t, -1),
                       qi.reshape(bsz, t, -1), ki, wi)
            k_out.append(k)
            v_out.append(v)
            ki_out.append(ki)
            w_o = p['w_att_out'][j]
        x = out_ln(h.reshape(m, -1), w_o, x, g[0], b[0], tm=tm_mem)
        q_mem = proj(x, p['w_mem_q'][i], tm=tm, tn=512).reshape(bsz, t, D_MODEL)
        o_mem = mem_attn(q_mem, mem_k, mem_v, i, tm=min(t, tm_mem))
        x = out_ln(o_mem.reshape(m, D_MODEL), p['w_mem_out'][i], x, g[1], b[1], tm=tm_mem)
        x = ffn_ln(x, p['w_ffn_in'][i], p['w_ffn_out'][i], g[2], b[2], tm=tm)
    return (x.reshape(bsz, t, D_MODEL), jnp.stack(k_out), jnp.stack(v_out), jnp.stack(ki_out),
            jnp.stack(conv_out), jnp.stack(ssm_out))


def kernel(x_prompt, x_sample, cache_k, cache_v, cache_kidx, cache_mem_k, cache_mem_v, state_conv, state_ssm,
           page_table, mem_prompt, w_ssm_in, ssm_conv_w, ssm_conv_b, ssm_dt_bias, ssm_a_log, ssm_d, ssm_norm_w,
           w_ssm_out, w_att_in, w_att_out, w_mem_q, w_mem_kv, w_mem_out, w_ffn_in, w_ffn_out, ln_g, ln_b):
    zx_cols = D_INNER + CONV_DIM
    att_cols = sum(ATT_SPLITS[:4])
    pad128 = lambda w: jnp.pad(w, ((0, 0), (0, 0), (0, 128 - w.shape[-1])))
    params = {'w_ssm_main': w_ssm_in[..., :zx_cols].astype(BF16), 'w_ssm_dt': pad128(w_ssm_in[..., zx_cols:]).astype(BF16),
              'ssm_conv_w': ssm_conv_w, 'ssm_conv_b': ssm_conv_b, 'ssm_dt_bias': ssm_dt_bias,
              'ssm_a_log': ssm_a_log, 'ssm_d': ssm_d, 'ssm_norm_w': ssm_norm_w, 'w_ssm_out': w_ssm_out.astype(BF16),
              'w_att_main': w_att_in[..., :att_cols].astype(BF16), 'w_att_tail': pad128(w_att_in[..., att_cols:]).astype(BF16),
              'w_att_out': w_att_out.astype(BF16),
              'w_mem_q': w_mem_q.astype(BF16), 'w_mem_out': w_mem_out.astype(BF16),
              'w_ffn_in': w_ffn_in.astype(BF16), 'w_ffn_out': w_ffn_out.astype(BF16),
              'ln_g': ln_g, 'ln_b': ln_b}
    bp, seq = x_prompt.shape[:2]
    pos_p = jnp.arange(seq, dtype=jnp.int32)
    w_mem_kv_bf = w_mem_kv.astype(BF16)
    mem_rows = mem_prompt.reshape(bp * N_MEM, D_MODEL)
    mem_kv = jnp.stack([proj(mem_rows, w_mem_kv_bf[l], tm=bp * N_MEM, tn=512) for l in range(DEPTH)])
    memk_prompt = mem_kv[..., :D_MODEL].reshape(DEPTH, bp, N_MEM, N_MEM_HEADS, MEM_HEAD_DIM)
    memv_prompt = mem_kv[..., D_MODEL:].reshape(DEPTH, bp, N_MEM, N_MEM_HEADS, MEM_HEAD_DIM)
    n_ssm = (DEPTH + 1) // 2
    conv0 = jnp.zeros((n_ssm, bp, CONV_W - 1, CONV_DIM), x_prompt.dtype)
    ssm0 = jnp.zeros((n_ssm, bp, N_SSM_HEADS, SSM_HEAD_DIM, D_STATE), x_prompt.dtype)
    attend_p = lambda j, q, k, v, qi, ki, wi: dsa_prompt_attend_t(q, k, v, qi, ki, wi)
    y_prompt, k_prompt, v_prompt, kidx_prompt, conv_prompt, ssm_prompt = run_trunk(
        x_prompt, memk_prompt, memv_prompt, conv0, ssm0, attend_p, pos_p, params, tm=min(1024, bp * seq), tm_mem=min(512, seq))
    past = page_table.shape[1] * PAGE_SIZE
    bs, ts = x_sample.shape[:2]
    pos_s = past + jnp.arange(ts, dtype=jnp.int32)
    attend_s = lambda j, q, k, v, qi, ki, wi: dsa_sample_attend(q, k, v, qi, ki, wi, cache_k, cache_v, cache_kidx, j,
                                                                page_table)
    y_sample, k_sample, v_sample, kidx_sample, conv_sample, ssm_sample = run_trunk(
        x_sample, cache_mem_k, cache_mem_v, state_conv, state_ssm, attend_s, pos_s, params, tm=bs * ts, tm_mem=bs * ts)
    return (y_prompt, y_sample, k_prompt, v_prompt, kidx_prompt, conv_prompt, ssm_prompt, memk_prompt, memv_prompt,
            k_sample, v_sample, kidx_sample, conv_sample, ssm_sample)
```

```python
import functools

import jax
import jax.numpy as jnp
from jax import lax
from jax.experimental import pallas as pl
from jax.experimental.pallas import tpu as pltpu

F32 = jnp.float32
BF16 = jnp.bfloat16

DEPTH = 4
N_MIXERS = 2
D_MODEL = 1024
PAGE_SIZE = 128

D_INNER = 2048
SSM_HEAD_DIM = 64
N_SSM_HEADS = 32
SSM_GROUPS = 4
HEADS_PER_GROUP = 8
D_STATE = 128
CONV_W = 4
CONV_DIM = D_INNER + 2 * SSM_GROUPS * D_STATE
SSM_CHUNK = 128

ATT_HEAD_DIM = 64
N_ATT_HEADS = 16
N_KV_HEADS = 4
KV_GROUP = 4
N_IDX_HEADS = 8
IDX_DIM = 64
TOPK_MAX = 256
ATT_SPLITS = [1024, 256, 256, 512, 64, 8]
ROPE_THETA = 10000.0

N_MEM = 256
N_MEM_HEADS = 4
MEM_HEAD_DIM = 256
FFN_HIDDEN = 2816

DEEPNORM_ALPHA = (2 * DEPTH) ** 0.25
LN_EPS = 1e-5

VMEM_LIMIT_BYTES = 48 * 1024 * 1024


def _layer_norm_rows(y, g, b):
    mu = jnp.mean(y, axis=-1, keepdims=True)
    d = y - mu
    var = jnp.mean(d * d, axis=-1, keepdims=True)
    return d * lax.rsqrt(var + LN_EPS) * g + b


def _ffn_ln_kernel(x_ref, wa_ref, wb_ref, wo_ref, g_ref, b_ref, o_ref, xb_ref, acc_ref):
    k = pl.program_id(1)

    @pl.when(k == 0)
    def _():
        xb_ref[...] = x_ref[...].astype(BF16)
        acc_ref[...] = jnp.zeros_like(acc_ref)

    xb = xb_ref[...]
    a = jnp.dot(xb, wa_ref[...], preferred_element_type=F32)
    b = jnp.dot(xb, wb_ref[...], preferred_element_type=F32)
    h = (a * jax.nn.sigmoid(a)) * b
    acc_ref[...] += jnp.dot(h.astype(BF16), wo_ref[...], preferred_element_type=F32)

    @pl.when(k == pl.num_programs(1) - 1)
    def _():
        y = DEEPNORM_ALPHA * x_ref[...] + acc_ref[...]
        o_ref[...] = _layer_norm_rows(y, g_ref[...], b_ref[...])


def ffn_ln(x, w_in, w_out, g, b, *, tm, th=256):
    m, d = x.shape
    hidden = w_out.shape[0]
    nh = hidden // th
    return pl.pallas_call(
        _ffn_ln_kernel,
        grid=(m // tm, nh),
        in_specs=[
            pl.BlockSpec((tm, d), lambda i, k: (i, 0)),
            pl.BlockSpec((d, th), lambda i, k: (0, k)),
            pl.BlockSpec((d, th), lambda i, k: (0, k + nh)),
            pl.BlockSpec((th, d), lambda i, k: (k, 0)),
            pl.BlockSpec((1, d), lambda i, k: (0, 0)),
            pl.BlockSpec((1, d), lambda i, k: (0, 0)),
        ],
        out_specs=pl.BlockSpec((tm, d), lambda i, k: (i, 0)),
        out_shape=jax.ShapeDtypeStruct((m, d), F32),
        scratch_shapes=[pltpu.VMEM((tm, d), BF16), pltpu.VMEM((tm, d), F32)],
        compiler_params=pltpu.CompilerParams(
            dimension_semantics=("parallel", "arbitrary"), vmem_limit_bytes=VMEM_LIMIT_BYTES),
        name="ffn_ln",
    )(x, w_in, w_in, w_out, g.reshape(1, d), b.reshape(1, d))


def _proj_kernel(x_ref, w_ref, o_ref, xb_ref):
    @pl.when(pl.program_id(1) == 0)
    def _():
        xb_ref[...] = x_ref[...].astype(BF16)

    o_ref[...] = jnp.dot(xb_ref[...], w_ref[...], preferred_element_type=F32)


def proj(x, w, *, tm, tn):
    m, kd = x.shape
    n = w.shape[1]
    return pl.pallas_call(
        _proj_kernel,
        grid=(m // tm, n // tn),
        in_specs=[pl.BlockSpec((tm, kd), lambda i, j: (i, 0)), pl.BlockSpec((kd, tn), lambda i, j: (0, j))],
        out_specs=pl.BlockSpec((tm, tn), lambda i, j: (i, j)),
        out_shape=jax.ShapeDtypeStruct((m, n), F32),
        scratch_shapes=[pltpu.VMEM((tm, kd), BF16)],
        compiler_params=pltpu.CompilerParams(
            dimension_semantics=("parallel", "arbitrary"), vmem_limit_bytes=VMEM_LIMIT_BYTES),
        name="proj",
    )(x, w)


def _out_ln_kernel(h_ref, w_ref, x_ref, g_ref, b_ref, o_ref):
    y = DEEPNORM_ALPHA * x_ref[...] + jnp.dot(h_ref[...].astype(BF16), w_ref[...], preferred_element_type=F32)
    o_ref[...] = _layer_norm_rows(y, g_ref[...], b_ref[...])


def out_ln(h, w, x, g, b, *, tm):
    m, kd = h.shape
    d = w.shape[1]
    return pl.pallas_call(
        _out_ln_kernel,
        grid=(m // tm,),
        in_specs=[
            pl.BlockSpec((tm, kd), lambda i: (i, 0)),
            pl.BlockSpec((kd, d), lambda i: (0, 0)),
            pl.BlockSpec((tm, d), lambda i: (i, 0)),
            pl.BlockSpec((1, d), lambda i: (0, 0)),
            pl.BlockSpec((1, d), lambda i: (0, 0)),
        ],
        out_specs=pl.BlockSpec((tm, d), lambda i: (i, 0)),
        out_shape=jax.ShapeDtypeStruct((m, d), F32),
        compiler_params=pltpu.CompilerParams(
            dimension_semantics=("parallel",), vmem_limit_bytes=VMEM_LIMIT_BYTES),
        name="out_ln",
    )(h, w, x, g.reshape(1, d), b.reshape(1, d))


def _mem_attn_kernel(q_ref, mk_ref, mv_ref, o_ref):
    heads = [slice(h * MEM_HEAD_DIM, (h + 1) * MEM_HEAD_DIM) for h in range(N_MEM_HEADS)]
    s_all = [lax.dot_general(q_ref[0, :, sl].astype(BF16), mk_ref[:, h, :].astype(BF16), _NT,
                             preferred_element_type=F32) * MEM_HEAD_DIM ** -0.5 for h, sl in enumerate(heads)]
    e_all = [jnp.exp(s - jnp.max(s, axis=1, keepdims=True)) for s in s_all]
    for h, sl in enumerate(heads):
        pv = jnp.dot(e_all[h].astype(BF16), mv_ref[:, h, :].astype(BF16), preferred_element_type=F32)
        o_ref[0, :, sl] = pv / jnp.sum(e_all[h], axis=1, keepdims=True)


def mem_attn(q, mk, mv, layer, *, tm):
    bsz, t, d = q.shape
    mem_spec = pl.BlockSpec((None, None, N_MEM, N_MEM_HEADS, MEM_HEAD_DIM), lambda b, i: (layer, b, 0, 0, 0))
    return pl.pallas_call(
        _mem_attn_kernel,
        grid=(bsz, t // tm),
        in_specs=[pl.BlockSpec((1, tm, d), lambda b, i: (b, i, 0)), mem_spec, mem_spec],
        out_specs=pl.BlockSpec((1, tm, d), lambda b, i: (b, i, 0)),
        out_shape=jax.ShapeDtypeStruct((bsz, t, d), F32),
        compiler_params=pltpu.CompilerParams(
            dimension_semantics=("parallel", "arbitrary"), vmem_limit_bytes=VMEM_LIMIT_BYTES),
        name="mem_attn",
    )(q, mk, mv)


CONV_COLS = 1024


def _conv_kernel(x_ref, st_ref, w_ref, b_ref, act_ref, last_ref, prev_ref, *, tc):
    @pl.when(pl.program_id(2) == 0)
    def _():
        prev_ref[...] = st_ref[0]

    x = x_ref[0]
    prev = prev_ref[...]
    row = lax.broadcasted_iota(jnp.int32, (8, CONV_COLS), 0)
    acc = jnp.broadcast_to(b_ref[...], x.shape)
    for s in (3, 2, 1):
        rolled = pltpu.roll(x, s, axis=0)
        top = jnp.where(row < s, pltpu.roll(prev, s, axis=0), rolled[0:8])
        shifted = top if tc == 8 else jnp.concatenate([top, rolled[8:]], axis=0)
        acc = acc + shifted * w_ref[CONV_W - 1 - s:CONV_W - s, :]
    acc = acc + x * w_ref[CONV_W - 1:CONV_W, :]
    act_ref[0] = acc * jax.nn.sigmoid(acc)
    prev_ref[...] = x[tc - 8:tc]
    last_ref[0] = x[tc - 8:tc]


def ssm_conv(zx, conv_state, conv_w, conv_b, *, tc):
    bsz, t = zx.shape[:2]
    ncb = CONV_DIM // CONV_COLS
    col0 = D_INNER // CONV_COLS
    st8 = jnp.pad(conv_state, ((0, 0), (8 - (CONV_W - 1), 0), (0, 0)))
    act, last = pl.pallas_call(
        functools.partial(_conv_kernel, tc=tc),
        grid=(bsz, ncb, t // tc),
        in_specs=[
            pl.BlockSpec((1, tc, CONV_COLS), lambda b, c, i: (b, i, col0 + c)),
            pl.BlockSpec((1, 8, CONV_COLS), lambda b, c, i: (b, 0, c)),
            pl.BlockSpec((CONV_W, CONV_COLS), lambda b, c, i: (0, c)),
            pl.BlockSpec((1, CONV_COLS), lambda b, c, i: (0, c)),
        ],
        out_specs=[
            pl.BlockSpec((1, tc, CONV_COLS), lambda b, c, i: (b, i, c)),
            pl.BlockSpec((1, 8, CONV_COLS), lambda b, c, i: (b, 0, c)),
        ],
        out_shape=[jax.ShapeDtypeStruct((bsz, t, CONV_DIM), F32), jax.ShapeDtypeStruct((bsz, 8, CONV_DIM), F32)],
        scratch_shapes=[pltpu.VMEM((8, CONV_COLS), F32)],
        compiler_params=pltpu.CompilerParams(
            dimension_semantics=("parallel", "parallel", "arbitrary"), vmem_limit_bytes=VMEM_LIMIT_BYTES),
        name="ssm_conv",
    )(zx, st8, conv_w, conv_b.reshape(1, CONV_DIM))
    return act, last[:, 8 - (CONV_W - 1):, :]


def _split3(v):
    hi = v.astype(BF16)
    r1 = v - hi.astype(F32)
    mid = r1.astype(BF16)
    lo = (r1 - mid.astype(F32)).astype(BF16)
    return hi, mid, lo


def _dot01_right(v, ones_mat):
    return sum(jnp.dot(p, ones_mat, preferred_element_type=F32) for p in _split3(v))


def _dot01_left(ones_mat, v):
    return sum(jnp.dot(ones_mat, p, preferred_element_type=F32) for p in _split3(v))


def _softplus(x):
    return jnp.maximum(x, 0.0) + jnp.log1p(jnp.exp(-jnp.abs(x)))


def _ssd_kernel(xs_ref, b_ref, c_ref, z_ref, dt_ref, dtt_ref, dtb_ref, dtbt_ref, alog_ref, alogt_ref,
                d_ref, nw_ref, h0_ref, y_ref, hT_ref, st_ref, yacc_ref, *, c):
    ci = pl.program_id(1)

    @pl.when(ci == 0)
    def _():
        st_ref[...] = h0_ref[0]

    gw = HEADS_PER_GROUP * SSM_HEAD_DIM
    dt = _softplus(dt_ref[0][:, 0:N_SSM_HEADS] + dtb_ref[...])
    dtt = _softplus(dtt_ref[0] + dtbt_ref[...])
    a = dt * -jnp.exp(alog_ref[...])
    at = dtt * -jnp.exp(alogt_ref[...])
    ri = lax.broadcasted_iota(jnp.int32, (c, c), 0)
    cj = lax.broadcasted_iota(jnp.int32, (c, c), 1)
    causal = ri >= cj
    acum = _dot01_left(jnp.where(causal, 1.0, 0.0).astype(BF16), a)
    acum_t = _dot01_right(at, jnp.where(ri <= cj, 1.0, 0.0).astype(BF16))
    a_last = acum[c - 1:c, :]
    expand = jnp.where(lax.broadcasted_iota(jnp.int32, (N_SSM_HEADS, D_INNER), 1) // SSM_HEAD_DIM
                       == lax.broadcasted_iota(jnp.int32, (N_SSM_HEADS, D_INNER), 0), 1.0, 0.0).astype(BF16)
    e_dt = _dot01_right(dt, expand)
    e_in = _dot01_right(jnp.exp(acum), expand)
    e_out = _dot01_right(jnp.exp(a_last - acum), expand)
    e_chunk = e_in[c - 1:c, :]
    xs = xs_ref[0]
    xdt = xs * e_dt
    head_of_col = lax.broadcasted_iota(jnp.int32, (c, gw), 1) // SSM_HEAD_DIM
    for g in range(SSM_GROUPS):
        gs = slice(g * gw, (g + 1) * gw)
        bg = b_ref[0][:, g * D_STATE:(g + 1) * D_STATE].astype(BF16)
        cg = c_ref[0][:, g * D_STATE:(g + 1) * D_STATE].astype(BF16)
        cb = lax.dot_general(cg, bg, _NT, preferred_element_type=F32)
        xg = xdt[:, gs]
        yg = jnp.zeros((c, gw), F32)
        for r in range(HEADS_PER_GROUP):
            h = g * HEADS_PER_GROUP + r
            seg = acum[:, h:h + 1] - acum_t[h:h + 1, :]
            m = (cb * jnp.exp(jnp.where(causal, seg, -jnp.inf))).astype(BF16)
            xm = jnp.where(head_of_col == r, xg, 0.0).astype(BF16)
            yg = yg + jnp.dot(m, xm, preferred_element_type=F32)
        state = st_ref[g]
        y_off = jnp.dot(cg, state.astype(BF16), preferred_element_type=F32) * e_in[:, gs]
        yacc_ref[:, gs] = yg + y_off + xs[:, gs] * d_ref[:, gs]
        xd = (xg * e_out[:, gs]).astype(BF16)
        st_ref[g] = state * e_chunk[:, gs] + lax.dot_general(bg, xd, (((0,), (0,)), ((), ())),
                                                             preferred_element_type=F32)
    z = z_ref[0]
    yz = yacc_ref[...] * (z * jax.nn.sigmoid(z))
    y_ref[0] = yz * lax.rsqrt(jnp.mean(yz * yz, axis=-1, keepdims=True) + LN_EPS) * nw_ref[...]

    @pl.when(ci == pl.num_programs(1) - 1)
    def _():
        hT_ref[0] = st_ref[...]


def ssd_scan(act, zx, dt_raw, dt_bias, a_log, d_skip, norm_w, ssm_state, *, c):
    bsz, t = act.shape[:2]
    gw = HEADS_PER_GROUP * SSM_HEAD_DIM
    bc_w = SSM_GROUPS * D_STATE
    h0 = ssm_state.reshape(bsz, SSM_GROUPS, HEADS_PER_GROUP, SSM_HEAD_DIM, D_STATE)
    h0 = h0.transpose(0, 1, 4, 2, 3).reshape(bsz, SSM_GROUPS, D_STATE, gw)
    dtt = jnp.swapaxes(dt_raw[..., :N_SSM_HEADS], 1, 2)
    row = lambda v: v.reshape(1, -1)
    col = lambda v: v.reshape(-1, 1)
    full = lambda shape: pl.BlockSpec(shape, lambda b, i: (0,) * len(shape))
    y, h_t = pl.pallas_call(
        functools.partial(_ssd_kernel, c=c),
        grid=(bsz, t // c),
        in_specs=[
            pl.BlockSpec((1, c, D_INNER), lambda b, i: (b, i, 0)),
            pl.BlockSpec((1, c, bc_w), lambda b, i: (b, i, D_INNER // bc_w)),
            pl.BlockSpec((1, c, bc_w), lambda b, i: (b, i, D_INNER // bc_w + 1)),
            pl.BlockSpec((1, c, D_INNER), lambda b, i: (b, i, 0)),
            pl.BlockSpec((1, c, 128), lambda b, i: (b, i, 0)),
            pl.BlockSpec((1, N_SSM_HEADS, c), lambda b, i: (b, 0, i)),
            full((1, N_SSM_HEADS)), full((N_SSM_HEADS, 1)), full((1, N_SSM_HEADS)), full((N_SSM_HEADS, 1)),
            full((1, D_INNER)), full((1, D_INNER)),
            pl.BlockSpec((1, SSM_GROUPS, D_STATE, gw), lambda b, i: (b, 0, 0, 0)),
        ],
        out_specs=[
            pl.BlockSpec((1, c, D_INNER), lambda b, i: (b, i, 0)),
            pl.BlockSpec((1, SSM_GROUPS, D_STATE, gw), lambda b, i: (b, 0, 0, 0)),
        ],
        out_shape=[jax.ShapeDtypeStruct((bsz, t, D_INNER), F32),
                   jax.ShapeDtypeStruct((bsz, SSM_GROUPS, D_STATE, gw), F32)],
        scratch_shapes=[pltpu.VMEM((SSM_GROUPS, D_STATE, gw), F32), pltpu.VMEM((c, D_INNER), F32)],
        compiler_params=pltpu.CompilerParams(
            dimension_semantics=("parallel", "arbitrary"), vmem_limit_bytes=VMEM_LIMIT_BYTES),
        name="ssd_scan",
    )(act, act, act, zx, dt_raw, dtt, row(dt_bias), col(dt_bias), row(a_log), col(a_log),
      row(jnp.repeat(d_skip, SSM_HEAD_DIM)), row(norm_w), h0)
    h_t = h_t.reshape(bsz, SSM_GROUPS, D_STATE, HEADS_PER_GROUP, SSM_HEAD_DIM)
    return y, h_t.transpose(0, 1, 3, 4, 2).reshape(bsz, N_SSM_HEADS, SSM_HEAD_DIM, D_STATE)


INT_MIN = -2 ** 31
NEG_BIG = -1e30
_NT = (((1,), (1,)), ((), ()))


def _float_key(x):
    x = jnp.where(x == 0.0, 0.0, x)
    bits = lax.bitcast_convert_type(x, jnp.int32)
    return bits ^ ((bits >> 31) & 0x7FFFFFFF)


def _dsa_prompt_kernel(qt_ref, qit_ref, wit_ref, k_ref, vxt_ref, ki_ref, o_ref, key_ref, m_ref, acc_ref,
                         s_ref, p_ref, *, tq, tk, topk):
    i = pl.program_id(1)
    nkb = ((i + 1) * tq + tk - 1) // tk
    gq = KV_GROUP * tq
    rep8 = lambda v: jnp.broadcast_to(v, (8, tq))
    as3 = lambda x: x.reshape(tk // 8, 8, tq)
    w_heads = wit_ref[0, 0] * (N_IDX_HEADS ** -0.5)
    q_pos = i * tq + lax.broadcasted_iota(jnp.int32, (tk, tq), 1)

    def score_block(kb, c):
        off = pl.multiple_of(kb * tk, tk)
        s = jnp.dot(ki_ref[0, pl.ds(off, tk), :], qit_ref[0, 0], preferred_element_type=F32)
        acc = jnp.zeros((tk, tq), F32)
        for h in range(N_IDX_HEADS):
            acc = acc + w_heads[h:h + 1, :] * jnp.maximum(s[:, h * tq:(h + 1) * tq], 0.0)
        kpos = off + lax.broadcasted_iota(jnp.int32, (tk, tq), 0)
        key_ref[kb] = jnp.where(kpos <= q_pos, _float_key(acc), INT_MIN)
        return c

    lax.fori_loop(0, nkb, score_block, 0)

    def count(cand, strict):
        def body(kb, accs):
            blk = key_ref[kb]
            accs = list(accs)
            for r in range(tk // 8):
                part = blk[r * 8:(r + 1) * 8, :]
                hit = (part > cand) if strict else (part >= cand)
                accs[r % 4] = accs[r % 4] + jnp.where(hit, 1.0, 0.0)
            return tuple(accs)
        a0, a1, a2, a3 = lax.fori_loop(0, nkb, body, (jnp.zeros((8, tq), F32),) * 4)
        return jnp.sum((a0 + a1) + (a2 + a3), axis=0, keepdims=True)

    kf = float(topk)
    zero = jnp.zeros((8, tq), jnp.int32)
    thr = jnp.where(rep8(count(zero, False)) >= kf, zero, jnp.full((8, tq), INT_MIN, jnp.int32))

    def bit_step(it, thr):
        cand = thr | jnp.left_shift(jnp.int32(1), 30 - it)
        return jnp.where(rep8(count(cand, False)) >= kf, cand, thr)

    thr = lax.fori_loop(0, 31, bit_step, thr)
    need = jnp.where(thr == INT_MIN, 0.0, kf - rep8(count(thr, True)))

    m_ref[...] = jnp.full(m_ref.shape, NEG_BIG, F32)
    acc_ref[...] = jnp.zeros(acc_ref.shape, F32)
    lower = jnp.where(lax.broadcasted_iota(jnp.int32, (tk, tk), 1) < lax.broadcasted_iota(jnp.int32, (tk, tk), 0),
                      1.0, 0.0).astype(BF16)

    def attend_block(kb, eq_before):
        off = pl.multiple_of(kb * tk, tk)
        key3 = as3(key_ref[kb])
        eq3 = key3 == thr[None]
        eqf = jnp.where(eq3, 1.0, 0.0)
        rank3 = as3(jnp.dot(lower, eqf.reshape(tk, tq).astype(BF16), preferred_element_type=F32)) + eq_before[None]
        bias = jnp.where(key3 > thr[None], 0.0,
                         jnp.where(eq3, jnp.where(rank3 < need[None], 0.0, NEG_BIG), NEG_BIG)).reshape(tk, tq)
        bias4 = jnp.concatenate([bias] * KV_GROUP, axis=1)
        col_max = []
        for j in range(N_KV_HEADS):
            kblk = k_ref[0, pl.ds(off, tk), j * 64:(j + 1) * 64]
            s = jnp.dot(kblk, qt_ref[0, 0, j], preferred_element_type=F32) + bias4
            s_ref[j] = s
            col_max.append(jnp.max(s, axis=0, keepdims=True))
        alpha = []
        for j in range(N_KV_HEADS):
            m_old = m_ref[j]
            m_new = jnp.maximum(m_old, col_max[j])
            alpha.append(jnp.exp(m_old - m_new))
            p_ref[j] = jnp.exp(s_ref[j] - m_new).astype(BF16)
            m_ref[j] = m_new
        for j in range(N_KV_HEADS):
            pv = jnp.dot(vxt_ref[0, j, kb], p_ref[j], preferred_element_type=F32)
            acc_ref[j] = alpha[j] * acc_ref[j] + pv
        return eq_before + rep8(jnp.sum(jnp.sum(eqf, axis=0), axis=0, keepdims=True))

    lax.fori_loop(0, nkb, attend_block, jnp.zeros((8, tq), F32))

    for j in range(N_KV_HEADS):
        a = acc_ref[j]
        o_t = a[0:64, :] / a[64:65, :]
        for g in range(KV_GROUP):
            h = j * KV_GROUP + g
            o_ref[0, :, h * 64:(h + 1) * 64] = o_t[:, g * tq:(g + 1) * tq].T


def dsa_prompt_attend(q, k, v, qi, ki, wi, *, tq, tk):
    bsz, t = q.shape[:2]
    nq, nk = t // tq, t // tk
    topk = min(TOPK_MAX, t // 4)
    qt = (q * ATT_HEAD_DIM ** -0.5).astype(BF16).reshape(bsz, nq, tq, N_KV_HEADS, KV_GROUP, 64)
    qt = qt.transpose(0, 1, 3, 5, 4, 2).reshape(bsz, nq, N_KV_HEADS, 64, KV_GROUP * tq)
    qit = (qi * IDX_DIM ** -0.5).astype(BF16).reshape(bsz, nq, tq, N_IDX_HEADS, 64)
    qit = qit.transpose(0, 1, 4, 3, 2).reshape(bsz, nq, 64, N_IDX_HEADS * tq)
    wit = wi.reshape(bsz, nq, tq, N_IDX_HEADS).transpose(0, 1, 3, 2)
    vt = v.astype(BF16).reshape(bsz, nk, tk, N_KV_HEADS, 64).transpose(0, 3, 1, 4, 2)
    vxt = jnp.concatenate([vt, jnp.ones_like(vt)], axis=3)
    kern = functools.partial(_dsa_prompt_kernel, tq=tq, tk=tk, topk=topk)
    return pl.pallas_call(
        kern,
        grid=(bsz, nq),
        in_specs=[
            pl.BlockSpec((1, 1, N_KV_HEADS, 64, KV_GROUP * tq), lambda b, i: (b, i, 0, 0, 0)),
            pl.BlockSpec((1, 1, 64, N_IDX_HEADS * tq), lambda b, i: (b, i, 0, 0)),
            pl.BlockSpec((1, 1, N_IDX_HEADS, tq), lambda b, i: (b, i, 0, 0)),
            pl.BlockSpec((1, t, 256), lambda b, i: (b, 0, 0), pipeline_mode=pl.Buffered(1)),
            pl.BlockSpec((1, N_KV_HEADS, nk, 128, tk), lambda b, i: (b, 0, 0, 0, 0), pipeline_mode=pl.Buffered(1)),
            pl.BlockSpec((1, t, IDX_DIM), lambda b, i: (b, 0, 0), pipeline_mode=pl.Buffered(1)),
        ],
        out_specs=pl.BlockSpec((1, tq, 1024), lambda b, i: (b, i, 0)),
        out_shape=jax.ShapeDtypeStruct((bsz, t, 1024), F32),
        scratch_shapes=[
            pltpu.VMEM((nk, tk, tq), jnp.int32),
            pltpu.VMEM((N_KV_HEADS, 1, KV_GROUP * tq), F32),
            pltpu.VMEM((N_KV_HEADS, 128, KV_GROUP * tq), F32),
            pltpu.VMEM((N_KV_HEADS, tk, KV_GROUP * tq), F32),
            pltpu.VMEM((N_KV_HEADS, tk, KV_GROUP * tq), BF16),
        ],
        compiler_params=pltpu.CompilerParams(
            dimension_semantics=("parallel", "arbitrary"), vmem_limit_bytes=VMEM_LIMIT_BYTES),
        name="dsa_prompt_attend",
    )(qt, qit, wit, k.astype(BF16), vxt, ki.astype(BF16))


PAGES_PER_STEP = 8


def _sample_select_kernel(pt_ref, qi_ref, wi_ref, kinew_ref, *rest, t, n_steps, topk):
    ki_refs = rest[:PAGES_PER_STEP]
    bias_ref, qis_ref, wib_ref, key_ref = rest[PAGES_PER_STEP:]
    p = pl.program_id(1)
    w = PAGES_PER_STEP * PAGE_SIZE
    nl = w // 128

    @pl.when(p == 0)
    def _():
        for h in range(N_IDX_HEADS):
            qis_ref[h * t:(h + 1) * t, :] = qi_ref[0, :, h * 64:(h + 1) * 64]
            wib_ref[h] = jnp.broadcast_to(wi_ref[0, :, h:h + 1] * (N_IDX_HEADS ** -0.5), (t, 128))

    def scores(s):
        acc = jnp.zeros((t, s.shape[1]), F32)
        for h in range(N_IDX_HEADS):
            wfull = jnp.concatenate([wib_ref[h]] * (s.shape[1] // 128), axis=1)
            acc = acc + wfull * jnp.maximum(s[h * t:(h + 1) * t, :], 0.0)
        return acc

    ki_t = jnp.concatenate([r[...] for r in ki_refs], axis=1).astype(BF16)
    key_ref[p] = _float_key(scores(jnp.dot(qis_ref[...].astype(BF16), ki_t, preferred_element_type=F32)))

    @pl.when(p == n_steps - 1)
    def _():
        knew = _float_key(scores(lax.dot_general(qis_ref[...].astype(BF16), kinew_ref[0].astype(BF16), _NT,
                                                 preferred_element_type=F32)))
        n_idx = lax.broadcasted_iota(jnp.int32, (t, 128), 1)
        r_idx = lax.broadcasted_iota(jnp.int32, (t, 128), 0)
        knew = jnp.where(n_idx <= r_idx, knew, INT_MIN)
        key_ref[n_steps] = jnp.concatenate([knew, jnp.full((t, w - 128), INT_MIN, jnp.int32)], axis=1)

        def count(cand, strict):
            def body(blk, acc):
                kb = key_ref[blk]
                for c in range(nl):
                    part = kb[:, c * 128:(c + 1) * 128]
                    hit = (part > cand) if strict else (part >= cand)
                    acc = acc + jnp.where(hit, 1.0, 0.0)
                return acc
            acc = lax.fori_loop(0, n_steps + 1, body, jnp.zeros((t, 128), F32))
            return jnp.sum(acc, axis=1, keepdims=True)

        kf = float(topk)
        zero = jnp.zeros((t, 128), jnp.int32)
        thr = jnp.where(count(zero, False) >= kf, zero, jnp.full((t, 128), INT_MIN, jnp.int32))

        def bit_step(it, thr):
            cand = thr | jnp.left_shift(jnp.int32(1), 30 - it)
            return jnp.where(count(cand, False) >= kf, cand, thr)

        thr = lax.fori_loop(0, 31, bit_step, thr)
        need = jnp.where(thr == INT_MIN, 0.0, kf - count(thr, True))
        upper = jnp.where(lax.broadcasted_iota(jnp.int32, (128, 128), 0)
                          < lax.broadcasted_iota(jnp.int32, (128, 128), 1), 1.0, 0.0).astype(BF16)

        def emit(blk, eq_before):
            kb = key_ref[blk]
            parts = []
            for c in range(nl):
                part = kb[:, c * 128:(c + 1) * 128]
                eq = part == thr
                eqf = jnp.where(eq, 1.0, 0.0)
                rank = jnp.dot(eqf.astype(BF16), upper, preferred_element_type=F32) + eq_before
                parts.append(jnp.where(part > thr, 0.0,
                                       jnp.where(eq, jnp.where(rank < need, 0.0, NEG_BIG), NEG_BIG)))
                eq_before = eq_before + jnp.sum(eqf, axis=1, keepdims=True)
            bias_ref[0, blk] = jnp.concatenate(parts, axis=1)
            return eq_before

        lax.fori_loop(0, n_steps + 1, emit, jnp.zeros((t, 128), F32))


def _sample_attend_kernel(pt_ref, q_ref, bias_ref, biasnew_ref, knew_ref, vnew_ref, *rest, t, n_steps):
    k_refs = rest[:PAGES_PER_STEP]
    v_refs = rest[PAGES_PER_STEP:2 * PAGES_PER_STEP]
    o_ref, qs_ref, m_ref, l_ref, acc_ref = rest[2 * PAGES_PER_STEP:]
    p = pl.program_id(1)
    rows = KV_GROUP * t

    @pl.when(p == 0)
    def _():
        for h in range(N_ATT_HEADS):
            j, g = divmod(h, KV_GROUP)
            qs_ref[j, g * t:(g + 1) * t, :] = q_ref[0, :, h * 64:(h + 1) * 64]
        m_ref[...] = jnp.full(m_ref.shape, NEG_BIG, F32)
        l_ref[...] = jnp.zeros(l_ref.shape, F32)
        acc_ref[...] = jnp.zeros(acc_ref.shape, F32)

    def update(k_t, v_t, bias):
        bias4 = jnp.concatenate([bias] * KV_GROUP, axis=0)
        s_all = [jnp.dot(qs_ref[j].astype(BF16), k_t(j).astype(BF16), preferred_element_type=F32) + bias4
                 for j in range(N_KV_HEADS)]
        alphas, probs = [], []
        for j in range(N_KV_HEADS):
            m_old = m_ref[j]
            m_new = jnp.maximum(m_old, jnp.max(s_all[j], axis=1, keepdims=True))
            alpha = jnp.exp(m_old - m_new)
            pr = jnp.exp(s_all[j] - m_new[:, 0:1])
            l_ref[j] = alpha * l_ref[j] + jnp.sum(pr, axis=1, keepdims=True)
            m_ref[j] = m_new
            alphas.append(alpha)
            probs.append(pr.astype(BF16))
        for j in range(N_KV_HEADS):
            pv = lax.dot_general(probs[j], v_t(j).astype(BF16), _NT, preferred_element_type=F32)
            acc_ref[j] = alphas[j][:, 0:64] * acc_ref[j] + pv

    update(lambda j: jnp.concatenate([r[j] for r in k_refs], axis=1),
           lambda j: jnp.concatenate([r[j] for r in v_refs], axis=1), bias_ref[0, 0])

    @pl.when(p == n_steps - 1)
    def _():
        update(lambda j: knew_ref[0, j * 64:(j + 1) * 64, :], lambda j: vnew_ref[0, j * 64:(j + 1) * 64, :],
               biasnew_ref[0, 0, :, 0:128])
        for h in range(N_ATT_HEADS):
            j, g = divmod(h, KV_GROUP)
            sl = slice(g * t, (g + 1) * t)
            o_ref[0, :, h * 64:(h + 1) * 64] = acc_ref[j, sl, :] / l_ref[j, sl, 0:64]


def dsa_sample_attend(q, k, v, qi, ki, wi, cache_k, cache_v, cache_kidx, layer, page_table):
    bsz, t = q.shape[:2]
    n_pages = page_table.shape[1]
    past = n_pages * PAGE_SIZE
    n_steps = n_pages // PAGES_PER_STEP
    w = PAGES_PER_STEP * PAGE_SIZE
    topk = min(TOPK_MAX, (past + t) // 4)
    pad = lambda a: jnp.pad(a, ((0, 0), (0, PAGE_SIZE - t), (0, 0)))
    pool_k = cache_k.transpose(0, 1, 3, 4, 2)
    pool_v = cache_v.transpose(0, 1, 3, 4, 2)
    pool_ki = cache_kidx.transpose(0, 1, 3, 2)
    knew_t = jnp.swapaxes(pad(k), 1, 2)
    vnew_t = jnp.swapaxes(pad(v), 1, 2)

    def page_spec(r, *major):
        zeros = (0,) * (1 + len(major))
        return pl.BlockSpec((None, None) + major + (PAGE_SIZE,),
                            lambda b, p, pt: (layer, pt[b, p * PAGES_PER_STEP + r]) + zeros)

    bias = pl.pallas_call(
        functools.partial(_sample_select_kernel, t=t, n_steps=n_steps, topk=topk),
        grid_spec=pltpu.PrefetchScalarGridSpec(
            num_scalar_prefetch=1,
            grid=(bsz, n_steps),
            in_specs=[
                pl.BlockSpec((1, t, 512), lambda b, p, pt: (b, 0, 0)),
                pl.BlockSpec((1, t, N_IDX_HEADS), lambda b, p, pt: (b, 0, 0)),
                pl.BlockSpec((1, PAGE_SIZE, IDX_DIM), lambda b, p, pt: (b, 0, 0)),
            ] + [page_spec(r, IDX_DIM) for r in range(PAGES_PER_STEP)],
            out_specs=pl.BlockSpec((1, n_steps + 1, t, w), lambda b, p, pt: (b, 0, 0, 0)),
            scratch_shapes=[
                pltpu.VMEM((N_IDX_HEADS * t, 64), F32),
                pltpu.VMEM((N_IDX_HEADS, t, 128), F32),
                pltpu.VMEM((n_steps + 1, t, w), jnp.int32),
            ]),
        out_shape=jax.ShapeDtypeStruct((bsz, n_steps + 1, t, w), F32),
        compiler_params=pltpu.CompilerParams(
            dimension_semantics=("parallel", "arbitrary"), vmem_limit_bytes=VMEM_LIMIT_BYTES),
        name="dsa_sample_select",
    )(page_table, qi * IDX_DIM ** -0.5, wi, pad(ki), *([pool_ki] * PAGES_PER_STEP))

    return pl.pallas_call(
        functools.partial(_sample_attend_kernel, t=t, n_steps=n_steps),
        grid_spec=pltpu.PrefetchScalarGridSpec(
            num_scalar_prefetch=1,
            grid=(bsz, n_steps),
            in_specs=[
                pl.BlockSpec((1, t, 1024), lambda b, p, pt: (b, 0, 0)),
                pl.BlockSpec((1, 1, t, w), lambda b, p, pt: (b, p, 0, 0)),
                pl.BlockSpec((1, 1, t, w), lambda b, p, pt: (b, n_steps, 0, 0)),
                pl.BlockSpec((1, 256, PAGE_SIZE), lambda b, p, pt: (b, 0, 0)),
                pl.BlockSpec((1, 256, PAGE_SIZE), lambda b, p, pt: (b, 0, 0)),
            ] + [page_spec(r, N_KV_HEADS, ATT_HEAD_DIM) for r in range(PAGES_PER_STEP)] * 2,
            out_specs=pl.BlockSpec((1, t, 1024), lambda b, p, pt: (b, 0, 0)),
            scratch_shapes=[
                pltpu.VMEM((N_KV_HEADS, KV_GROUP * t, 64), F32),
                pltpu.VMEM((N_KV_HEADS, KV_GROUP * t, 128), F32),
                pltpu.VMEM((N_KV_HEADS, KV_GROUP * t, 128), F32),
                pltpu.VMEM((N_KV_HEADS, KV_GROUP * t, 64), F32),
            ]),
        out_shape=jax.ShapeDtypeStruct((bsz, t, 1024), F32),
        compiler_params=pltpu.CompilerParams(
            dimension_semantics=("parallel", "arbitrary"), vmem_limit_bytes=VMEM_LIMIT_BYTES),
        name="dsa_sample_attend",
    )(page_table, q * ATT_HEAD_DIM ** -0.5, bias, bias, knew_t, vnew_t,
      *([pool_k] * PAGES_PER_STEP), *([pool_v] * PAGES_PER_STEP))


def dsa_project(x, w_main, w_tail, pos, tm):
    bsz, t = x.shape[:2]
    x2 = x.reshape(bsz * t, D_MODEL)
    main = proj(x2, w_main, tm=tm, tn=512).reshape(bsz, t, -1)
    tail = proj(x2, w_tail, tm=tm, tn=128).reshape(bsz, t, -1)
    main, tail = rope_apply(main, tail, pos, tr=min(t, 512))
    q, k, v, qi = jnp.split(main, [1024, 1280, 1536], axis=-1)
    ki, wi = tail[..., :IDX_DIM], tail[..., IDX_DIM:IDX_DIM + N_IDX_HEADS]
    return (q, k.reshape(bsz, t, N_KV_HEADS, ATT_HEAD_DIM), v.reshape(bsz, t, N_KV_HEADS, ATT_HEAD_DIM), qi, ki, wi)


ROPE_V_BLOCKS = (10, 11)


def _rope_kernel(main_ref, tail_ref, cos_ref, sin_ref, mo_ref, to_ref):
    cos, sin = cos_ref[...], sin_ref[...]
    lane = lax.broadcasted_iota(jnp.int32, cos.shape, 1)
    first_half = lane % ATT_HEAD_DIM < ATT_HEAD_DIM // 2

    def rot(xb):
        partner = jnp.where(first_half, -pltpu.roll(xb, 128 - 32, axis=1), pltpu.roll(xb, 32, axis=1))
        return xb * cos + partner * sin

    for cb in range(main_ref.shape[2] // 128):
        xb = main_ref[0, :, cb * 128:(cb + 1) * 128]
        mo_ref[0, :, cb * 128:(cb + 1) * 128] = xb if cb in ROPE_V_BLOCKS else rot(xb)
    tb = tail_ref[0]
    to_ref[0] = jnp.where(lane < IDX_DIM, rot(tb), tb)


def rope_apply(main, tail, pos, *, tr):
    bsz, t, wm = main.shape
    inv_freq = ROPE_THETA ** (-jnp.arange(0, ATT_HEAD_DIM, 2, dtype=F32) / ATT_HEAD_DIM)
    ang = pos.astype(F32)[:, None] * inv_freq[None, :]
    cos = jnp.tile(jnp.cos(ang), (1, 4))
    sin = jnp.tile(jnp.sin(ang), (1, 4))
    return pl.pallas_call(
        _rope_kernel,
        grid=(bsz, t // tr),
        in_specs=[
            pl.BlockSpec((1, tr, wm), lambda b, i: (b, i, 0)),
            pl.BlockSpec((1, tr, 128), lambda b, i: (b, i, 0)),
            pl.BlockSpec((tr, 128), lambda b, i: (i, 0)),
            pl.BlockSpec((tr, 128), lambda b, i: (i, 0)),
        ],
        out_specs=[pl.BlockSpec((1, tr, wm), lambda b, i: (b, i, 0)), pl.BlockSpec((1, tr, 128), lambda b, i: (b, i, 0))],
        out_shape=[jax.ShapeDtypeStruct(main.shape, F32), jax.ShapeDtypeStruct(tail.shape, F32)],
        compiler_params=pltpu.CompilerParams(
            dimension_semantics=("parallel", "parallel"), vmem_limit_bytes=VMEM_LIMIT_BYTES),
        name="rope",
    )(main, tail, cos, sin)


def run_trunk(x, mem_k, mem_v, conv0, ssm0, attend, pos, p, *, tm, tm_mem):
    conv_out, ssm_out, k_out, v_out, ki_out = [], [], [], [], []
    bsz, t = x.shape[:2]
    m = bsz * t
    x = x.reshape(m, D_MODEL)
    for i in range(DEPTH):
        j = i // N_MIXERS
        g, b = p['ln_g'][i], p['ln_b'][i]
        if i % N_MIXERS == 0:
            zx = proj(x, p['w_ssm_main'][j], tm=tm, tn=512).reshape(bsz, t, -1)
            dt_raw = proj(x, p['w_ssm_dt'][j], tm=tm, tn=128).reshape(bsz, t, -1)
            act, cs = ssm_conv(zx, conv0[j], p['ssm_conv_w'][j], p['ssm_conv_b'][j], tc=min(t, 512))
            h, ss = ssd_scan(act, zx, dt_raw, p['ssm_dt_bias'][j], p['ssm_a_log'][j], p['ssm_d'][j],
                             p['ssm_norm_w'][j], ssm0[j], c=SSM_CHUNK if t % SSM_CHUNK == 0 else t)
            conv_out.append(cs)
            ssm_out.append(ss)
            w_o = p['w_ssm_out'][j]
        else:
            q, k, v, qi, ki, wi = dsa_project(x.reshape(bsz, t, D_MODEL), p['w_att_main'][j], p['w_att_tail'][j],
                                              pos, tm)
            h = attend(j, q.reshape(bsz, t, -1), k.reshape(bsz, t, -1), v.reshape(bsz, t, -1),
                       qi.reshape(bsz, t, -1), ki, wi)
            k_out.append(k)
            v_out.append(v)
            ki_out.append(ki)
            w_o = p['w_att_out'][j]
        x = out_ln(h.reshape(m, -1), w_o, x, g[0], b[0], tm=tm_mem)
        q_mem = proj(x, p['w_mem_q'][i], tm=tm, tn=512).reshape(bsz, t, D_MODEL)
        o_mem = mem_attn(q_mem, mem_k, mem_v, i, tm=min(t, tm_mem))
        x = out_ln(o_mem.reshape(m, D_MODEL), p['w_mem_out'][i], x, g[1], b[1], tm=tm_mem)
        x = ffn_ln(x, p['w_ffn_in'][i], p['w_ffn_out'][i], g[2], b[2], tm=tm)
    return (x.reshape(bsz, t, D_MODEL), jnp.stack(k_out), jnp.stack(v_out), jnp.stack(ki_out),
            jnp.stack(conv_out), jnp.stack(ssm_out))


def kernel(x_prompt, x_sample, cache_k, cache_v, cache_kidx, cache_mem_k, cache_mem_v, state_conv, state_ssm,
           page_table, mem_prompt, w_ssm_in, ssm_conv_w, ssm_conv_b, ssm_dt_bias, ssm_a_log, ssm_d, ssm_norm_w,
           w_ssm_out, w_att_in, w_att_out, w_mem_q, w_mem_kv, w_mem_out, w_ffn_in, w_ffn_out, ln_g, ln_b):
    zx_cols = D_INNER + CONV_DIM
    att_cols = sum(ATT_SPLITS[:4])
    pad128 = lambda w: jnp.pad(w, ((0, 0), (0, 0), (0, 128 - w.shape[-1])))
    params = {'w_ssm_main': w_ssm_in[..., :zx_cols].astype(BF16), 'w_ssm_dt': pad128(w_ssm_in[..., zx_cols:]).astype(BF16),
              'ssm_conv_w': ssm_conv_w, 'ssm_conv_b': ssm_conv_b, 'ssm_dt_bias': ssm_dt_bias,
              'ssm_a_log': ssm_a_log, 'ssm_d': ssm_d, 'ssm_norm_w': ssm_norm_w, 'w_ssm_out': w_ssm_out.astype(BF16),
              'w_att_main': w_att_in[..., :att_cols].astype(BF16), 'w_att_tail': pad128(w_att_in[..., att_cols:]).astype(BF16),
              'w_att_out': w_att_out.astype(BF16),
              'w_mem_q': w_mem_q.astype(BF16), 'w_mem_out': w_mem_out.astype(BF16),
              'w_ffn_in': w_ffn_in.astype(BF16), 'w_ffn_out': w_ffn_out.astype(BF16),
              'ln_g': ln_g, 'ln_b': ln_b}
    bp, seq = x_prompt.shape[:2]
    pos_p = jnp.arange(seq, dtype=jnp.int32)
    w_mem_kv_bf = w_mem_kv.astype(BF16)
    mem_rows = mem_prompt.reshape(bp * N_MEM, D_MODEL)
    mem_kv = jnp.stack([proj(mem_rows, w_mem_kv_bf[l], tm=bp * N_MEM, tn=512) for l in range(DEPTH)])
    memk_prompt = mem_kv[..., :D_MODEL].reshape(DEPTH, bp, N_MEM, N_MEM_HEADS, MEM_HEAD_DIM)
    memv_prompt = mem_kv[..., D_MODEL:].reshape(DEPTH, bp, N_MEM, N_MEM_HEADS, MEM_HEAD_DIM)
    n_ssm = (DEPTH + 1) // 2
    conv0 = jnp.zeros((n_ssm, bp, CONV_W - 1, CONV_DIM), x_prompt.dtype)
    ssm0 = jnp.zeros((n_ssm, bp, N_SSM_HEADS, SSM_HEAD_DIM, D_STATE), x_prompt.dtype)
    attend_p = lambda j, q, k, v, qi, ki, wi: dsa_prompt_attend(q, k, v, qi, ki, wi, tq=min(256, seq),
                                                                tk=min(512, seq))
    y_prompt, k_prompt, v_prompt, kidx_prompt, conv_prompt, ssm_prompt = run_trunk(
        x_prompt, memk_prompt, memv_prompt, conv0, ssm0, attend_p, pos_p, params, tm=min(1024, bp * seq), tm_mem=min(512, seq))
    past = page_table.shape[1] * PAGE_SIZE
    bs, ts = x_sample.shape[:2]
    pos_s = past + jnp.arange(ts, dtype=jnp.int32)
    attend_s = lambda j, q, k, v, qi, ki, wi: dsa_sample_attend(q, k, v, qi, ki, wi, cache_k, cache_v, cache_kidx, j,
                                                                page_table)
    y_sample, k_sample, v_sample, kidx_sample, conv_sample, ssm_sample = run_trunk(
        x_sample, cache_mem_k, cache_mem_v, state_conv, state_ssm, attend_s, pos_s, params, tm=bs * ts, tm_mem=bs * ts)
    return (y_prompt, y_sample, k_prompt, v_prompt, kidx_prompt, conv_prompt, ssm_prompt, memk_prompt, memv_prompt,
            k_sample, v_sample, kidx_sample, conv_sample, ssm_sample)
```

```python
import functools

import jax
import jax.numpy as jnp
from jax import lax
from jax.experimental import pallas as pl
from jax.experimental.pallas import tpu as pltpu

F32 = jnp.float32
BF16 = jnp.bfloat16

DEPTH = 4
N_MIXERS = 2
D_MODEL = 1024
PAGE_SIZE = 128

D_INNER = 2048
SSM_HEAD_DIM = 64
N_SSM_HEADS = 32
SSM_GROUPS = 4
HEADS_PER_GROUP = 8
D_STATE = 128
CONV_W = 4
CONV_DIM = D_INNER + 2 * SSM_GROUPS * D_STATE
SSM_CHUNK = 128

ATT_HEAD_DIM = 64
N_ATT_HEADS = 16
N_KV_HEADS = 4
KV_GROUP = 4
N_IDX_HEADS = 8
IDX_DIM = 64
TOPK_MAX = 256
ATT_SPLITS = [1024, 256, 256, 512, 64, 8]
ROPE_THETA = 10000.0

N_MEM = 256
N_MEM_HEADS = 4
MEM_HEAD_DIM = 256
FFN_HIDDEN = 2816

DEEPNORM_ALPHA = (2 * DEPTH) ** 0.25
LN_EPS = 1e-5

VMEM_LIMIT_BYTES = 48 * 1024 * 1024


def _layer_norm_rows(y, g, b):
    mu = jnp.mean(y, axis=-1, keepdims=True)
    d = y - mu
    var = jnp.mean(d * d, axis=-1, keepdims=True)
    return d * lax.rsqrt(var + LN_EPS) * g + b


def _ffn_ln_kernel(x_ref, wa_ref, wb_ref, wo_ref, g_ref, b_ref, o_ref, xb_ref, acc_ref):
    k = pl.program_id(1)

    @pl.when(k == 0)
    def _():
        xb_ref[...] = x_ref[...].astype(BF16)
        acc_ref[...] = jnp.zeros_like(acc_ref)

    xb = xb_ref[...]
    a = jnp.dot(xb, wa_ref[...], preferred_element_type=F32)
    b = jnp.dot(xb, wb_ref[...], preferred_element_type=F32)
    h = (a * jax.nn.sigmoid(a)) * b
    acc_ref[...] += jnp.dot(h.astype(BF16), wo_ref[...], preferred_element_type=F32)

    @pl.when(k == pl.num_programs(1) - 1)
    def _():
        y = DEEPNORM_ALPHA * x_ref[...] + acc_ref[...]
        o_ref[...] = _layer_norm_rows(y, g_ref[...], b_ref[...])


def ffn_ln(x, w_in, w_out, g, b, *, tm, th=256):
    m, d = x.shape
    hidden = w_out.shape[0]
    nh = hidden // th
    return pl.pallas_call(
        _ffn_ln_kernel,
        grid=(m // tm, nh),
        in_specs=[
            pl.BlockSpec((tm, d), lambda i, k: (i, 0)),
            pl.BlockSpec((d, th), lambda i, k: (0, k)),
            pl.BlockSpec((d, th), lambda i, k: (0, k + nh)),
            pl.BlockSpec((th, d), lambda i, k: (k, 0)),
            pl.BlockSpec((1, d), lambda i, k: (0, 0)),
            pl.BlockSpec((1, d), lambda i, k: (0, 0)),
        ],
        out_specs=pl.BlockSpec((tm, d), lambda i, k: (i, 0)),
        out_shape=jax.ShapeDtypeStruct((m, d), F32),
        scratch_shapes=[pltpu.VMEM((tm, d), BF16), pltpu.VMEM((tm, d), F32)],
        compiler_params=pltpu.CompilerParams(
            dimension_semantics=("parallel", "arbitrary"), vmem_limit_bytes=VMEM_LIMIT_BYTES),
        name="ffn_ln",
    )(x, w_in, w_in, w_out, g.reshape(1, d), b.reshape(1, d))


def _proj_kernel(x_ref, w_ref, o_ref, xb_ref):
    @pl.when(pl.program_id(1) == 0)
    def _():
        xb_ref[...] = x_ref[...].astype(BF16)

    o_ref[...] = jnp.dot(xb_ref[...], w_ref[...], preferred_element_type=F32)


def proj(x, w, *, tm, tn):
    m, kd = x.shape
    n = w.shape[1]
    return pl.pallas_call(
        _proj_kernel,
        grid=(m // tm, n // tn),
        in_specs=[pl.BlockSpec((tm, kd), lambda i, j: (i, 0)), pl.BlockSpec((kd, tn), lambda i, j: (0, j))],
        out_specs=pl.BlockSpec((tm, tn), lambda i, j: (i, j)),
        out_shape=jax.ShapeDtypeStruct((m, n), F32),
        scratch_shapes=[pltpu.VMEM((tm, kd), BF16)],
        compiler_params=pltpu.CompilerParams(
            dimension_semantics=("parallel", "arbitrary"), vmem_limit_bytes=VMEM_LIMIT_BYTES),
        name="proj",
    )(x, w)


def _out_ln_kernel(h_ref, w_ref, x_ref, g_ref, b_ref, o_ref):
    y = DEEPNORM_ALPHA * x_ref[...] + jnp.dot(h_ref[...].astype(BF16), w_ref[...], preferred_element_type=F32)
    o_ref[...] = _layer_norm_rows(y, g_ref[...], b_ref[...])


def out_ln(h, w, x, g, b, *, tm):
    m, kd = h.shape
    d = w.shape[1]
    return pl.pallas_call(
        _out_ln_kernel,
        grid=(m // tm,),
        in_specs=[
            pl.BlockSpec((tm, kd), lambda i: (i, 0)),
            pl.BlockSpec((kd, d), lambda i: (0, 0)),
            pl.BlockSpec((tm, d), lambda i: (i, 0)),
            pl.BlockSpec((1, d), lambda i: (0, 0)),
            pl.BlockSpec((1, d), lambda i: (0, 0)),
        ],
        out_specs=pl.BlockSpec((tm, d), lambda i: (i, 0)),
        out_shape=jax.ShapeDtypeStruct((m, d), F32),
        compiler_params=pltpu.CompilerParams(
            dimension_semantics=("parallel",), vmem_limit_bytes=VMEM_LIMIT_BYTES),
        name="out_ln",
    )(h, w, x, g.reshape(1, d), b.reshape(1, d))


def _mem_attn_kernel(q_ref, mk_ref, mv_ref, o_ref):
    heads = [slice(h * MEM_HEAD_DIM, (h + 1) * MEM_HEAD_DIM) for h in range(N_MEM_HEADS)]
    scale = MEM_HEAD_DIM ** -0.5
    if mk_ref.ndim == 2:
        k_of = lambda h: mk_ref[:, heads[h]].astype(BF16)
        v_of = lambda h: mv_ref[:, heads[h]].astype(BF16)
        keep = lambda h, s: s
    else:
        rows = N_MEM * N_MEM_HEADS
        k_all = mk_ref[...].reshape(rows, MEM_HEAD_DIM).astype(BF16)
        v_all = mv_ref[...].reshape(rows, MEM_HEAD_DIM).astype(BF16)
        k_of = lambda h: k_all
        v_of = lambda h: v_all
        head_of_col = lax.broadcasted_iota(jnp.int32, (1, rows), 1) % N_MEM_HEADS
        keep = lambda h, s: jnp.where(head_of_col == h, s, NEG_BIG)
    s_all = [keep(h, lax.dot_general(q_ref[0, :, sl].astype(BF16), k_of(h), _NT, preferred_element_type=F32) * scale)
             for h, sl in enumerate(heads)]
    e_all = [jnp.exp(s - jnp.max(s, axis=1, keepdims=True)) for s in s_all]
    for h, sl in enumerate(heads):
        pv = jnp.dot(e_all[h].astype(BF16), v_of(h), preferred_element_type=F32)
        o_ref[0, :, sl] = pv / jnp.sum(e_all[h], axis=1, keepdims=True)


def mem_attn(q, mk, mv, layer, *, tm):
    bsz, t, d = q.shape
    if mk.ndim == 5:
        mk_spec = mv_spec = pl.BlockSpec((None, None, N_MEM, N_MEM_HEADS, MEM_HEAD_DIM),
                                         lambda b, i: (layer, b, 0, 0, 0))
    else:
        mk_spec = pl.BlockSpec((None, N_MEM, d), lambda b, i: (layer, b, 0))
        mv_spec = pl.BlockSpec((None, N_MEM, d), lambda b, i: (layer, b, 1))
    return pl.pallas_call(
        _mem_attn_kernel,
        grid=(bsz, t // tm),
        in_specs=[pl.BlockSpec((1, tm, d), lambda b, i: (b, i, 0)), mk_spec, mv_spec],
        out_specs=pl.BlockSpec((1, tm, d), lambda b, i: (b, i, 0)),
        out_shape=jax.ShapeDtypeStruct((bsz, t, d), F32),
        compiler_params=pltpu.CompilerParams(
            dimension_semantics=("parallel", "arbitrary"), vmem_limit_bytes=VMEM_LIMIT_BYTES),
        name="mem_attn",
    )(q, mk, mv)


CONV_COLS = 1024


def _conv_kernel(x_ref, st_ref, w_ref, b_ref, act_ref, last_ref, prev_ref, *, tc):
    @pl.when(pl.program_id(2) == 0)
    def _():
        prev_ref[...] = st_ref[0]

    x = x_ref[0]
    prev = prev_ref[...]
    row = lax.broadcasted_iota(jnp.int32, (8, CONV_COLS), 0)
    acc = jnp.broadcast_to(b_ref[...], x.shape)
    for s in (3, 2, 1):
        rolled = pltpu.roll(x, s, axis=0)
        top = jnp.where(row < s, pltpu.roll(prev, s, axis=0), rolled[0:8])
        shifted = top if tc == 8 else jnp.concatenate([top, rolled[8:]], axis=0)
        acc = acc + shifted * w_ref[CONV_W - 1 - s:CONV_W - s, :]
    acc = acc + x * w_ref[CONV_W - 1:CONV_W, :]
    act_ref[0] = acc * jax.nn.sigmoid(acc)
    prev_ref[...] = x[tc - 8:tc]
    last_ref[0] = x[tc - 8:tc]


def ssm_conv(zx, conv_state, conv_w, conv_b, *, tc):
    bsz, t = zx.shape[:2]
    ncb = CONV_DIM // CONV_COLS
    col0 = D_INNER // CONV_COLS
    st8 = jnp.pad(conv_state, ((0, 0), (8 - (CONV_W - 1), 0), (0, 0)))
    act, last = pl.pallas_call(
        functools.partial(_conv_kernel, tc=tc),
        grid=(bsz, ncb, t // tc),
        in_specs=[
            pl.BlockSpec((1, tc, CONV_COLS), lambda b, c, i: (b, i, col0 + c)),
            pl.BlockSpec((1, 8, CONV_COLS), lambda b, c, i: (b, 0, c)),
            pl.BlockSpec((CONV_W, CONV_COLS), lambda b, c, i: (0, c)),
            pl.BlockSpec((1, CONV_COLS), lambda b, c, i: (0, c)),
        ],
        out_specs=[
            pl.BlockSpec((1, tc, CONV_COLS), lambda b, c, i: (b, i, c)),
            pl.BlockSpec((1, 8, CONV_COLS), lambda b, c, i: (b, 0, c)),
        ],
        out_shape=[jax.ShapeDtypeStruct((bsz, t, CONV_DIM), F32), jax.ShapeDtypeStruct((bsz, 8, CONV_DIM), F32)],
        scratch_shapes=[pltpu.VMEM((8, CONV_COLS), F32)],
        compiler_params=pltpu.CompilerParams(
            dimension_semantics=("parallel", "parallel", "arbitrary"), vmem_limit_bytes=VMEM_LIMIT_BYTES),
        name="ssm_conv",
    )(zx, st8, conv_w, conv_b.reshape(1, CONV_DIM))
    return act, last[:, 8 - (CONV_W - 1):, :]


def _split3(v):
    hi = v.astype(BF16)
    r1 = v - hi.astype(F32)
    mid = r1.astype(BF16)
    lo = (r1 - mid.astype(F32)).astype(BF16)
    return hi, mid, lo


def _dot01_right(v, ones_mat):
    return sum(jnp.dot(p, ones_mat, preferred_element_type=F32) for p in _split3(v))


def _dot01_left(ones_mat, v):
    return sum(jnp.dot(ones_mat, p, preferred_element_type=F32) for p in _split3(v))


def _softplus(x):
    return jnp.maximum(x, 0.0) + jnp.log1p(jnp.exp(-jnp.abs(x)))


def _ssd_kernel(xs_ref, b_ref, c_ref, z_ref, dt_ref, dtt_ref, dtb_ref, dtbt_ref, alog_ref, alogt_ref,
                d_ref, nw_ref, h0_ref, y_ref, hT_ref, st_ref, yacc_ref, *, c):
    ci = pl.program_id(1)

    @pl.when(ci == 0)
    def _():
        st_ref[...] = h0_ref[0]

    gw = HEADS_PER_GROUP * SSM_HEAD_DIM
    dt = _softplus(dt_ref[0][:, 0:N_SSM_HEADS] + dtb_ref[...])
    dtt = _softplus(dtt_ref[0] + dtbt_ref[...])
    a = dt * -jnp.exp(alog_ref[...])
    at = dtt * -jnp.exp(alogt_ref[...])
    ri = lax.broadcasted_iota(jnp.int32, (c, c), 0)
    cj = lax.broadcasted_iota(jnp.int32, (c, c), 1)
    causal = ri >= cj
    acum = _dot01_left(jnp.where(causal, 1.0, 0.0).astype(BF16), a)
    acum_t = _dot01_right(at, jnp.where(ri <= cj, 1.0, 0.0).astype(BF16))
    a_last = acum[c - 1:c, :]
    expand = jnp.where(lax.broadcasted_iota(jnp.int32, (N_SSM_HEADS, D_INNER), 1) // SSM_HEAD_DIM
                       == lax.broadcasted_iota(jnp.int32, (N_SSM_HEADS, D_INNER), 0), 1.0, 0.0).astype(BF16)
    e_dt = _dot01_right(dt, expand)
    e_in = _dot01_right(jnp.exp(acum), expand)
    e_out = _dot01_right(jnp.exp(a_last - acum), expand)
    e_chunk = e_in[c - 1:c, :]
    xs = xs_ref[0]
    xdt = xs * e_dt
    head_of_col = lax.broadcasted_iota(jnp.int32, (c, gw), 1) // SSM_HEAD_DIM
    for g in range(SSM_GROUPS):
        gs = slice(g * gw, (g + 1) * gw)
        bg = b_ref[0][:, g * D_STATE:(g + 1) * D_STATE].astype(BF16)
        cg = c_ref[0][:, g * D_STATE:(g + 1) * D_STATE].astype(BF16)
        cb = lax.dot_general(cg, bg, _NT, preferred_element_type=F32)
        xg = xdt[:, gs]
        yg = jnp.zeros((c, gw), F32)
        for r in range(HEADS_PER_GROUP):
            h = g * HEADS_PER_GROUP + r
            seg = acum[:, h:h + 1] - acum_t[h:h + 1, :]
            m = (cb * jnp.exp(jnp.where(causal, seg, -jnp.inf))).astype(BF16)
            xm = jnp.where(head_of_col == r, xg, 0.0).astype(BF16)
            yg = yg + jnp.dot(m, xm, preferred_element_type=F32)
        state = st_ref[g]
        y_off = jnp.dot(cg, state.astype(BF16), preferred_element_type=F32) * e_in[:, gs]
        yacc_ref[:, gs] = yg + y_off + xs[:, gs] * d_ref[:, gs]
        xd = (xg * e_out[:, gs]).astype(BF16)
        st_ref[g] = state * e_chunk[:, gs] + lax.dot_general(bg, xd, (((0,), (0,)), ((), ())),
                                                             preferred_element_type=F32)
    z = z_ref[0]
    yz = yacc_ref[...] * (z * jax.nn.sigmoid(z))
    y_ref[0] = yz * lax.rsqrt(jnp.mean(yz * yz, axis=-1, keepdims=True) + LN_EPS) * nw_ref[...]

    @pl.when(ci == pl.num_programs(1) - 1)
    def _():
        hT_ref[0] = st_ref[...]


def ssd_scan(act, zx, dt_raw, dt_bias, a_log, d_skip, norm_w, ssm_state, *, c):
    bsz, t = act.shape[:2]
    gw = HEADS_PER_GROUP * SSM_HEAD_DIM
    bc_w = SSM_GROUPS * D_STATE
    h0 = ssm_state.reshape(bsz, SSM_GROUPS, HEADS_PER_GROUP, SSM_HEAD_DIM, D_STATE)
    h0 = h0.transpose(0, 1, 4, 2, 3).reshape(bsz, SSM_GROUPS, D_STATE, gw)
    dtt = jnp.swapaxes(dt_raw[..., :N_SSM_HEADS], 1, 2)
    row = lambda v: v.reshape(1, -1)
    col = lambda v: v.reshape(-1, 1)
    full = lambda shape: pl.BlockSpec(shape, lambda b, i: (0,) * len(shape))
    y, h_t = pl.pallas_call(
        functools.partial(_ssd_kernel, c=c),
        grid=(bsz, t // c),
        in_specs=[
            pl.BlockSpec((1, c, D_INNER), lambda b, i: (b, i, 0)),
            pl.BlockSpec((1, c, bc_w), lambda b, i: (b, i, D_INNER // bc_w)),
            pl.BlockSpec((1, c, bc_w), lambda b, i: (b, i, D_INNER // bc_w + 1)),
            pl.BlockSpec((1, c, D_INNER), lambda b, i: (b, i, 0)),
            pl.BlockSpec((1, c, 128), lambda b, i: (b, i, 0)),
            pl.BlockSpec((1, N_SSM_HEADS, c), lambda b, i: (b, 0, i)),
            full((1, N_SSM_HEADS)), full((N_SSM_HEADS, 1)), full((1, N_SSM_HEADS)), full((N_SSM_HEADS, 1)),
            full((1, D_INNER)), full((1, D_INNER)),
            pl.BlockSpec((1, SSM_GROUPS, D_STATE, gw), lambda b, i: (b, 0, 0, 0)),
        ],
        out_specs=[
            pl.BlockSpec((1, c, D_INNER), lambda b, i: (b, i, 0)),
            pl.BlockSpec((1, SSM_GROUPS, D_STATE, gw), lambda b, i: (b, 0, 0, 0)),
        ],
        out_shape=[jax.ShapeDtypeStruct((bsz, t, D_INNER), F32),
                   jax.ShapeDtypeStruct((bsz, SSM_GROUPS, D_STATE, gw), F32)],
        scratch_shapes=[pltpu.VMEM((SSM_GROUPS, D_STATE, gw), F32), pltpu.VMEM((c, D_INNER), F32)],
        compiler_params=pltpu.CompilerParams(
            dimension_semantics=("parallel", "arbitrary"), vmem_limit_bytes=VMEM_LIMIT_BYTES),
        name="ssd_scan",
    )(act, act, act, zx, dt_raw, dtt, row(dt_bias), col(dt_bias), row(a_log), col(a_log),
      row(jnp.repeat(d_skip, SSM_HEAD_DIM)), row(norm_w), h0)
    h_t = h_t.reshape(bsz, SSM_GROUPS, D_STATE, HEADS_PER_GROUP, SSM_HEAD_DIM)
    return y, h_t.transpose(0, 1, 3, 4, 2).reshape(bsz, N_SSM_HEADS, SSM_HEAD_DIM, D_STATE)


INT_MIN = -2 ** 31
NEG_BIG = -1e30
_NT = (((1,), (1,)), ((), ()))


def _float_key(x):
    x = jnp.where(x == 0.0, 0.0, x)
    bits = lax.bitcast_convert_type(x, jnp.int32)
    return bits ^ ((bits >> 31) & 0x7FFFFFFF)


def _dsa_prompt_kernel(qt_ref, qit_ref, wit_ref, k_ref, vxt_ref, ki_ref, o_ref, key_ref, m_ref, acc_ref,
                         s_ref, p_ref, *, tq, tk, topk):
    i = pl.program_id(1)
    nkb = ((i + 1) * tq + tk - 1) // tk
    gq = KV_GROUP * tq
    rep8 = lambda v: jnp.broadcast_to(v, (8, tq))
    as3 = lambda x: x.reshape(tk // 8, 8, tq)
    w_heads = wit_ref[0, 0] * (N_IDX_HEADS ** -0.5)
    q_pos = i * tq + lax.broadcasted_iota(jnp.int32, (tk, tq), 1)

    def score_block(kb, c):
        off = pl.multiple_of(kb * tk, tk)
        s = jnp.dot(ki_ref[0, pl.ds(off, tk), :], qit_ref[0, 0], preferred_element_type=F32)
        acc = jnp.zeros((tk, tq), F32)
        for h in range(N_IDX_HEADS):
            acc = acc + w_heads[h:h + 1, :] * jnp.maximum(s[:, h * tq:(h + 1) * tq], 0.0)
        kpos = off + lax.broadcasted_iota(jnp.int32, (tk, tq), 0)
        key_ref[kb] = jnp.where(kpos <= q_pos, _float_key(acc), INT_MIN)
        return c

    lax.fori_loop(0, nkb, score_block, 0)

    def count(cand, strict):
        def body(kb, accs):
            blk = key_ref[kb]
            accs = list(accs)
            for r in range(tk // 8):
                part = blk[r * 8:(r + 1) * 8, :]
                hit = (part > cand) if strict else (part >= cand)
                accs[r % 4] = accs[r % 4] + jnp.where(hit, 1.0, 0.0)
            return tuple(accs)
        a0, a1, a2, a3 = lax.fori_loop(0, nkb, body, (jnp.zeros((8, tq), F32),) * 4)
        return jnp.sum((a0 + a1) + (a2 + a3), axis=0, keepdims=True)

    kf = float(topk)
    zero = jnp.zeros((8, tq), jnp.int32)
    thr = jnp.where(rep8(count(zero, False)) >= kf, zero, jnp.full((8, tq), INT_MIN, jnp.int32))

    def bit_step(it, thr):
        cand = thr | jnp.left_shift(jnp.int32(1), 30 - it)
        return jnp.where(rep8(count(cand, False)) >= kf, cand, thr)

    thr = lax.fori_loop(0, 31, bit_step, thr)
    need = jnp.where(thr == INT_MIN, 0.0, kf - rep8(count(thr, True)))

    m_ref[...] = jnp.full(m_ref.shape, NEG_BIG, F32)
    acc_ref[...] = jnp.zeros(acc_ref.shape, F32)
    lower = jnp.where(lax.broadcasted_iota(jnp.int32, (tk, tk), 1) < lax.broadcasted_iota(jnp.int32, (tk, tk), 0),
                      1.0, 0.0).astype(BF16)

    def attend_block(kb, eq_before):
        off = pl.multiple_of(kb * tk, tk)
        key3 = as3(key_ref[kb])
        eq3 = key3 == thr[None]
        eqf = jnp.where(eq3, 1.0, 0.0)
        rank3 = as3(jnp.dot(lower, eqf.reshape(tk, tq).astype(BF16), preferred_element_type=F32)) + eq_before[None]
        bias = jnp.where(key3 > thr[None], 0.0,
                         jnp.where(eq3, jnp.where(rank3 < need[None], 0.0, NEG_BIG), NEG_BIG)).reshape(tk, tq)
        bias4 = jnp.concatenate([bias] * KV_GROUP, axis=1)
        col_max = []
        for j in range(N_KV_HEADS):
            kblk = k_ref[0, pl.ds(off, tk), j * 64:(j + 1) * 64]
            s = jnp.dot(kblk, qt_ref[0, 0, j], preferred_element_type=F32) + bias4
            s_ref[j] = s
            col_max.append(jnp.max(s, axis=0, keepdims=True))
        alpha = []
        for j in range(N_KV_HEADS):
            m_old = m_ref[j]
            m_new = jnp.maximum(m_old, col_max[j])
            alpha.append(jnp.exp(m_old - m_new))
            p_ref[j] = jnp.exp(s_ref[j] - m_new).astype(BF16)
            m_ref[j] = m_new
        for j in range(N_KV_HEADS):
            pv = jnp.dot(vxt_ref[0, j, kb], p_ref[j], preferred_element_type=F32)
            acc_ref[j] = alpha[j] * acc_ref[j] + pv
        return eq_before + rep8(jnp.sum(jnp.sum(eqf, axis=0), axis=0, keepdims=True))

    lax.fori_loop(0, nkb, attend_block, jnp.zeros((8, tq), F32))

    for j in range(N_KV_HEADS):
        a = acc_ref[j]
        o_t = a[0:64, :] / a[64:65, :]
        for g in range(KV_GROUP):
            h = j * KV_GROUP + g
            o_ref[0, :, h * 64:(h + 1) * 64] = o_t[:, g * tq:(g + 1) * tq].T


def dsa_prompt_attend(q, k, v, qi, ki, wi, *, tq, tk):
    bsz, t = q.shape[:2]
    nq, nk = t // tq, t // tk
    topk = min(TOPK_MAX, t // 4)
    qt = (q * ATT_HEAD_DIM ** -0.5).astype(BF16).reshape(bsz, nq, tq, N_KV_HEADS, KV_GROUP, 64)
    qt = qt.transpose(0, 1, 3, 5, 4, 2).reshape(bsz, nq, N_KV_HEADS, 64, KV_GROUP * tq)
    qit = (qi * IDX_DIM ** -0.5).astype(BF16).reshape(bsz, nq, tq, N_IDX_HEADS, 64)
    qit = qit.transpose(0, 1, 4, 3, 2).reshape(bsz, nq, 64, N_IDX_HEADS * tq)
    wit = wi.reshape(bsz, nq, tq, N_IDX_HEADS).transpose(0, 1, 3, 2)
    vt = v.astype(BF16).reshape(bsz, nk, tk, N_KV_HEADS, 64).transpose(0, 3, 1, 4, 2)
    vxt = jnp.concatenate([vt, jnp.ones_like(vt)], axis=3)
    kern = functools.partial(_dsa_prompt_kernel, tq=tq, tk=tk, topk=topk)
    return pl.pallas_call(
        kern,
        grid=(bsz, nq),
        in_specs=[
            pl.BlockSpec((1, 1, N_KV_HEADS, 64, KV_GROUP * tq), lambda b, i: (b, i, 0, 0, 0)),
            pl.BlockSpec((1, 1, 64, N_IDX_HEADS * tq), lambda b, i: (b, i, 0, 0)),
            pl.BlockSpec((1, 1, N_IDX_HEADS, tq), lambda b, i: (b, i, 0, 0)),
            pl.BlockSpec((1, t, 256), lambda b, i: (b, 0, 0), pipeline_mode=pl.Buffered(1)),
            pl.BlockSpec((1, N_KV_HEADS, nk, 128, tk), lambda b, i: (b, 0, 0, 0, 0), pipeline_mode=pl.Buffered(1)),
            pl.BlockSpec((1, t, IDX_DIM), lambda b, i: (b, 0, 0), pipeline_mode=pl.Buffered(1)),
        ],
        out_specs=pl.BlockSpec((1, tq, 1024), lambda b, i: (b, i, 0)),
        out_shape=jax.ShapeDtypeStruct((bsz, t, 1024), F32),
        scratch_shapes=[
            pltpu.VMEM((nk, tk, tq), jnp.int32),
            pltpu.VMEM((N_KV_HEADS, 1, KV_GROUP * tq), F32),
            pltpu.VMEM((N_KV_HEADS, 128, KV_GROUP * tq), F32),
            pltpu.VMEM((N_KV_HEADS, tk, KV_GROUP * tq), F32),
            pltpu.VMEM((N_KV_HEADS, tk, KV_GROUP * tq), BF16),
        ],
        compiler_params=pltpu.CompilerParams(
            dimension_semantics=("parallel", "arbitrary"), vmem_limit_bytes=VMEM_LIMIT_BYTES),
        name="dsa_prompt_attend",
    )(qt, qit, wit, k.astype(BF16), vxt, ki.astype(BF16))


PAGES_PER_STEP = 8


def _sample_select_kernel(pt_ref, qi_ref, wi_ref, kinew_ref, *rest, t, n_steps, topk):
    ki_refs = rest[:PAGES_PER_STEP]
    bias_ref, qis_ref, wib_ref, key_ref = rest[PAGES_PER_STEP:]
    p = pl.program_id(1)
    w = PAGES_PER_STEP * PAGE_SIZE
    nl = w // 128

    @pl.when(p == 0)
    def _():
        for h in range(N_IDX_HEADS):
            qis_ref[h * t:(h + 1) * t, :] = qi_ref[0, :, h * 64:(h + 1) * 64]
            wib_ref[h] = jnp.broadcast_to(wi_ref[0, :, h:h + 1] * (N_IDX_HEADS ** -0.5), (t, 128))

    def scores(s):
        acc = jnp.zeros((t, s.shape[1]), F32)
        for h in range(N_IDX_HEADS):
            wfull = jnp.concatenate([wib_ref[h]] * (s.shape[1] // 128), axis=1)
            acc = acc + wfull * jnp.maximum(s[h * t:(h + 1) * t, :], 0.0)
        return acc

    ki_t = jnp.concatenate([r[...] for r in ki_refs], axis=1).astype(BF16)
    key_ref[p] = _float_key(scores(jnp.dot(qis_ref[...].astype(BF16), ki_t, preferred_element_type=F32)))

    @pl.when(p == n_steps - 1)
    def _():
        knew = _float_key(scores(lax.dot_general(qis_ref[...].astype(BF16), kinew_ref[0].astype(BF16), _NT,
                                                 preferred_element_type=F32)))
        n_idx = lax.broadcasted_iota(jnp.int32, (t, 128), 1)
        r_idx = lax.broadcasted_iota(jnp.int32, (t, 128), 0)
        knew = jnp.where(n_idx <= r_idx, knew, INT_MIN)
        key_ref[n_steps] = jnp.concatenate([knew, jnp.full((t, w - 128), INT_MIN, jnp.int32)], axis=1)

        def count(cand, strict):
            def body(blk, acc):
                kb = key_ref[blk]
                for c in range(nl):
                    part = kb[:, c * 128:(c + 1) * 128]
                    hit = (part > cand) if strict else (part >= cand)
                    acc = acc + jnp.where(hit, 1.0, 0.0)
                return acc
            acc = lax.fori_loop(0, n_steps + 1, body, jnp.zeros((t, 128), F32))
            return jnp.sum(acc, axis=1, keepdims=True)

        kf = float(topk)
        zero = jnp.zeros((t, 128), jnp.int32)
        thr = jnp.where(count(zero, False) >= kf, zero, jnp.full((t, 128), INT_MIN, jnp.int32))

        def bit_step(it, thr):
            cand = thr | jnp.left_shift(jnp.int32(1), 30 - it)
            return jnp.where(count(cand, False) >= kf, cand, thr)

        thr = lax.fori_loop(0, 31, bit_step, thr)
        need = jnp.where(thr == INT_MIN, 0.0, kf - count(thr, True))
        floor_thr = jnp.where(thr == INT_MIN, INT_MIN + 1, thr)
        untied = jnp.where((thr == INT_MIN) | (count(thr, False) <= kf), 1.0, 0.0)
        no_ties = jnp.min(untied) > 0.0

        @pl.when(no_ties)
        def _():
            wide = jnp.concatenate([floor_thr] * nl, axis=1)

            def emit_simple(blk, c):
                bias_ref[0, blk] = jnp.where(key_ref[blk] >= wide, 0.0, NEG_BIG)
                return c

            lax.fori_loop(0, n_steps + 1, emit_simple, 0)

        upper = jnp.where(lax.broadcasted_iota(jnp.int32, (128, 128), 0)
                          < lax.broadcasted_iota(jnp.int32, (128, 128), 1), 1.0, 0.0).astype(BF16)

        def emit(blk, eq_before):
            kb = key_ref[blk]
            parts = []
            for c in range(nl):
                part = kb[:, c * 128:(c + 1) * 128]
                eq = part == thr
                eqf = jnp.where(eq, 1.0, 0.0)
                rank = jnp.dot(eqf.astype(BF16), upper, preferred_element_type=F32) + eq_before
                parts.append(jnp.where(part > thr, 0.0,
                                       jnp.where(eq, jnp.where(rank < need, 0.0, NEG_BIG), NEG_BIG)))
                eq_before = eq_before + jnp.sum(eqf, axis=1, keepdims=True)
            bias_ref[0, blk] = jnp.concatenate(parts, axis=1)
            return eq_before

        @pl.when(jnp.logical_not(no_ties))
        def _():
            lax.fori_loop(0, n_steps + 1, emit, jnp.zeros((t, 128), F32))


def _sample_attend_kernel(pt_ref, q_ref, bias_ref, biasnew_ref, knew_ref, vnew_ref, *rest, t, n_steps):
    k_refs = rest[:PAGES_PER_STEP]
    v_refs = rest[PAGES_PER_STEP:2 * PAGES_PER_STEP]
    o_ref, qs_ref, m_ref, l_ref, acc_ref = rest[2 * PAGES_PER_STEP:]
    p = pl.program_id(1)
    rows = KV_GROUP * t

    @pl.when(p == 0)
    def _():
        for h in range(N_ATT_HEADS):
            j, g = divmod(h, KV_GROUP)
            qs_ref[j, g * t:(g + 1) * t, :] = q_ref[0, :, h * 64:(h + 1) * 64]
        m_ref[...] = jnp.full(m_ref.shape, NEG_BIG, F32)
        l_ref[...] = jnp.zeros(l_ref.shape, F32)
        acc_ref[...] = jnp.zeros(acc_ref.shape, F32)

    def update(k_t, v_t, bias):
        bias4 = jnp.concatenate([bias] * KV_GROUP, axis=0)
        s_all = [jnp.dot(qs_ref[j].astype(BF16), k_t(j).astype(BF16), preferred_element_type=F32) + bias4
                 for j in range(N_KV_HEADS)]
        alphas, probs = [], []
        for j in range(N_KV_HEADS):
            m_old = m_ref[j]
            m_new = jnp.maximum(m_old, jnp.max(s_all[j], axis=1, keepdims=True))
            alpha = jnp.exp(m_old - m_new)
            pr = jnp.exp(s_all[j] - m_new[:, 0:1])
            l_ref[j] = alpha * l_ref[j] + jnp.sum(pr, axis=1, keepdims=True)
            m_ref[j] = m_new
            alphas.append(alpha)
            probs.append(pr.astype(BF16))
        for j in range(N_KV_HEADS):
            pv = lax.dot_general(probs[j], v_t(j).astype(BF16), _NT, preferred_element_type=F32)
            acc_ref[j] = alphas[j][:, 0:64] * acc_ref[j] + pv

    update(lambda j: jnp.concatenate([r[j] for r in k_refs], axis=1),
           lambda j: jnp.concatenate([r[j] for r in v_refs], axis=1), bias_ref[0, 0])

    @pl.when(p == n_steps - 1)
    def _():
        update(lambda j: knew_ref[0, j * 64:(j + 1) * 64, :], lambda j: vnew_ref[0, j * 64:(j + 1) * 64, :],
               biasnew_ref[0, 0, :, 0:128])
        for h in range(N_ATT_HEADS):
            j, g = divmod(h, KV_GROUP)
            sl = slice(g * t, (g + 1) * t)
            o_ref[0, :, h * 64:(h + 1) * 64] = acc_ref[j, sl, :] / l_ref[j, sl, 0:64]


def dsa_sample_attend(q, k, v, qi, ki, wi, cache_k, cache_v, cache_kidx, layer, page_table):
    bsz, t = q.shape[:2]
    n_pages = page_table.shape[1]
    past = n_pages * PAGE_SIZE
    n_steps = n_pages // PAGES_PER_STEP
    w = PAGES_PER_STEP * PAGE_SIZE
    topk = min(TOPK_MAX, (past + t) // 4)
    pad = lambda a: jnp.pad(a, ((0, 0), (0, PAGE_SIZE - t), (0, 0)))
    pool_k = cache_k.transpose(0, 1, 3, 4, 2)
    pool_v = cache_v.transpose(0, 1, 3, 4, 2)
    pool_ki = cache_kidx.transpose(0, 1, 3, 2)
    knew_t = jnp.swapaxes(pad(k), 1, 2)
    vnew_t = jnp.swapaxes(pad(v), 1, 2)

    def page_spec(r, *major):
        zeros = (0,) * (1 + len(major))
        return pl.BlockSpec((None, None) + major + (PAGE_SIZE,),
                            lambda b, p, pt: (layer, pt[b, p * PAGES_PER_STEP + r]) + zeros)

    bias = pl.pallas_call(
        functools.partial(_sample_select_kernel, t=t, n_steps=n_steps, topk=topk),
        grid_spec=pltpu.PrefetchScalarGridSpec(
            num_scalar_prefetch=1,
            grid=(bsz, n_steps),
            in_specs=[
                pl.BlockSpec((1, t, 512), lambda b, p, pt: (b, 0, 0)),
                pl.BlockSpec((1, t, N_IDX_HEADS), lambda b, p, pt: (b, 0, 0)),
                pl.BlockSpec((1, PAGE_SIZE, IDX_DIM), lambda b, p, pt: (b, 0, 0)),
            ] + [page_spec(r, IDX_DIM) for r in range(PAGES_PER_STEP)],
            out_specs=pl.BlockSpec((1, n_steps + 1, t, w), lambda b, p, pt: (b, 0, 0, 0)),
            scratch_shapes=[
                pltpu.VMEM((N_IDX_HEADS * t, 64), F32),
                pltpu.VMEM((N_IDX_HEADS, t, 128), F32),
                pltpu.VMEM((n_steps + 1, t, w), jnp.int32),
            ]),
        out_shape=jax.ShapeDtypeStruct((bsz, n_steps + 1, t, w), F32),
        compiler_params=pltpu.CompilerParams(
            dimension_semantics=("parallel", "arbitrary"), vmem_limit_bytes=VMEM_LIMIT_BYTES),
        name="dsa_sample_select",
    )(page_table, qi * IDX_DIM ** -0.5, wi, pad(ki), *([pool_ki] * PAGES_PER_STEP))

    return pl.pallas_call(
        functools.partial(_sample_attend_kernel, t=t, n_steps=n_steps),
        grid_spec=pltpu.PrefetchScalarGridSpec(
            num_scalar_prefetch=1,
            grid=(bsz, n_steps),
            in_specs=[
                pl.BlockSpec((1, t, 1024), lambda b, p, pt: (b, 0, 0)),
                pl.BlockSpec((1, 1, t, w), lambda b, p, pt: (b, p, 0, 0)),
                pl.BlockSpec((1, 1, t, w), lambda b, p, pt: (b, n_steps, 0, 0)),
                pl.BlockSpec((1, 256, PAGE_SIZE), lambda b, p, pt: (b, 0, 0)),
                pl.BlockSpec((1, 256, PAGE_SIZE), lambda b, p, pt: (b, 0, 0)),
            ] + [page_spec(r, N_KV_HEADS, ATT_HEAD_DIM) for r in range(PAGES_PER_STEP)] * 2,
            out_specs=pl.BlockSpec((1, t, 1024), lambda b, p, pt: (b, 0, 0)),
            scratch_shapes=[
                pltpu.VMEM((N_KV_HEADS, KV_GROUP * t, 64), F32),
                pltpu.VMEM((N_KV_HEADS, KV_GROUP * t, 128), F32),
                pltpu.VMEM((N_KV_HEADS, KV_GROUP * t, 128), F32),
                pltpu.VMEM((N_KV_HEADS, KV_GROUP * t, 64), F32),
            ]),
        out_shape=jax.ShapeDtypeStruct((bsz, t, 1024), F32),
        compiler_params=pltpu.CompilerParams(
            dimension_semantics=("parallel", "arbitrary"), vmem_limit_bytes=VMEM_LIMIT_BYTES),
        name="dsa_sample_attend",
    )(page_table, q * ATT_HEAD_DIM ** -0.5, bias, bias, knew_t, vnew_t,
      *([pool_k] * PAGES_PER_STEP), *([pool_v] * PAGES_PER_STEP))


def dsa_project(x, w_main, w_tail, pos, tm):
    bsz, t = x.shape[:2]
    x2 = x.reshape(bsz * t, D_MODEL)
    main = proj(x2, w_main, tm=tm, tn=512).reshape(bsz, t, -1)
    tail = proj(x2, w_tail, tm=tm, tn=128).reshape(bsz, t, -1)
    main, tail = rope_apply(main, tail, pos, tr=min(t, 512))
    q, k, v, qi = jnp.split(main, [1024, 1280, 1536], axis=-1)
    ki, wi = tail[..., :IDX_DIM], tail[..., IDX_DIM:IDX_DIM + N_IDX_HEADS]
    return (q, k.reshape(bsz, t, N_KV_HEADS, ATT_HEAD_DIM), v.reshape(bsz, t, N_KV_HEADS, ATT_HEAD_DIM), qi, ki, wi)


ROPE_V_BLOCKS = (10, 11)


def _rope_kernel(main_ref, tail_ref, cos_ref, sin_ref, mo_ref, to_ref):
    cos, sin = cos_ref[...], sin_ref[...]
    lane = lax.broadcasted_iota(jnp.int32, cos.shape, 1)
    first_half = lane % ATT_HEAD_DIM < ATT_HEAD_DIM // 2

    def rot(xb):
        partner = jnp.where(first_half, -pltpu.roll(xb, 128 - 32, axis=1), pltpu.roll(xb, 32, axis=1))
        return xb * cos + partner * sin

    for cb in range(main_ref.shape[2] // 128):
        xb = main_ref[0, :, cb * 128:(cb + 1) * 128]
        mo_ref[0, :, cb * 128:(cb + 1) * 128] = xb if cb in ROPE_V_BLOCKS else rot(xb)
    tb = tail_ref[0]
    to_ref[0] = jnp.where(lane < IDX_DIM, rot(tb), tb)


def rope_apply(main, tail, pos, *, tr):
    bsz, t, wm = main.shape
    inv_freq = ROPE_THETA ** (-jnp.arange(0, ATT_HEAD_DIM, 2, dtype=F32) / ATT_HEAD_DIM)
    ang = pos.astype(F32)[:, None] * inv_freq[None, :]
    cos = jnp.tile(jnp.cos(ang), (1, 4))
    sin = jnp.tile(jnp.sin(ang), (1, 4))
    return pl.pallas_call(
        _rope_kernel,
        grid=(bsz, t // tr),
        in_specs=[
            pl.BlockSpec((1, tr, wm), lambda b, i: (b, i, 0)),
            pl.BlockSpec((1, tr, 128), lambda b, i: (b, i, 0)),
            pl.BlockSpec((tr, 128), lambda b, i: (i, 0)),
            pl.BlockSpec((tr, 128), lambda b, i: (i, 0)),
        ],
        out_specs=[pl.BlockSpec((1, tr, wm), lambda b, i: (b, i, 0)), pl.BlockSpec((1, tr, 128), lambda b, i: (b, i, 0))],
        out_shape=[jax.ShapeDtypeStruct(main.shape, F32), jax.ShapeDtypeStruct(tail.shape, F32)],
        compiler_params=pltpu.CompilerParams(
            dimension_semantics=("parallel", "parallel"), vmem_limit_bytes=VMEM_LIMIT_BYTES),
        name="rope",
    )(main, tail, cos, sin)


def run_trunk(x, mem_k, mem_v, conv0, ssm0, attend, pos, p, *, tm, tm_mem):
    conv_out, ssm_out, k_out, v_out, ki_out = [], [], [], [], []
    bsz, t = x.shape[:2]
    m = bsz * t
    x = x.reshape(m, D_MODEL)
    for i in range(DEPTH):
        j = i // N_MIXERS
        g, b = p['ln_g'][i], p['ln_b'][i]
        if i % N_MIXERS == 0:
            zx = proj(x, p['w_ssm_main'][j], tm=tm, tn=512).reshape(bsz, t, -1)
            dt_raw = proj(x, p['w_ssm_dt'][j], tm=tm, tn=128).reshape(bsz, t, -1)
            act, cs = ssm_conv(zx, conv0[j], p['ssm_conv_w'][j], p['ssm_conv_b'][j], tc=min(t, 512))
            h, ss = ssd_scan(act, zx, dt_raw, p['ssm_dt_bias'][j], p['ssm_a_log'][j], p['ssm_d'][j],
                             p['ssm_norm_w'][j], ssm0[j], c=SSM_CHUNK if t % SSM_CHUNK == 0 else t)
            conv_out.append(cs)
            ssm_out.append(ss)
            w_o = p['w_ssm_out'][j]
        else:
            q, k, v, qi, ki, wi = dsa_project(x.reshape(bsz, t, D_MODEL), p['w_att_main'][j], p['w_att_tail'][j],
                                              pos, tm)
            h = attend(j, q.reshape(bsz, t, -1), k.reshape(bsz, t, -1), v.reshape(bsz, t, -1),
                       qi.reshape(bsz, t, -1), ki, wi)
            k_out.append(k)
            v_out.append(v)
            ki_out.append(ki)
            w_o = p['w_att_out'][j]
        x = out_ln(h.reshape(m, -1), w_o, x, g[0], b[0], tm=tm_mem)
        q_mem = proj(x, p['w_mem_q'][i], tm=tm, tn=512).reshape(bsz, t, D_MODEL)
        o_mem = mem_attn(q_mem, mem_k, mem_v, i, tm=min(t, tm_mem))
        x = out_ln(o_mem.reshape(m, D_MODEL), p['w_mem_out'][i], x, g[1], b[1], tm=tm_mem)
        x = ffn_ln(x, p['w_ffn_in'][i], p['w_ffn_out'][i], g[2], b[2], tm=tm)
    return (x.reshape(bsz, t, D_MODEL), jnp.stack(k_out), jnp.stack(v_out), jnp.stack(ki_out),
            jnp.stack(conv_out), jnp.stack(ssm_out))


def kernel(x_prompt, x_sample, cache_k, cache_v, cache_kidx, cache_mem_k, cache_mem_v, state_conv, state_ssm,
           page_table, mem_prompt, w_ssm_in, ssm_conv_w, ssm_conv_b, ssm_dt_bias, ssm_a_log, ssm_d, ssm_norm_w,
           w_ssm_out, w_att_in, w_att_out, w_mem_q, w_mem_kv, w_mem_out, w_ffn_in, w_ffn_out, ln_g, ln_b):
    zx_cols = D_INNER + CONV_DIM
    att_cols = sum(ATT_SPLITS[:4])
    pad128 = lambda w: jnp.pad(w, ((0, 0), (0, 0), (0, 128 - w.shape[-1])))
    params = {'w_ssm_main': w_ssm_in[..., :zx_cols].astype(BF16), 'w_ssm_dt': pad128(w_ssm_in[..., zx_cols:]).astype(BF16),
              'ssm_conv_w': ssm_conv_w, 'ssm_conv_b': ssm_conv_b, 'ssm_dt_bias': ssm_dt_bias,
              'ssm_a_log': ssm_a_log, 'ssm_d': ssm_d, 'ssm_norm_w': ssm_norm_w, 'w_ssm_out': w_ssm_out.astype(BF16),
              'w_att_main': w_att_in[..., :att_cols].astype(BF16), 'w_att_tail': pad128(w_att_in[..., att_cols:]).astype(BF16),
              'w_att_out': w_att_out.astype(BF16),
              'w_mem_q': w_mem_q.astype(BF16), 'w_mem_out': w_mem_out.astype(BF16),
              'w_ffn_in': w_ffn_in.astype(BF16), 'w_ffn_out': w_ffn_out.astype(BF16),
              'ln_g': ln_g, 'ln_b': ln_b}
    bp, seq = x_prompt.shape[:2]
    pos_p = jnp.arange(seq, dtype=jnp.int32)
    w_mem_kv_bf = w_mem_kv.astype(BF16)
    mem_rows = mem_prompt.reshape(bp * N_MEM, D_MODEL)
    mem_kv = jnp.stack([proj(mem_rows, w_mem_kv_bf[l], tm=bp * N_MEM, tn=512) for l in range(DEPTH)])
    memk_prompt = mem_kv[..., :D_MODEL].reshape(DEPTH, bp, N_MEM, N_MEM_HEADS, MEM_HEAD_DIM)
    memv_prompt = mem_kv[..., D_MODEL:].reshape(DEPTH, bp, N_MEM, N_MEM_HEADS, MEM_HEAD_DIM)
    n_ssm = (DEPTH + 1) // 2
    conv0 = jnp.zeros((n_ssm, bp, CONV_W - 1, CONV_DIM), x_prompt.dtype)
    ssm0 = jnp.zeros((n_ssm, bp, N_SSM_HEADS, SSM_HEAD_DIM, D_STATE), x_prompt.dtype)
    attend_p = lambda j, q, k, v, qi, ki, wi: dsa_prompt_attend(q, k, v, qi, ki, wi, tq=min(256, seq),
                                                                tk=min(512, seq))
    y_prompt, k_prompt, v_prompt, kidx_prompt, conv_prompt, ssm_prompt = run_trunk(
        x_prompt, mem_kv, mem_kv, conv0, ssm0, attend_p, pos_p, params, tm=min(1024, bp * seq), tm_mem=min(512, seq))
    past = page_table.shape[1] * PAGE_SIZE
    bs, ts = x_sample.shape[:2]
    pos_s = past + jnp.arange(ts, dtype=jnp.int32)
    attend_s = lambda j, q, k, v, qi, ki, wi: dsa_sample_attend(q, k, v, qi, ki, wi, cache_k, cache_v, cache_kidx, j,
                                                                page_table)
    y_sample, k_sample, v_sample, kidx_sample, conv_sample, ssm_sample = run_trunk(
        x_sample, cache_mem_k, cache_mem_v, state_conv, state_ssm, attend_s, pos_s, params, tm=bs * ts, tm_mem=bs * ts)
    return (y_prompt, y_sample, k_prompt, v_prompt, kidx_prompt, conv_prompt, ssm_prompt, memk_prompt, memv_prompt,
            k_sample, v_sample, kidx_sample, conv_sample, ssm_sample)
```

```python
import functools

import jax
import jax.numpy as jnp
from jax import lax
from jax.experimental import pallas as pl
from jax.experimental.pallas import tpu as pltpu

F32 = jnp.float32
BF16 = jnp.bfloat16

DEPTH = 4
N_MIXERS = 2
D_MODEL = 1024
PAGE_SIZE = 128

D_INNER = 2048
SSM_HEAD_DIM = 64
N_SSM_HEADS = 32
SSM_GROUPS = 4
HEADS_PER_GROUP = 8
D_STATE = 128
CONV_W = 4
CONV_DIM = D_INNER + 2 * SSM_GROUPS * D_STATE
SSM_CHUNK = 128

ATT_HEAD_DIM = 64
N_ATT_HEADS = 16
N_KV_HEADS = 4
KV_GROUP = 4
N_IDX_HEADS = 8
IDX_DIM = 64
TOPK_MAX = 256
ATT_SPLITS = [1024, 256, 256, 512, 64, 8]
ROPE_THETA = 10000.0

N_MEM = 256
N_MEM_HEADS = 4
MEM_HEAD_DIM = 256
FFN_HIDDEN = 2816

DEEPNORM_ALPHA = (2 * DEPTH) ** 0.25
LN_EPS = 1e-5

VMEM_LIMIT_BYTES = 48 * 1024 * 1024


def _layer_norm_rows(y, g, b):
    mu = jnp.mean(y, axis=-1, keepdims=True)
    d = y - mu
    var = jnp.mean(d * d, axis=-1, keepdims=True)
    return d * lax.rsqrt(var + LN_EPS) * g + b


def _ffn_ln_kernel(x_ref, wa_ref, wb_ref, wo_ref, g_ref, b_ref, o_ref, xb_ref, acc_ref):
    k = pl.program_id(1)

    @pl.when(k == 0)
    def _():
        xb_ref[...] = x_ref[...].astype(BF16)
        acc_ref[...] = jnp.zeros_like(acc_ref)

    xb = xb_ref[...]
    a = jnp.dot(xb, wa_ref[...], preferred_element_type=F32)
    b = jnp.dot(xb, wb_ref[...], preferred_element_type=F32)
    h = (a * jax.nn.sigmoid(a)) * b
    acc_ref[...] += jnp.dot(h.astype(BF16), wo_ref[...], preferred_element_type=F32)

    @pl.when(k == pl.num_programs(1) - 1)
    def _():
        y = DEEPNORM_ALPHA * x_ref[...] + acc_ref[...]
        o_ref[...] = _layer_norm_rows(y, g_ref[...], b_ref[...])


def ffn_ln(x, w_in, w_out, g, b, *, tm, th=256):
    m, d = x.shape
    hidden = w_out.shape[0]
    nh = hidden // th
    return pl.pallas_call(
        _ffn_ln_kernel,
        grid=(m // tm, nh),
        in_specs=[
            pl.BlockSpec((tm, d), lambda i, k: (i, 0)),
            pl.BlockSpec((d, th), lambda i, k: (0, k)),
            pl.BlockSpec((d, th), lambda i, k: (0, k + nh)),
            pl.BlockSpec((th, d), lambda i, k: (k, 0)),
            pl.BlockSpec((1, d), lambda i, k: (0, 0)),
            pl.BlockSpec((1, d), lambda i, k: (0, 0)),
        ],
        out_specs=pl.BlockSpec((tm, d), lambda i, k: (i, 0)),
        out_shape=jax.ShapeDtypeStruct((m, d), F32),
        scratch_shapes=[pltpu.VMEM((tm, d), BF16), pltpu.VMEM((tm, d), F32)],
        compiler_params=pltpu.CompilerParams(
            dimension_semantics=("parallel", "arbitrary"), vmem_limit_bytes=VMEM_LIMIT_BYTES),
        name="ffn_ln",
    )(x, w_in, w_in, w_out, g.reshape(1, d), b.reshape(1, d))


def _proj_kernel(x_ref, w_ref, o_ref, xb_ref):
    @pl.when(pl.program_id(1) == 0)
    def _():
        xb_ref[...] = x_ref[...].astype(BF16)

    o_ref[...] = jnp.dot(xb_ref[...], w_ref[...], preferred_element_type=F32)


def proj(x, w, *, tm, tn):
    m, kd = x.shape
    n = w.shape[1]
    return pl.pallas_call(
        _proj_kernel,
        grid=(m // tm, n // tn),
        in_specs=[pl.BlockSpec((tm, kd), lambda i, j: (i, 0)), pl.BlockSpec((kd, tn), lambda i, j: (0, j))],
        out_specs=pl.BlockSpec((tm, tn), lambda i, j: (i, j)),
        out_shape=jax.ShapeDtypeStruct((m, n), F32),
        scratch_shapes=[pltpu.VMEM((tm, kd), BF16)],
        compiler_params=pltpu.CompilerParams(
            dimension_semantics=("parallel", "arbitrary"), vmem_limit_bytes=VMEM_LIMIT_BYTES),
        name="proj",
    )(x, w)


def _out_ln_kernel(h_ref, w_ref, x_ref, g_ref, b_ref, o_ref):
    y = DEEPNORM_ALPHA * x_ref[...] + jnp.dot(h_ref[...].astype(BF16), w_ref[...], preferred_element_type=F32)
    o_ref[...] = _layer_norm_rows(y, g_ref[...], b_ref[...])


def out_ln(h, w, x, g, b, *, tm):
    m, kd = h.shape
    d = w.shape[1]
    return pl.pallas_call(
        _out_ln_kernel,
        grid=(m // tm,),
        in_specs=[
            pl.BlockSpec((tm, kd), lambda i: (i, 0)),
            pl.BlockSpec((kd, d), lambda i: (0, 0)),
            pl.BlockSpec((tm, d), lambda i: (i, 0)),
            pl.BlockSpec((1, d), lambda i: (0, 0)),
            pl.BlockSpec((1, d), lambda i: (0, 0)),
        ],
        out_specs=pl.BlockSpec((tm, d), lambda i: (i, 0)),
        out_shape=jax.ShapeDtypeStruct((m, d), F32),
        compiler_params=pltpu.CompilerParams(
            dimension_semantics=("parallel",), vmem_limit_bytes=VMEM_LIMIT_BYTES),
        name="out_ln",
    )(h, w, x, g.reshape(1, d), b.reshape(1, d))


def _mem_block_kernel(x_ref, wq_ref, mk_ref, mv_ref, wo_ref, g_ref, b_ref, o_ref):
    heads = [slice(h * MEM_HEAD_DIM, (h + 1) * MEM_HEAD_DIM) for h in range(N_MEM_HEADS)]
    scale = MEM_HEAD_DIM ** -0.5
    x = x_ref[0]
    q = jnp.dot(x.astype(BF16), wq_ref[...], preferred_element_type=F32).astype(BF16)
    if mk_ref.ndim == 2:
        k_of = lambda h: mk_ref[:, heads[h]].astype(BF16)
        v_of = lambda h: mv_ref[:, heads[h]].astype(BF16)
        keep = lambda h, s: s
    else:
        rows = N_MEM * N_MEM_HEADS
        k_all = mk_ref[...].reshape(rows, MEM_HEAD_DIM).astype(BF16)
        v_all = mv_ref[...].reshape(rows, MEM_HEAD_DIM).astype(BF16)
        k_of = lambda h: k_all
        v_of = lambda h: v_all
        head_of_col = lax.broadcasted_iota(jnp.int32, (1, rows), 1) % N_MEM_HEADS
        keep = lambda h, s: jnp.where(head_of_col == h, s, NEG_BIG)
    s_all = [keep(h, lax.dot_general(q[:, sl], k_of(h), _NT, preferred_element_type=F32) * scale)
             for h, sl in enumerate(heads)]
    e_all = [jnp.exp(s - jnp.max(s, axis=1, keepdims=True)) for s in s_all]
    o_heads = []
    for h in range(N_MEM_HEADS):
        pv = jnp.dot(e_all[h].astype(BF16), v_of(h), preferred_element_type=F32)
        o_heads.append((pv / jnp.sum(e_all[h], axis=1, keepdims=True)).astype(BF16))
    o = jnp.concatenate(o_heads, axis=1)
    y = DEEPNORM_ALPHA * x + jnp.dot(o, wo_ref[...], preferred_element_type=F32)
    o_ref[0] = _layer_norm_rows(y, g_ref[...], b_ref[...])


def mem_block(x, w_q, mk, mv, layer, w_o, g, b, *, tm):
    bsz, t, d = x.shape
    const = lambda shape: pl.BlockSpec(shape, lambda b, i: (0,) * len(shape))
    if mk.ndim == 5:
        mk_spec = mv_spec = pl.BlockSpec((None, None, N_MEM, N_MEM_HEADS, MEM_HEAD_DIM),
                                         lambda b, i: (layer, b, 0, 0, 0))
    else:
        mk_spec = pl.BlockSpec((None, N_MEM, d), lambda b, i: (layer, b, 0))
        mv_spec = pl.BlockSpec((None, N_MEM, d), lambda b, i: (layer, b, 1))
    return pl.pallas_call(
        _mem_block_kernel,
        grid=(bsz, t // tm),
        in_specs=[pl.BlockSpec((1, tm, d), lambda b, i: (b, i, 0)), const((d, d)), mk_spec, mv_spec,
                  const((d, d)), const((1, d)), const((1, d))],
        out_specs=pl.BlockSpec((1, tm, d), lambda b, i: (b, i, 0)),
        out_shape=jax.ShapeDtypeStruct((bsz, t, d), F32),
        compiler_params=pltpu.CompilerParams(
            dimension_semantics=("parallel", "arbitrary"), vmem_limit_bytes=VMEM_LIMIT_BYTES),
        name="mem_block",
    )(x, w_q, mk, mv, w_o, g.reshape(1, d), b.reshape(1, d))


CONV_COLS = 1024


def _conv_kernel(x_ref, st_ref, w_ref, b_ref, act_ref, last_ref, prev_ref, *, tc):
    @pl.when(pl.program_id(2) == 0)
    def _():
        prev_ref[...] = st_ref[0]

    x = x_ref[0]
    prev = prev_ref[...]
    row = lax.broadcasted_iota(jnp.int32, (8, CONV_COLS), 0)
    acc = jnp.broadcast_to(b_ref[...], x.shape)
    for s in (3, 2, 1):
        rolled = pltpu.roll(x, s, axis=0)
        top = jnp.where(row < s, pltpu.roll(prev, s, axis=0), rolled[0:8])
        shifted = top if tc == 8 else jnp.concatenate([top, rolled[8:]], axis=0)
        acc = acc + shifted * w_ref[CONV_W - 1 - s:CONV_W - s, :]
    acc = acc + x * w_ref[CONV_W - 1:CONV_W, :]
    act_ref[0] = acc * jax.nn.sigmoid(acc)
    prev_ref[...] = x[tc - 8:tc]
    last_ref[0] = x[tc - 8:tc]


def ssm_conv(zx, conv_state, conv_w, conv_b, *, tc):
    bsz, t = zx.shape[:2]
    ncb = CONV_DIM // CONV_COLS
    col0 = D_INNER // CONV_COLS
    st8 = jnp.pad(conv_state, ((0, 0), (8 - (CONV_W - 1), 0), (0, 0)))
    act, last = pl.pallas_call(
        functools.partial(_conv_kernel, tc=tc),
        grid=(bsz, ncb, t // tc),
        in_specs=[
            pl.BlockSpec((1, tc, CONV_COLS), lambda b, c, i: (b, i, col0 + c)),
            pl.BlockSpec((1, 8, CONV_COLS), lambda b, c, i: (b, 0, c)),
            pl.BlockSpec((CONV_W, CONV_COLS), lambda b, c, i: (0, c)),
            pl.BlockSpec((1, CONV_COLS), lambda b, c, i: (0, c)),
        ],
        out_specs=[
            pl.BlockSpec((1, tc, CONV_COLS), lambda b, c, i: (b, i, c)),
            pl.BlockSpec((1, 8, CONV_COLS), lambda b, c, i: (b, 0, c)),
        ],
        out_shape=[jax.ShapeDtypeStruct((bsz, t, CONV_DIM), F32), jax.ShapeDtypeStruct((bsz, 8, CONV_DIM), F32)],
        scratch_shapes=[pltpu.VMEM((8, CONV_COLS), F32)],
        compiler_params=pltpu.CompilerParams(
            dimension_semantics=("parallel", "parallel", "arbitrary"), vmem_limit_bytes=VMEM_LIMIT_BYTES),
        name="ssm_conv",
    )(zx, st8, conv_w, conv_b.reshape(1, CONV_DIM))
    return act, last[:, 8 - (CONV_W - 1):, :]


def _split3(v):
    hi = v.astype(BF16)
    r1 = v - hi.astype(F32)
    mid = r1.astype(BF16)
    lo = (r1 - mid.astype(F32)).astype(BF16)
    return hi, mid, lo


def _dot01_right(v, ones_mat):
    return sum(jnp.dot(p, ones_mat, preferred_element_type=F32) for p in _split3(v))


def _dot01_left(ones_mat, v):
    return sum(jnp.dot(ones_mat, p, preferred_element_type=F32) for p in _split3(v))


def _softplus(x):
    return jnp.maximum(x, 0.0) + jnp.log1p(jnp.exp(-jnp.abs(x)))


def _ssd_kernel(xs_ref, b_ref, c_ref, z_ref, dt_ref, dtt_ref, dtb_ref, dtbt_ref, alog_ref, alogt_ref,
                d_ref, nw_ref, h0_ref, y_ref, hT_ref, st_ref, yacc_ref, *, c):
    ci = pl.program_id(1)

    @pl.when(ci == 0)
    def _():
        st_ref[...] = h0_ref[0]

    gw = HEADS_PER_GROUP * SSM_HEAD_DIM
    dt = _softplus(dt_ref[0][:, 0:N_SSM_HEADS] + dtb_ref[...])
    dtt = _softplus(dtt_ref[0] + dtbt_ref[...])
    a = dt * -jnp.exp(alog_ref[...])
    at = dtt * -jnp.exp(alogt_ref[...])
    ri = lax.broadcasted_iota(jnp.int32, (c, c), 0)
    cj = lax.broadcasted_iota(jnp.int32, (c, c), 1)
    causal = ri >= cj
    acum = _dot01_left(jnp.where(causal, 1.0, 0.0).astype(BF16), a)
    acum_t = _dot01_right(at, jnp.where(ri <= cj, 1.0, 0.0).astype(BF16))
    a_last = acum[c - 1:c, :]
    expand = jnp.where(lax.broadcasted_iota(jnp.int32, (N_SSM_HEADS, D_INNER), 1) // SSM_HEAD_DIM
                       == lax.broadcasted_iota(jnp.int32, (N_SSM_HEADS, D_INNER), 0), 1.0, 0.0).astype(BF16)
    e_dt = _dot01_right(dt, expand)
    e_in = _dot01_right(jnp.exp(acum), expand)
    e_out = _dot01_right(jnp.exp(a_last - acum), expand)
    e_chunk = e_in[c - 1:c, :]
    xs = xs_ref[0]
    xdt = xs * e_dt
    head_of_col = lax.broadcasted_iota(jnp.int32, (c, gw), 1) // SSM_HEAD_DIM
    for g in range(SSM_GROUPS):
        gs = slice(g * gw, (g + 1) * gw)
        bg = b_ref[0][:, g * D_STATE:(g + 1) * D_STATE].astype(BF16)
        cg = c_ref[0][:, g * D_STATE:(g + 1) * D_STATE].astype(BF16)
        cb = lax.dot_general(cg, bg, _NT, preferred_element_type=F32)
        xg = xdt[:, gs]
        yg = jnp.zeros((c, gw), F32)
        for r in range(HEADS_PER_GROUP):
            h = g * HEADS_PER_GROUP + r
            seg = acum[:, h:h + 1] - acum_t[h:h + 1, :]
            m = (cb * jnp.exp(jnp.where(causal, seg, -jnp.inf))).astype(BF16)
            xm = jnp.where(head_of_col == r, xg, 0.0).astype(BF16)
            yg = yg + jnp.dot(m, xm, preferred_element_type=F32)
        state = st_ref[g]
        y_off = jnp.dot(cg, state.astype(BF16), preferred_element_type=F32) * e_in[:, gs]
        yacc_ref[:, gs] = yg + y_off + xs[:, gs] * d_ref[:, gs]
        xd = (xg * e_out[:, gs]).astype(BF16)
        st_ref[g] = state * e_chunk[:, gs] + lax.dot_general(bg, xd, (((0,), (0,)), ((), ())),
                                                             preferred_element_type=F32)
    z = z_ref[0]
    yz = yacc_ref[...] * (z * jax.nn.sigmoid(z))
    y_ref[0] = yz * lax.rsqrt(jnp.mean(yz * yz, axis=-1, keepdims=True) + LN_EPS) * nw_ref[...]

    @pl.when(ci == pl.num_programs(1) - 1)
    def _():
        hT_ref[0] = st_ref[...]


def ssd_scan(act, zx, dt_raw, dt_bias, a_log, d_skip, norm_w, ssm_state, *, c):
    bsz, t = act.shape[:2]
    gw = HEADS_PER_GROUP * SSM_HEAD_DIM
    bc_w = SSM_GROUPS * D_STATE
    h0 = ssm_state.reshape(bsz, SSM_GROUPS, HEADS_PER_GROUP, SSM_HEAD_DIM, D_STATE)
    h0 = h0.transpose(0, 1, 4, 2, 3).reshape(bsz, SSM_GROUPS, D_STATE, gw)
    dtt = jnp.swapaxes(dt_raw[..., :N_SSM_HEADS], 1, 2)
    row = lambda v: v.reshape(1, -1)
    col = lambda v: v.reshape(-1, 1)
    full = lambda shape: pl.BlockSpec(shape, lambda b, i: (0,) * len(shape))
    y, h_t = pl.pallas_call(
        functools.partial(_ssd_kernel, c=c),
        grid=(bsz, t // c),
        in_specs=[
            pl.BlockSpec((1, c, D_INNER), lambda b, i: (b, i, 0)),
            pl.BlockSpec((1, c, bc_w), lambda b, i: (b, i, D_INNER // bc_w)),
            pl.BlockSpec((1, c, bc_w), lambda b, i: (b, i, D_INNER // bc_w + 1)),
            pl.BlockSpec((1, c, D_INNER), lambda b, i: (b, i, 0)),
            pl.BlockSpec((1, c, 128), lambda b, i: (b, i, 0)),
            pl.BlockSpec((1, N_SSM_HEADS, c), lambda b, i: (b, 0, i)),
            full((1, N_SSM_HEADS)), full((N_SSM_HEADS, 1)), full((1, N_SSM_HEADS)), full((N_SSM_HEADS, 1)),
            full((1, D_INNER)), full((1, D_INNER)),
            pl.BlockSpec((1, SSM_GROUPS, D_STATE, gw), lambda b, i: (b, 0, 0, 0)),
        ],
        out_specs=[
            pl.BlockSpec((1, c, D_INNER), lambda b, i: (b, i, 0)),
            pl.BlockSpec((1, SSM_GROUPS, D_STATE, gw), lambda b, i: (b, 0, 0, 0)),
        ],
        out_shape=[jax.ShapeDtypeStruct((bsz, t, D_INNER), F32),
                   jax.ShapeDtypeStruct((bsz, SSM_GROUPS, D_STATE, gw), F32)],
        scratch_shapes=[pltpu.VMEM((SSM_GROUPS, D_STATE, gw), F32), pltpu.VMEM((c, D_INNER), F32)],
        compiler_params=pltpu.CompilerParams(
            dimension_semantics=("parallel", "arbitrary"), vmem_limit_bytes=VMEM_LIMIT_BYTES),
        name="ssd_scan",
    )(act, act, act, zx, dt_raw, dtt, row(dt_bias), col(dt_bias), row(a_log), col(a_log),
      row(jnp.repeat(d_skip, SSM_HEAD_DIM)), row(norm_w), h0)
    h_t = h_t.reshape(bsz, SSM_GROUPS, D_STATE, HEADS_PER_GROUP, SSM_HEAD_DIM)
    return y, h_t.transpose(0, 1, 3, 4, 2).reshape(bsz, N_SSM_HEADS, SSM_HEAD_DIM, D_STATE)


INT_MIN = -2 ** 31
NEG_BIG = -1e30
_NT = (((1,), (1,)), ((), ()))


def _float_key(x):
    x = jnp.where(x == 0.0, 0.0, x)
    bits = lax.bitcast_convert_type(x, jnp.int32)
    return bits ^ ((bits >> 31) & 0x7FFFFFFF)


def _dsa_prompt_kernel(qt_ref, qit_ref, wit_ref, k_ref, vxt_ref, ki_ref, o_ref, key_ref, m_ref, acc_ref,
                         s_ref, p_ref, *, tq, tk, topk):
    i = pl.program_id(1)
    nkb = ((i + 1) * tq + tk - 1) // tk
    gq = KV_GROUP * tq
    rep8 = lambda v: jnp.broadcast_to(v, (8, tq))
    as3 = lambda x: x.reshape(tk // 8, 8, tq)
    w_heads = wit_ref[0, 0] * (N_IDX_HEADS ** -0.5)
    q_pos = i * tq + lax.broadcasted_iota(jnp.int32, (tk, tq), 1)

    def score_block(kb, c):
        off = pl.multiple_of(kb * tk, tk)
        s = jnp.dot(ki_ref[0, pl.ds(off, tk), :], qit_ref[0, 0], preferred_element_type=F32)
        acc = jnp.zeros((tk, tq), F32)
        for h in range(N_IDX_HEADS):
            acc = acc + w_heads[h:h + 1, :] * jnp.maximum(s[:, h * tq:(h + 1) * tq], 0.0)
        kpos = off + lax.broadcasted_iota(jnp.int32, (tk, tq), 0)
        key_ref[kb] = jnp.where(kpos <= q_pos, _float_key(acc), INT_MIN)
        return c

    lax.fori_loop(0, nkb, score_block, 0)

    def count(cand, strict):
        def body(kb, accs):
            blk = key_ref[kb]
            accs = list(accs)
            for r in range(tk // 8):
                part = blk[r * 8:(r + 1) * 8, :]
                hit = (part > cand) if strict else (part >= cand)
                accs[r % 4] = accs[r % 4] + jnp.where(hit, 1.0, 0.0)
            return tuple(accs)
        a0, a1, a2, a3 = lax.fori_loop(0, nkb, body, (jnp.zeros((8, tq), F32),) * 4)
        return jnp.sum((a0 + a1) + (a2 + a3), axis=0, keepdims=True)

    kf = float(topk)
    zero = jnp.zeros((8, tq), jnp.int32)
    thr = jnp.where(rep8(count(zero, False)) >= kf, zero, jnp.full((8, tq), INT_MIN, jnp.int32))

    def bit_step(it, thr):
        cand = thr | jnp.left_shift(jnp.int32(1), 30 - it)
        return jnp.where(rep8(count(cand, False)) >= kf, cand, thr)

    thr = lax.fori_loop(0, 31, bit_step, thr)
    need = jnp.where(thr == INT_MIN, 0.0, kf - rep8(count(thr, True)))

    m_ref[...] = jnp.full(m_ref.shape, NEG_BIG, F32)
    acc_ref[...] = jnp.zeros(acc_ref.shape, F32)
    lower = jnp.where(lax.broadcasted_iota(jnp.int32, (tk, tk), 1) < lax.broadcasted_iota(jnp.int32, (tk, tk), 0),
                      1.0, 0.0).astype(BF16)

    def attend_block(kb, eq_before):
        off = pl.multiple_of(kb * tk, tk)
        key3 = as3(key_ref[kb])
        eq3 = key3 == thr[None]
        eqf = jnp.where(eq3, 1.0, 0.0)
        rank3 = as3(jnp.dot(lower, eqf.reshape(tk, tq).astype(BF16), preferred_element_type=F32)) + eq_before[None]
        bias = jnp.where(key3 > thr[None], 0.0,
                         jnp.where(eq3, jnp.where(rank3 < need[None], 0.0, NEG_BIG), NEG_BIG)).reshape(tk, tq)
        bias4 = jnp.concatenate([bias] * KV_GROUP, axis=1)
        col_max = []
        for j in range(N_KV_HEADS):
            kblk = k_ref[0, pl.ds(off, tk), j * 64:(j + 1) * 64]
            s = jnp.dot(kblk, qt_ref[0, 0, j], preferred_element_type=F32) + bias4
            s_ref[j] = s
            col_max.append(jnp.max(s, axis=0, keepdims=True))
        alpha = []
        for j in range(N_KV_HEADS):
            m_old = m_ref[j]
            m_new = jnp.maximum(m_old, col_max[j])
            alpha.append(jnp.exp(m_old - m_new))
            p_ref[j] = jnp.exp(s_ref[j] - m_new).astype(BF16)
            m_ref[j] = m_new
        for j in range(N_KV_HEADS):
            pv = jnp.dot(vxt_ref[0, j, kb], p_ref[j], preferred_element_type=F32)
            acc_ref[j] = alpha[j] * acc_ref[j] + pv
        return eq_before + rep8(jnp.sum(jnp.sum(eqf, axis=0), axis=0, keepdims=True))

    lax.fori_loop(0, nkb, attend_block, jnp.zeros((8, tq), F32))

    for j in range(N_KV_HEADS):
        a = acc_ref[j]
        o_t = a[0:64, :] / a[64:65, :]
        for g in range(KV_GROUP):
            h = j * KV_GROUP + g
            o_ref[0, :, h * 64:(h + 1) * 64] = o_t[:, g * tq:(g + 1) * tq].T


def dsa_prompt_attend(q, k, v, qi, ki, wi, *, tq, tk):
    bsz, t = q.shape[:2]
    nq, nk = t // tq, t // tk
    topk = min(TOPK_MAX, t // 4)
    qt = (q * ATT_HEAD_DIM ** -0.5).astype(BF16).reshape(bsz, nq, tq, N_KV_HEADS, KV_GROUP, 64)
    qt = qt.transpose(0, 1, 3, 5, 4, 2).reshape(bsz, nq, N_KV_HEADS, 64, KV_GROUP * tq)
    qit = (qi * IDX_DIM ** -0.5).astype(BF16).reshape(bsz, nq, tq, N_IDX_HEADS, 64)
    qit = qit.transpose(0, 1, 4, 3, 2).reshape(bsz, nq, 64, N_IDX_HEADS * tq)
    wit = wi.reshape(bsz, nq, tq, N_IDX_HEADS).transpose(0, 1, 3, 2)
    vt = v.astype(BF16).reshape(bsz, nk, tk, N_KV_HEADS, 64).transpose(0, 3, 1, 4, 2)
    vxt = jnp.concatenate([vt, jnp.ones_like(vt)], axis=3)
    kern = functools.partial(_dsa_prompt_kernel, tq=tq, tk=tk, topk=topk)
    return pl.pallas_call(
        kern,
        grid=(bsz, nq),
        in_specs=[
            pl.BlockSpec((1, 1, N_KV_HEADS, 64, KV_GROUP * tq), lambda b, i: (b, i, 0, 0, 0)),
            pl.BlockSpec((1, 1, 64, N_IDX_HEADS * tq), lambda b, i: (b, i, 0, 0)),
            pl.BlockSpec((1, 1, N_IDX_HEADS, tq), lambda b, i: (b, i, 0, 0)),
            pl.BlockSpec((1, t, 256), lambda b, i: (b, 0, 0), pipeline_mode=pl.Buffered(1)),
            pl.BlockSpec((1, N_KV_HEADS, nk, 128, tk), lambda b, i: (b, 0, 0, 0, 0), pipeline_mode=pl.Buffered(1)),
            pl.BlockSpec((1, t, IDX_DIM), lambda b, i: (b, 0, 0), pipeline_mode=pl.Buffered(1)),
        ],
        out_specs=pl.BlockSpec((1, tq, 1024), lambda b, i: (b, i, 0)),
        out_shape=jax.ShapeDtypeStruct((bsz, t, 1024), F32),
        scratch_shapes=[
            pltpu.VMEM((nk, tk, tq), jnp.int32),
            pltpu.VMEM((N_KV_HEADS, 1, KV_GROUP * tq), F32),
            pltpu.VMEM((N_KV_HEADS, 128, KV_GROUP * tq), F32),
            pltpu.VMEM((N_KV_HEADS, tk, KV_GROUP * tq), F32),
            pltpu.VMEM((N_KV_HEADS, tk, KV_GROUP * tq), BF16),
        ],
        compiler_params=pltpu.CompilerParams(
            dimension_semantics=("parallel", "arbitrary"), vmem_limit_bytes=VMEM_LIMIT_BYTES),
        name="dsa_prompt_attend",
    )(qt, qit, wit, k.astype(BF16), vxt, ki.astype(BF16))


PAGES_PER_STEP = 8


def _sample_select_kernel(pt_ref, qi_ref, wi_ref, kinew_ref, *rest, t, n_steps, topk):
    ki_refs = rest[:PAGES_PER_STEP]
    bias_ref, qis_ref, wib_ref, key_ref = rest[PAGES_PER_STEP:]
    p = pl.program_id(1)
    w = PAGES_PER_STEP * PAGE_SIZE
    nl = w // 128

    @pl.when(p == 0)
    def _():
        for h in range(N_IDX_HEADS):
            qis_ref[h * t:(h + 1) * t, :] = qi_ref[0, :, h * 64:(h + 1) * 64]
            wib_ref[h] = jnp.broadcast_to(wi_ref[0, :, h:h + 1] * (N_IDX_HEADS ** -0.5), (t, 128))

    def scores(s):
        acc = jnp.zeros((t, s.shape[1]), F32)
        for h in range(N_IDX_HEADS):
            wfull = jnp.concatenate([wib_ref[h]] * (s.shape[1] // 128), axis=1)
            acc = acc + wfull * jnp.maximum(s[h * t:(h + 1) * t, :], 0.0)
        return acc

    ki_t = jnp.concatenate([r[...] for r in ki_refs], axis=1).astype(BF16)
    key_ref[p] = _float_key(scores(jnp.dot(qis_ref[...].astype(BF16), ki_t, preferred_element_type=F32)))

    @pl.when(p == n_steps - 1)
    def _():
        knew = _float_key(scores(lax.dot_general(qis_ref[...].astype(BF16), kinew_ref[0].astype(BF16), _NT,
                                                 preferred_element_type=F32)))
        n_idx = lax.broadcasted_iota(jnp.int32, (t, 128), 1)
        r_idx = lax.broadcasted_iota(jnp.int32, (t, 128), 0)
        knew = jnp.where(n_idx <= r_idx, knew, INT_MIN)
        key_ref[n_steps] = jnp.concatenate([knew, jnp.full((t, w - 128), INT_MIN, jnp.int32)], axis=1)

        def count(cand, strict):
            def body(blk, acc):
                kb = key_ref[blk]
                for c in range(nl):
                    part = kb[:, c * 128:(c + 1) * 128]
                    hit = (part > cand) if strict else (part >= cand)
                    acc = acc + jnp.where(hit, 1.0, 0.0)
                return acc
            acc = lax.fori_loop(0, n_steps + 1, body, jnp.zeros((t, 128), F32))
            return jnp.sum(acc, axis=1, keepdims=True)

        kf = float(topk)
        zero = jnp.zeros((t, 128), jnp.int32)
        thr = jnp.where(count(zero, False) >= kf, zero, jnp.full((t, 128), INT_MIN, jnp.int32))

        def bit_step(it, thr):
            cand = thr | jnp.left_shift(jnp.int32(1), 30 - it)
            return jnp.where(count(cand, False) >= kf, cand, thr)

        thr = lax.fori_loop(0, 31, bit_step, thr)
        need = jnp.where(thr == INT_MIN, 0.0, kf - count(thr, True))
        floor_thr = jnp.where(thr == INT_MIN, INT_MIN + 1, thr)
        untied = jnp.where((thr == INT_MIN) | (count(thr, False) <= kf), 1.0, 0.0)
        no_ties = jnp.min(untied) > 0.0

        @pl.when(no_ties)
        def _():
            wide = jnp.concatenate([floor_thr] * nl, axis=1)

            def emit_simple(blk, c):
                bias_ref[0, blk] = jnp.where(key_ref[blk] >= wide, 0.0, NEG_BIG)
                return c

            lax.fori_loop(0, n_steps + 1, emit_simple, 0)

        upper = jnp.where(lax.broadcasted_iota(jnp.int32, (128, 128), 0)
                          < lax.broadcasted_iota(jnp.int32, (128, 128), 1), 1.0, 0.0).astype(BF16)

        def emit(blk, eq_before):
            kb = key_ref[blk]
            parts = []
            for c in range(nl):
                part = kb[:, c * 128:(c + 1) * 128]
                eq = part == thr
                eqf = jnp.where(eq, 1.0, 0.0)
                rank = jnp.dot(eqf.astype(BF16), upper, preferred_element_type=F32) + eq_before
                parts.append(jnp.where(part > thr, 0.0,
                                       jnp.where(eq, jnp.where(rank < need, 0.0, NEG_BIG), NEG_BIG)))
                eq_before = eq_before + jnp.sum(eqf, axis=1, keepdims=True)
            bias_ref[0, blk] = jnp.concatenate(parts, axis=1)
            return eq_before

        @pl.when(jnp.logical_not(no_ties))
        def _():
            lax.fori_loop(0, n_steps + 1, emit, jnp.zeros((t, 128), F32))


def _sample_attend_kernel(pt_ref, q_ref, bias_ref, biasnew_ref, knew_ref, vnew_ref, *rest, t, n_steps):
    k_refs = rest[:PAGES_PER_STEP]
    v_refs = rest[PAGES_PER_STEP:2 * PAGES_PER_STEP]
    o_ref, qs_ref, m_ref, l_ref, acc_ref = rest[2 * PAGES_PER_STEP:]
    p = pl.program_id(1)
    rows = KV_GROUP * t

    @pl.when(p == 0)
    def _():
        for h in range(N_ATT_HEADS):
            j, g = divmod(h, KV_GROUP)
            qs_ref[j, g * t:(g + 1) * t, :] = q_ref[0, :, h * 64:(h + 1) * 64]
        m_ref[...] = jnp.full(m_ref.shape, NEG_BIG, F32)
        l_ref[...] = jnp.zeros(l_ref.shape, F32)
        acc_ref[...] = jnp.zeros(acc_ref.shape, F32)

    def update(k_t, v_t, bias):
        bias4 = jnp.concatenate([bias] * KV_GROUP, axis=0)
        s_all = [jnp.dot(qs_ref[j].astype(BF16), k_t(j).astype(BF16), preferred_element_type=F32) + bias4
                 for j in range(N_KV_HEADS)]
        alphas, probs = [], []
        for j in range(N_KV_HEADS):
            m_old = m_ref[j]
            m_new = jnp.maximum(m_old, jnp.max(s_all[j], axis=1, keepdims=True))
            alpha = jnp.exp(m_old - m_new)
            pr = jnp.exp(s_all[j] - m_new[:, 0:1])
            l_ref[j] = alpha * l_ref[j] + jnp.sum(pr, axis=1, keepdims=True)
            m_ref[j] = m_new
            alphas.append(alpha)
            probs.append(pr.astype(BF16))
        for j in range(N_KV_HEADS):
            pv = lax.dot_general(probs[j], v_t(j).astype(BF16), _NT, preferred_element_type=F32)
            acc_ref[j] = alphas[j][:, 0:64] * acc_ref[j] + pv

    update(lambda j: jnp.concatenate([r[j] for r in k_refs], axis=1),
           lambda j: jnp.concatenate([r[j] for r in v_refs], axis=1), bias_ref[0, 0])

    @pl.when(p == n_steps - 1)
    def _():
        update(lambda j: knew_ref[0, j * 64:(j + 1) * 64, :], lambda j: vnew_ref[0, j * 64:(j + 1) * 64, :],
               biasnew_ref[0, 0, :, 0:128])
        for h in range(N_ATT_HEADS):
            j, g = divmod(h, KV_GROUP)
            sl = slice(g * t, (g + 1) * t)
            o_ref[0, :, h * 64:(h + 1) * 64] = acc_ref[j, sl, :] / l_ref[j, sl, 0:64]


def dsa_sample_attend(q, k, v, qi, ki, wi, cache_k, cache_v, cache_kidx, layer, page_table):
    bsz, t = q.shape[:2]
    n_pages = page_table.shape[1]
    past = n_pages * PAGE_SIZE
    n_steps = n_pages // PAGES_PER_STEP
    w = PAGES_PER_STEP * PAGE_SIZE
    topk = min(TOPK_MAX, (past + t) // 4)
    pad = lambda a: jnp.pad(a, ((0, 0), (0, PAGE_SIZE - t), (0, 0)))
    pool_k = cache_k.transpose(0, 1, 3, 4, 2)
    pool_v = cache_v.transpose(0, 1, 3, 4, 2)
    pool_ki = cache_kidx.transpose(0, 1, 3, 2)
    knew_t = jnp.swapaxes(pad(k), 1, 2)
    vnew_t = jnp.swapaxes(pad(v), 1, 2)

    def page_spec(r, *major):
        zeros = (0,) * (1 + len(major))
        return pl.BlockSpec((None, None) + major + (PAGE_SIZE,),
                            lambda b, p, pt: (layer, pt[b, p * PAGES_PER_STEP + r]) + zeros)

    bias = pl.pallas_call(
        functools.partial(_sample_select_kernel, t=t, n_steps=n_steps, topk=topk),
        grid_spec=pltpu.PrefetchScalarGridSpec(
            num_scalar_prefetch=1,
            grid=(bsz, n_steps),
            in_specs=[
                pl.BlockSpec((1, t, 512), lambda b, p, pt: (b, 0, 0)),
                pl.BlockSpec((1, t, N_IDX_HEADS), lambda b, p, pt: (b, 0, 0)),
                pl.BlockSpec((1, PAGE_SIZE, IDX_DIM), lambda b, p, pt: (b, 0, 0)),
            ] + [page_spec(r, IDX_DIM) for r in range(PAGES_PER_STEP)],
            out_specs=pl.BlockSpec((1, n_steps + 1, t, w), lambda b, p, pt: (b, 0, 0, 0)),
            scratch_shapes=[
                pltpu.VMEM((N_IDX_HEADS * t, 64), F32),
                pltpu.VMEM((N_IDX_HEADS, t, 128), F32),
                pltpu.VMEM((n_steps + 1, t, w), jnp.int32),
            ]),
        out_shape=jax.ShapeDtypeStruct((bsz, n_steps + 1, t, w), F32),
        compiler_params=pltpu.CompilerParams(
            dimension_semantics=("parallel", "arbitrary"), vmem_limit_bytes=VMEM_LIMIT_BYTES),
        name="dsa_sample_select",
    )(page_table, qi * IDX_DIM ** -0.5, wi, pad(ki), *([pool_ki] * PAGES_PER_STEP))

    return pl.pallas_call(
        functools.partial(_sample_attend_kernel, t=t, n_steps=n_steps),
        grid_spec=pltpu.PrefetchScalarGridSpec(
            num_scalar_prefetch=1,
            grid=(bsz, n_steps),
            in_specs=[
                pl.BlockSpec((1, t, 1024), lambda b, p, pt: (b, 0, 0)),
                pl.BlockSpec((1, 1, t, w), lambda b, p, pt: (b, p, 0, 0)),
                pl.BlockSpec((1, 1, t, w), lambda b, p, pt: (b, n_steps, 0, 0)),
                pl.BlockSpec((1, 256, PAGE_SIZE), lambda b, p, pt: (b, 0, 0)),
                pl.BlockSpec((1, 256, PAGE_SIZE), lambda b, p, pt: (b, 0, 0)),
            ] + [page_spec(r, N_KV_HEADS, ATT_HEAD_DIM) for r in range(PAGES_PER_STEP)] * 2,
            out_specs=pl.BlockSpec((1, t, 1024), lambda b, p, pt: (b, 0, 0)),
            scratch_shapes=[
                pltpu.VMEM((N_KV_HEADS, KV_GROUP * t, 64), F32),
                pltpu.VMEM((N_KV_HEADS, KV_GROUP * t, 128), F32),
                pltpu.VMEM((N_KV_HEADS, KV_GROUP * t, 128), F32),
                pltpu.VMEM((N_KV_HEADS, KV_GROUP * t, 64), F32),
            ]),
        out_shape=jax.ShapeDtypeStruct((bsz, t, 1024), F32),
        compiler_params=pltpu.CompilerParams(
            dimension_semantics=("parallel", "arbitrary"), vmem_limit_bytes=VMEM_LIMIT_BYTES),
        name="dsa_sample_attend",
    )(page_table, q * ATT_HEAD_DIM ** -0.5, bias, bias, knew_t, vnew_t,
      *([pool_k] * PAGES_PER_STEP), *([pool_v] * PAGES_PER_STEP))


def dsa_project(x, w_main, w_tail, pos, tm):
    bsz, t = x.shape[:2]
    x2 = x.reshape(bsz * t, D_MODEL)
    main = proj(x2, w_main, tm=tm, tn=1024).reshape(bsz, t, -1)
    tail = proj(x2, w_tail, tm=tm, tn=128).reshape(bsz, t, -1)
    main, tail = rope_apply(main, tail, pos, tr=min(t, 512))
    q, k, v, qi = jnp.split(main, [1024, 1280, 1536], axis=-1)
    ki, wi = tail[..., :IDX_DIM], tail[..., IDX_DIM:IDX_DIM + N_IDX_HEADS]
    return (q, k.reshape(bsz, t, N_KV_HEADS, ATT_HEAD_DIM), v.reshape(bsz, t, N_KV_HEADS, ATT_HEAD_DIM), qi, ki, wi)


ROPE_V_BLOCKS = (10, 11)


def _rope_kernel(main_ref, tail_ref, cos_ref, sin_ref, mo_ref, to_ref):
    cos, sin = cos_ref[...], sin_ref[...]
    lane = lax.broadcasted_iota(jnp.int32, cos.shape, 1)
    first_half = lane % ATT_HEAD_DIM < ATT_HEAD_DIM // 2

    def rot(xb):
        partner = jnp.where(first_half, -pltpu.roll(xb, 128 - 32, axis=1), pltpu.roll(xb, 32, axis=1))
        return xb * cos + partner * sin

    for cb in range(main_ref.shape[2] // 128):
        xb = main_ref[0, :, cb * 128:(cb + 1) * 128]
        mo_ref[0, :, cb * 128:(cb + 1) * 128] = xb if cb in ROPE_V_BLOCKS else rot(xb)
    tb = tail_ref[0]
    to_ref[0] = jnp.where(lane < IDX_DIM, rot(tb), tb)


def rope_apply(main, tail, pos, *, tr):
    bsz, t, wm = main.shape
    inv_freq = ROPE_THETA ** (-jnp.arange(0, ATT_HEAD_DIM, 2, dtype=F32) / ATT_HEAD_DIM)
    ang = pos.astype(F32)[:, None] * inv_freq[None, :]
    cos = jnp.tile(jnp.cos(ang), (1, 4))
    sin = jnp.tile(jnp.sin(ang), (1, 4))
    return pl.pallas_call(
        _rope_kernel,
        grid=(bsz, t // tr),
        in_specs=[
            pl.BlockSpec((1, tr, wm), lambda b, i: (b, i, 0)),
            pl.BlockSpec((1, tr, 128), lambda b, i: (b, i, 0)),
            pl.BlockSpec((tr, 128), lambda b, i: (i, 0)),
            pl.BlockSpec((tr, 128), lambda b, i: (i, 0)),
        ],
        out_specs=[pl.BlockSpec((1, tr, wm), lambda b, i: (b, i, 0)), pl.BlockSpec((1, tr, 128), lambda b, i: (b, i, 0))],
        out_shape=[jax.ShapeDtypeStruct(main.shape, F32), jax.ShapeDtypeStruct(tail.shape, F32)],
        compiler_params=pltpu.CompilerParams(
            dimension_semantics=("parallel", "parallel"), vmem_limit_bytes=VMEM_LIMIT_BYTES),
        name="rope",
    )(main, tail, cos, sin)


def run_trunk(x, mem_k, mem_v, conv0, ssm0, attend, pos, p, *, tm, tm_mem):
    conv_out, ssm_out, k_out, v_out, ki_out = [], [], [], [], []
    bsz, t = x.shape[:2]
    m = bsz * t
    x = x.reshape(m, D_MODEL)
    for i in range(DEPTH):
        j = i // N_MIXERS
        g, b = p['ln_g'][i], p['ln_b'][i]
        if i % N_MIXERS == 0:
            zx = proj(x, p['w_ssm_main'][j], tm=tm, tn=1024).reshape(bsz, t, -1)
            dt_raw = proj(x, p['w_ssm_dt'][j], tm=tm, tn=128).reshape(bsz, t, -1)
            act, cs = ssm_conv(zx, conv0[j], p['ssm_conv_w'][j], p['ssm_conv_b'][j], tc=min(t, 512))
            h, ss = ssd_scan(act, zx, dt_raw, p['ssm_dt_bias'][j], p['ssm_a_log'][j], p['ssm_d'][j],
                             p['ssm_norm_w'][j], ssm0[j], c=SSM_CHUNK if t % SSM_CHUNK == 0 else t)
            conv_out.append(cs)
            ssm_out.append(ss)
            w_o = p['w_ssm_out'][j]
        else:
            q, k, v, qi, ki, wi = dsa_project(x.reshape(bsz, t, D_MODEL), p['w_att_main'][j], p['w_att_tail'][j],
                                              pos, tm)
            h = attend(j, q.reshape(bsz, t, -1), k.reshape(bsz, t, -1), v.reshape(bsz, t, -1),
                       qi.reshape(bsz, t, -1), ki, wi)
            k_out.append(k)
            v_out.append(v)
            ki_out.append(ki)
            w_o = p['w_att_out'][j]
        x = out_ln(h.reshape(m, -1), w_o, x, g[0], b[0], tm=tm_mem)
        x = mem_block(x.reshape(bsz, t, D_MODEL), p['w_mem_q'][i], mem_k, mem_v, i, p['w_mem_out'][i], g[1], b[1],
                      tm=min(t, tm_mem)).reshape(m, D_MODEL)
        x = ffn_ln(x, p['w_ffn_in'][i], p['w_ffn_out'][i], g[2], b[2], tm=tm)
    return (x.reshape(bsz, t, D_MODEL), jnp.stack(k_out), jnp.stack(v_out), jnp.stack(ki_out),
            jnp.stack(conv_out), jnp.stack(ssm_out))


def kernel(x_prompt, x_sample, cache_k, cache_v, cache_kidx, cache_mem_k, cache_mem_v, state_conv, state_ssm,
           page_table, mem_prompt, w_ssm_in, ssm_conv_w, ssm_conv_b, ssm_dt_bias, ssm_a_log, ssm_d, ssm_norm_w,
           w_ssm_out, w_att_in, w_att_out, w_mem_q, w_mem_kv, w_mem_out, w_ffn_in, w_ffn_out, ln_g, ln_b):
    zx_cols = D_INNER + CONV_DIM
    att_cols = sum(ATT_SPLITS[:4])
    pad128 = lambda w: jnp.pad(w, ((0, 0), (0, 0), (0, 128 - w.shape[-1])))
    params = {'w_ssm_main': w_ssm_in[..., :zx_cols].astype(BF16), 'w_ssm_dt': pad128(w_ssm_in[..., zx_cols:]).astype(BF16),
              'ssm_conv_w': ssm_conv_w, 'ssm_conv_b': ssm_conv_b, 'ssm_dt_bias': ssm_dt_bias,
              'ssm_a_log': ssm_a_log, 'ssm_d': ssm_d, 'ssm_norm_w': ssm_norm_w, 'w_ssm_out': w_ssm_out.astype(BF16),
              'w_att_main': w_att_in[..., :att_cols].astype(BF16), 'w_att_tail': pad128(w_att_in[..., att_cols:]).astype(BF16),
              'w_att_out': w_att_out.astype(BF16),
              'w_mem_q': w_mem_q.astype(BF16), 'w_mem_out': w_mem_out.astype(BF16),
              'w_ffn_in': w_ffn_in.astype(BF16), 'w_ffn_out': w_ffn_out.astype(BF16),
              'ln_g': ln_g, 'ln_b': ln_b}
    bp, seq = x_prompt.shape[:2]
    pos_p = jnp.arange(seq, dtype=jnp.int32)
    w_mem_kv_bf = w_mem_kv.astype(BF16)
    mem_rows = mem_prompt.reshape(bp * N_MEM, D_MODEL)
    mem_kv = jnp.stack([proj(mem_rows, w_mem_kv_bf[l], tm=bp * N_MEM, tn=512) for l in range(DEPTH)])
    memk_prompt = mem_kv[..., :D_MODEL].reshape(DEPTH, bp, N_MEM, N_MEM_HEADS, MEM_HEAD_DIM)
    memv_prompt = mem_kv[..., D_MODEL:].reshape(DEPTH, bp, N_MEM, N_MEM_HEADS, MEM_HEAD_DIM)
    n_ssm = (DEPTH + 1) // 2
    conv0 = jnp.zeros((n_ssm, bp, CONV_W - 1, CONV_DIM), x_prompt.dtype)
    ssm0 = jnp.zeros((n_ssm, bp, N_SSM_HEADS, SSM_HEAD_DIM, D_STATE), x_prompt.dtype)
    attend_p = lambda j, q, k, v, qi, ki, wi: dsa_prompt_attend(q, k, v, qi, ki, wi, tq=min(256, seq),
                                                                tk=min(512, seq))
    y_prompt, k_prompt, v_prompt, kidx_prompt, conv_prompt, ssm_prompt = run_trunk(
        x_prompt, mem_kv, mem_kv, conv0, ssm0, attend_p, pos_p, params, tm=min(1024, bp * seq), tm_mem=min(512, seq))
    past = page_table.shape[1] * PAGE_SIZE
    bs, ts = x_sample.shape[:2]
    pos_s = past + jnp.arange(ts, dtype=jnp.int32)
    attend_s = lambda j, q, k, v, qi, ki, wi: dsa_sample_attend(q, k, v, qi, ki, wi, cache_k, cache_v, cache_kidx, j,
                                                                page_table)
    y_sample, k_sample, v_sample, kidx_sample, conv_sample, ssm_sample = run_trunk(
        x_sample, cache_mem_k, cache_mem_v, state_conv, state_ssm, attend_s, pos_s, params, tm=bs * ts, tm_mem=bs * ts)
    return (y_prompt, y_sample, k_prompt, v_prompt, kidx_prompt, conv_prompt, ssm_prompt, memk_prompt, memv_prompt,
            k_sample, v_sample, kidx_sample, conv_sample, ssm_sample)
```

```python
import functools

import jax
import jax.numpy as jnp
from jax import lax
from jax.experimental import pallas as pl
from jax.experimental.pallas import tpu as pltpu

F32 = jnp.float32
BF16 = jnp.bfloat16

DEPTH = 4
N_MIXERS = 2
D_MODEL = 1024
PAGE_SIZE = 128

D_INNER = 2048
SSM_HEAD_DIM = 64
N_SSM_HEADS = 32
SSM_GROUPS = 4
HEADS_PER_GROUP = 8
D_STATE = 128
CONV_W = 4
CONV_DIM = D_INNER + 2 * SSM_GROUPS * D_STATE
SSM_CHUNK = 128

ATT_HEAD_DIM = 64
N_ATT_HEADS = 16
N_KV_HEADS = 4
KV_GROUP = 4
N_IDX_HEADS = 8
IDX_DIM = 64
TOPK_MAX = 256
ATT_SPLITS = [1024, 256, 256, 512, 64, 8]
ROPE_THETA = 10000.0

N_MEM = 256
N_MEM_HEADS = 4
MEM_HEAD_DIM = 256
FFN_HIDDEN = 2816

DEEPNORM_ALPHA = (2 * DEPTH) ** 0.25
LN_EPS = 1e-5

VMEM_LIMIT_BYTES = 48 * 1024 * 1024


def _layer_norm_rows(y, g, b):
    mu = jnp.mean(y, axis=-1, keepdims=True)
    d = y - mu
    var = jnp.mean(d * d, axis=-1, keepdims=True)
    return d * lax.rsqrt(var + LN_EPS) * g + b


def _ffn_ln_kernel(x_ref, wa_ref, wb_ref, wo_ref, g_ref, b_ref, o_ref, xb_ref, acc_ref):
    k = pl.program_id(1)

    @pl.when(k == 0)
    def _():
        xb_ref[...] = x_ref[...].astype(BF16)
        acc_ref[...] = jnp.zeros_like(acc_ref)

    xb = xb_ref[...]
    a = jnp.dot(xb, wa_ref[...], preferred_element_type=F32)
    b = jnp.dot(xb, wb_ref[...], preferred_element_type=F32)
    h = (a * jax.nn.sigmoid(a)) * b
    acc_ref[...] += jnp.dot(h.astype(BF16), wo_ref[...], preferred_element_type=F32)

    @pl.when(k == pl.num_programs(1) - 1)
    def _():
        y = DEEPNORM_ALPHA * x_ref[...] + acc_ref[...]
        o_ref[...] = _layer_norm_rows(y, g_ref[...], b_ref[...])


def ffn_ln(x, w_in, w_out, g, b, *, tm, th=256):
    m, d = x.shape
    hidden = w_out.shape[0]
    nh = hidden // th
    return pl.pallas_call(
        _ffn_ln_kernel,
        grid=(m // tm, nh),
        in_specs=[
            pl.BlockSpec((tm, d), lambda i, k: (i, 0)),
            pl.BlockSpec((d, th), lambda i, k: (0, k)),
            pl.BlockSpec((d, th), lambda i, k: (0, k + nh)),
            pl.BlockSpec((th, d), lambda i, k: (k, 0)),
            pl.BlockSpec((1, d), lambda i, k: (0, 0)),
            pl.BlockSpec((1, d), lambda i, k: (0, 0)),
        ],
        out_specs=pl.BlockSpec((tm, d), lambda i, k: (i, 0)),
        out_shape=jax.ShapeDtypeStruct((m, d), F32),
        scratch_shapes=[pltpu.VMEM((tm, d), BF16), pltpu.VMEM((tm, d), F32)],
        compiler_params=pltpu.CompilerParams(
            dimension_semantics=("parallel", "arbitrary"), vmem_limit_bytes=VMEM_LIMIT_BYTES),
        name="ffn_ln",
    )(x, w_in, w_in, w_out, g.reshape(1, d), b.reshape(1, d))


def _proj_kernel(x_ref, w_ref, o_ref, xb_ref):
    @pl.when(pl.program_id(1) == 0)
    def _():
        xb_ref[...] = x_ref[...].astype(BF16)

    o_ref[...] = jnp.dot(xb_ref[...], w_ref[...], preferred_element_type=F32)


def proj(x, w, *, tm, tn):
    m, kd = x.shape
    n = w.shape[1]
    return pl.pallas_call(
        _proj_kernel,
        grid=(m // tm, n // tn),
        in_specs=[pl.BlockSpec((tm, kd), lambda i, j: (i, 0)), pl.BlockSpec((kd, tn), lambda i, j: (0, j))],
        out_specs=pl.BlockSpec((tm, tn), lambda i, j: (i, j)),
        out_shape=jax.ShapeDtypeStruct((m, n), F32),
        scratch_shapes=[pltpu.VMEM((tm, kd), BF16)],
        compiler_params=pltpu.CompilerParams(
            dimension_semantics=("parallel", "arbitrary"), vmem_limit_bytes=VMEM_LIMIT_BYTES),
        name="proj",
    )(x, w)


def _out_ln_kernel(h_ref, w_ref, x_ref, g_ref, b_ref, o_ref):
    y = DEEPNORM_ALPHA * x_ref[...] + jnp.dot(h_ref[...].astype(BF16), w_ref[...], preferred_element_type=F32)
    o_ref[...] = _layer_norm_rows(y, g_ref[...], b_ref[...])


def out_ln(h, w, x, g, b, *, tm):
    m, kd = h.shape
    d = w.shape[1]
    return pl.pallas_call(
        _out_ln_kernel,
        grid=(m // tm,),
        in_specs=[
            pl.BlockSpec((tm, kd), lambda i: (i, 0)),
            pl.BlockSpec((kd, d), lambda i: (0, 0)),
            pl.BlockSpec((tm, d), lambda i: (i, 0)),
            pl.BlockSpec((1, d), lambda i: (0, 0)),
            pl.BlockSpec((1, d), lambda i: (0, 0)),
        ],
        out_specs=pl.BlockSpec((tm, d), lambda i: (i, 0)),
        out_shape=jax.ShapeDtypeStruct((m, d), F32),
        compiler_params=pltpu.CompilerParams(
            dimension_semantics=("parallel",), vmem_limit_bytes=VMEM_LIMIT_BYTES),
        name="out_ln",
    )(h, w, x, g.reshape(1, d), b.reshape(1, d))


def _mem_block_kernel(x_ref, wq_ref, mk_ref, mv_ref, wo_ref, g_ref, b_ref, o_ref):
    heads = [slice(h * MEM_HEAD_DIM, (h + 1) * MEM_HEAD_DIM) for h in range(N_MEM_HEADS)]
    scale = MEM_HEAD_DIM ** -0.5
    x = x_ref[0]
    q = jnp.dot(x.astype(BF16), wq_ref[...], preferred_element_type=F32).astype(BF16)
    if mk_ref.ndim == 2:
        k_of = lambda h: mk_ref[:, heads[h]].astype(BF16)
        v_of = lambda h: mv_ref[:, heads[h]].astype(BF16)
        keep = lambda h, s: s
    else:
        rows = N_MEM * N_MEM_HEADS
        k_all = mk_ref[...].reshape(rows, MEM_HEAD_DIM).astype(BF16)
        v_all = mv_ref[...].reshape(rows, MEM_HEAD_DIM).astype(BF16)
        k_of = lambda h: k_all
        v_of = lambda h: v_all
        head_of_col = lax.broadcasted_iota(jnp.int32, (1, rows), 1) % N_MEM_HEADS
        keep = lambda h, s: jnp.where(head_of_col == h, s, NEG_BIG)
    s_all = [keep(h, lax.dot_general(q[:, sl], k_of(h), _NT, preferred_element_type=F32) * scale)
             for h, sl in enumerate(heads)]
    e_all = [jnp.exp(s - jnp.max(s, axis=1, keepdims=True)) for s in s_all]
    o_heads = []
    for h in range(N_MEM_HEADS):
        pv = jnp.dot(e_all[h].astype(BF16), v_of(h), preferred_element_type=F32)
        o_heads.append((pv / jnp.sum(e_all[h], axis=1, keepdims=True)).astype(BF16))
    o = jnp.concatenate(o_heads, axis=1)
    y = DEEPNORM_ALPHA * x + jnp.dot(o, wo_ref[...], preferred_element_type=F32)
    o_ref[0] = _layer_norm_rows(y, g_ref[...], b_ref[...])


def mem_block(x, w_q, mk, mv, layer, w_o, g, b, *, tm):
    bsz, t, d = x.shape
    const = lambda shape: pl.BlockSpec(shape, lambda b, i: (0,) * len(shape))
    if mk.ndim == 5:
        mk_spec = mv_spec = pl.BlockSpec((None, None, N_MEM, N_MEM_HEADS, MEM_HEAD_DIM),
                                         lambda b, i: (layer, b, 0, 0, 0))
    else:
        mk_spec = pl.BlockSpec((None, N_MEM, d), lambda b, i: (layer, b, 0))
        mv_spec = pl.BlockSpec((None, N_MEM, d), lambda b, i: (layer, b, 1))
    return pl.pallas_call(
        _mem_block_kernel,
        grid=(bsz, t // tm),
        in_specs=[pl.BlockSpec((1, tm, d), lambda b, i: (b, i, 0)), const((d, d)), mk_spec, mv_spec,
                  const((d, d)), const((1, d)), const((1, d))],
        out_specs=pl.BlockSpec((1, tm, d), lambda b, i: (b, i, 0)),
        out_shape=jax.ShapeDtypeStruct((bsz, t, d), F32),
        compiler_params=pltpu.CompilerParams(
            dimension_semantics=("parallel", "arbitrary"), vmem_limit_bytes=VMEM_LIMIT_BYTES),
        name="mem_block",
    )(x, w_q, mk, mv, w_o, g.reshape(1, d), b.reshape(1, d))


CONV_COLS = 1024


def _conv_kernel(x_ref, st_ref, w_ref, b_ref, act_ref, last_ref, prev_ref, *, tc):
    @pl.when(pl.program_id(2) == 0)
    def _():
        prev_ref[...] = st_ref[0]

    x = x_ref[0]
    prev = prev_ref[...]
    row = lax.broadcasted_iota(jnp.int32, (8, CONV_COLS), 0)
    acc = jnp.broadcast_to(b_ref[...], x.shape)
    for s in (3, 2, 1):
        rolled = pltpu.roll(x, s, axis=0)
        top = jnp.where(row < s, pltpu.roll(prev, s, axis=0), rolled[0:8])
        shifted = top if tc == 8 else jnp.concatenate([top, rolled[8:]], axis=0)
        acc = acc + shifted * w_ref[CONV_W - 1 - s:CONV_W - s, :]
    acc = acc + x * w_ref[CONV_W - 1:CONV_W, :]
    act_ref[0] = acc * jax.nn.sigmoid(acc)
    prev_ref[...] = x[tc - 8:tc]
    last_ref[0] = x[tc - 8:tc]


def ssm_conv(zx, conv_state, conv_w, conv_b, *, tc):
    bsz, t = zx.shape[:2]
    ncb = CONV_DIM // CONV_COLS
    col0 = D_INNER // CONV_COLS
    st8 = jnp.pad(conv_state, ((0, 0), (8 - (CONV_W - 1), 0), (0, 0)))
    act, last = pl.pallas_call(
        functools.partial(_conv_kernel, tc=tc),
        grid=(bsz, ncb, t // tc),
        in_specs=[
            pl.BlockSpec((1, tc, CONV_COLS), lambda b, c, i: (b, i, col0 + c)),
            pl.BlockSpec((1, 8, CONV_COLS), lambda b, c, i: (b, 0, c)),
            pl.BlockSpec((CONV_W, CONV_COLS), lambda b, c, i: (0, c)),
            pl.BlockSpec((1, CONV_COLS), lambda b, c, i: (0, c)),
        ],
        out_specs=[
            pl.BlockSpec((1, tc, CONV_COLS), lambda b, c, i: (b, i, c)),
            pl.BlockSpec((1, 8, CONV_COLS), lambda b, c, i: (b, 0, c)),
        ],
        out_shape=[jax.ShapeDtypeStruct((bsz, t, CONV_DIM), F32), jax.ShapeDtypeStruct((bsz, 8, CONV_DIM), F32)],
        scratch_shapes=[pltpu.VMEM((8, CONV_COLS), F32)],
        compiler_params=pltpu.CompilerParams(
            dimension_semantics=("parallel", "parallel", "arbitrary"), vmem_limit_bytes=VMEM_LIMIT_BYTES),
        name="ssm_conv",
    )(zx, st8, conv_w, conv_b.reshape(1, CONV_DIM))
    return act, last[:, 8 - (CONV_W - 1):, :]


def _split3(v):
    hi = v.astype(BF16)
    r1 = v - hi.astype(F32)
    mid = r1.astype(BF16)
    lo = (r1 - mid.astype(F32)).astype(BF16)
    return hi, mid, lo


def _dot01_right(v, ones_mat):
    return sum(jnp.dot(p, ones_mat, preferred_element_type=F32) for p in _split3(v))


def _dot01_left(ones_mat, v):
    return sum(jnp.dot(ones_mat, p, preferred_element_type=F32) for p in _split3(v))


def _softplus(x):
    return jnp.maximum(x, 0.0) + jnp.log1p(jnp.exp(-jnp.abs(x)))


def _ssd_kernel(xs_ref, b_ref, c_ref, z_ref, dt_ref, dtt_ref, dtb_ref, dtbt_ref, alog_ref, alogt_ref,
                d_ref, nw_ref, h0_ref, y_ref, hT_ref, st_ref, yacc_ref, *, c):
    ci = pl.program_id(1)

    @pl.when(ci == 0)
    def _():
        st_ref[...] = h0_ref[0]

    gw = HEADS_PER_GROUP * SSM_HEAD_DIM
    dt = _softplus(dt_ref[0][:, 0:N_SSM_HEADS] + dtb_ref[...])
    dtt = _softplus(dtt_ref[0] + dtbt_ref[...])
    a = dt * -jnp.exp(alog_ref[...])
    at = dtt * -jnp.exp(alogt_ref[...])
    ri = lax.broadcasted_iota(jnp.int32, (c, c), 0)
    cj = lax.broadcasted_iota(jnp.int32, (c, c), 1)
    causal = ri >= cj
    acum = _dot01_left(jnp.where(causal, 1.0, 0.0).astype(BF16), a)
    acum_t = _dot01_right(at, jnp.where(ri <= cj, 1.0, 0.0).astype(BF16))
    a_last = acum[c - 1:c, :]
    expand = jnp.where(lax.broadcasted_iota(jnp.int32, (N_SSM_HEADS, D_INNER), 1) // SSM_HEAD_DIM
                       == lax.broadcasted_iota(jnp.int32, (N_SSM_HEADS, D_INNER), 0), 1.0, 0.0).astype(BF16)
    e_dt = _dot01_right(dt, expand)
    e_in = _dot01_right(jnp.exp(acum), expand)
    e_out = _dot01_right(jnp.exp(a_last - acum), expand)
    e_chunk = e_in[c - 1:c, :]
    xs = xs_ref[0]
    xdt = xs * e_dt
    head_of_col = lax.broadcasted_iota(jnp.int32, (c, gw), 1) // SSM_HEAD_DIM
    for g in range(SSM_GROUPS):
        gs = slice(g * gw, (g + 1) * gw)
        bg = b_ref[0][:, g * D_STATE:(g + 1) * D_STATE].astype(BF16)
        cg = c_ref[0][:, g * D_STATE:(g + 1) * D_STATE].astype(BF16)
        cb = lax.dot_general(cg, bg, _NT, preferred_element_type=F32)
        xg = xdt[:, gs]
        yg = jnp.zeros((c, gw), F32)
        for r in range(HEADS_PER_GROUP):
            h = g * HEADS_PER_GROUP + r
            seg = acum[:, h:h + 1] - acum_t[h:h + 1, :]
            m = (cb * jnp.exp(jnp.where(causal, seg, -jnp.inf))).astype(BF16)
            xm = jnp.where(head_of_col == r, xg, 0.0).astype(BF16)
            yg = yg + jnp.dot(m, xm, preferred_element_type=F32)
        state = st_ref[g]
        y_off = jnp.dot(cg, state.astype(BF16), preferred_element_type=F32) * e_in[:, gs]
        yacc_ref[:, gs] = yg + y_off + xs[:, gs] * d_ref[:, gs]
        xd = (xg * e_out[:, gs]).astype(BF16)
        st_ref[g] = state * e_chunk[:, gs] + lax.dot_general(bg, xd, (((0,), (0,)), ((), ())),
                                                             preferred_element_type=F32)
    z = z_ref[0]
    yz = yacc_ref[...] * (z * jax.nn.sigmoid(z))
    y_ref[0] = yz * lax.rsqrt(jnp.mean(yz * yz, axis=-1, keepdims=True) + LN_EPS) * nw_ref[...]

    @pl.when(ci == pl.num_programs(1) - 1)
    def _():
        hT_ref[0] = st_ref[...]


def ssd_scan(act, zx, dt_raw, dt_bias, a_log, d_skip, norm_w, ssm_state, *, c):
    bsz, t = act.shape[:2]
    gw = HEADS_PER_GROUP * SSM_HEAD_DIM
    bc_w = SSM_GROUPS * D_STATE
    h0 = ssm_state.reshape(bsz, SSM_GROUPS, HEADS_PER_GROUP, SSM_HEAD_DIM, D_STATE)
    h0 = h0.transpose(0, 1, 4, 2, 3).reshape(bsz, SSM_GROUPS, D_STATE, gw)
    dtt = jnp.swapaxes(dt_raw[..., :N_SSM_HEADS], 1, 2)
    row = lambda v: v.reshape(1, -1)
    col = lambda v: v.reshape(-1, 1)
    full = lambda shape: pl.BlockSpec(shape, lambda b, i: (0,) * len(shape))
    y, h_t = pl.pallas_call(
        functools.partial(_ssd_kernel, c=c),
        grid=(bsz, t // c),
        in_specs=[
            pl.BlockSpec((1, c, D_INNER), lambda b, i: (b, i, 0)),
            pl.BlockSpec((1, c, bc_w), lambda b, i: (b, i, D_INNER // bc_w)),
            pl.BlockSpec((1, c, bc_w), lambda b, i: (b, i, D_INNER // bc_w + 1)),
            pl.BlockSpec((1, c, D_INNER), lambda b, i: (b, i, 0)),
            pl.BlockSpec((1, c, 128), lambda b, i: (b, i, 0)),
            pl.BlockSpec((1, N_SSM_HEADS, c), lambda b, i: (b, 0, i)),
            full((1, N_SSM_HEADS)), full((N_SSM_HEADS, 1)), full((1, N_SSM_HEADS)), full((N_SSM_HEADS, 1)),
            full((1, D_INNER)), full((1, D_INNER)),
            pl.BlockSpec((1, SSM_GROUPS, D_STATE, gw), lambda b, i: (b, 0, 0, 0)),
        ],
        out_specs=[
            pl.BlockSpec((1, c, D_INNER), lambda b, i: (b, i, 0)),
            pl.BlockSpec((1, SSM_GROUPS, D_STATE, gw), lambda b, i: (b, 0, 0, 0)),
        ],
        out_shape=[jax.ShapeDtypeStruct((bsz, t, D_INNER), F32),
                   jax.ShapeDtypeStruct((bsz, SSM_GROUPS, D_STATE, gw), F32)],
        scratch_shapes=[pltpu.VMEM((SSM_GROUPS, D_STATE, gw), F32), pltpu.VMEM((c, D_INNER), F32)],
        compiler_params=pltpu.CompilerParams(
            dimension_semantics=("parallel", "arbitrary"), vmem_limit_bytes=VMEM_LIMIT_BYTES),
        name="ssd_scan",
    )(act, act, act, zx, dt_raw, dtt, row(dt_bias), col(dt_bias), row(a_log), col(a_log),
      row(jnp.repeat(d_skip, SSM_HEAD_DIM)), row(norm_w), h0)
    h_t = h_t.reshape(bsz, SSM_GROUPS, D_STATE, HEADS_PER_GROUP, SSM_HEAD_DIM)
    return y, h_t.transpose(0, 1, 3, 4, 2).reshape(bsz, N_SSM_HEADS, SSM_HEAD_DIM, D_STATE)


INT_MIN = -2 ** 31
NEG_BIG = -1e30
_NT = (((1,), (1,)), ((), ()))


def _float_key(x):
    x = jnp.where(x == 0.0, 0.0, x)
    bits = lax.bitcast_convert_type(x, jnp.int32)
    return bits ^ ((bits >> 31) & 0x7FFFFFFF)


def _dsa_prompt_kernel(qt_ref, qit_ref, wit_ref, k_ref, vxt_ref, ki_ref, o_ref, key_ref, m_ref, acc_ref,
                         s_ref, p_ref, bias_ref, eqb_ref, *, tq, tk, topk):
    i = pl.program_id(1)
    nkb = ((i + 1) * tq + tk - 1) // tk
    gq = KV_GROUP * tq
    rep8 = lambda v: jnp.broadcast_to(v, (8, tq))
    as3 = lambda x: x.reshape(tk // 8, 8, tq)
    w_heads = wit_ref[0, 0] * (N_IDX_HEADS ** -0.5)
    q_pos = i * tq + lax.broadcasted_iota(jnp.int32, (tk, tq), 1)

    def score_block(kb, c):
        off = pl.multiple_of(kb * tk, tk)
        s = jnp.dot(ki_ref[0, pl.ds(off, tk), :], qit_ref[0, 0], preferred_element_type=F32)
        acc = jnp.zeros((tk, tq), F32)
        for h in range(N_IDX_HEADS):
            acc = acc + w_heads[h:h + 1, :] * jnp.maximum(s[:, h * tq:(h + 1) * tq], 0.0)
        kpos = off + lax.broadcasted_iota(jnp.int32, (tk, tq), 0)
        key_ref[kb] = jnp.where(kpos <= q_pos, _float_key(acc), INT_MIN)
        return c

    lax.fori_loop(0, nkb, score_block, 0)

    def count(cand, strict):
        def body(kb, accs):
            blk = key_ref[kb]
            accs = list(accs)
            for r in range(tk // 8):
                part = blk[r * 8:(r + 1) * 8, :]
                hit = (part > cand) if strict else (part >= cand)
                accs[r % 4] = accs[r % 4] + jnp.where(hit, 1.0, 0.0)
            return tuple(accs)
        a0, a1, a2, a3 = lax.fori_loop(0, nkb, body, (jnp.zeros((8, tq), F32),) * 4)
        return jnp.sum((a0 + a1) + (a2 + a3), axis=0, keepdims=True)

    kf = float(topk)
    zero = jnp.zeros((8, tq), jnp.int32)
    thr = jnp.where(rep8(count(zero, False)) >= kf, zero, jnp.full((8, tq), INT_MIN, jnp.int32))

    def bit_step(it, thr):
        cand = thr | jnp.left_shift(jnp.int32(1), 30 - it)
        return jnp.where(rep8(count(cand, False)) >= kf, cand, thr)

    thr = lax.fori_loop(0, 31, bit_step, thr)
    need = jnp.where(thr == INT_MIN, 0.0, kf - rep8(count(thr, True)))
    floor_thr = jnp.where(thr == INT_MIN, INT_MIN + 1, thr)
    untied = jnp.where((thr == INT_MIN) | (rep8(count(thr, False)) <= kf), 1.0, 0.0)
    no_ties = jnp.min(untied) > 0.0

    m_ref[...] = jnp.full(m_ref.shape, NEG_BIG, F32)
    acc_ref[...] = jnp.zeros(acc_ref.shape, F32)
    eqb_ref[...] = jnp.zeros(eqb_ref.shape, F32)

    def attend_block(kb, c):
        off = pl.multiple_of(kb * tk, tk)

        @pl.when(no_ties)
        def _():
            bias_ref[...] = jnp.where(as3(key_ref[kb]) >= floor_thr[None], 0.0, NEG_BIG).reshape(tk, tq)

        @pl.when(jnp.logical_not(no_ties))
        def _():
            lower = jnp.where(lax.broadcasted_iota(jnp.int32, (tk, tk), 1)
                              < lax.broadcasted_iota(jnp.int32, (tk, tk), 0), 1.0, 0.0).astype(BF16)
            key3 = as3(key_ref[kb])
            eq3 = key3 == thr[None]
            eqf = jnp.where(eq3, 1.0, 0.0)
            eq_before = eqb_ref[...]
            rank3 = as3(jnp.dot(lower, eqf.reshape(tk, tq).astype(BF16), preferred_element_type=F32)) + eq_before[None]
            bias_ref[...] = jnp.where(key3 > thr[None], 0.0,
                                      jnp.where(eq3, jnp.where(rank3 < need[None], 0.0, NEG_BIG),
                                                NEG_BIG)).reshape(tk, tq)
            eqb_ref[...] = eq_before + rep8(jnp.sum(jnp.sum(eqf, axis=0), axis=0, keepdims=True))

        bias4 = jnp.concatenate([bias_ref[...]] * KV_GROUP, axis=1)
        col_max = []
        for j in range(N_KV_HEADS):
            kblk = k_ref[0, pl.ds(off, tk), j * 64:(j + 1) * 64]
            s = jnp.dot(kblk, qt_ref[0, 0, j], preferred_element_type=F32) + bias4
            s_ref[j] = s
            col_max.append(jnp.max(s, axis=0, keepdims=True))
        alpha = []
        for j in range(N_KV_HEADS):
            m_old = m_ref[j]
            m_new = jnp.maximum(m_old, col_max[j])
            alpha.append(jnp.exp(m_old - m_new))
            p_ref[j] = jnp.exp(s_ref[j] - m_new).astype(BF16)
            m_ref[j] = m_new
        for j in range(N_KV_HEADS):
            pv = jnp.dot(vxt_ref[0, j, kb], p_ref[j], preferred_element_type=F32)
            acc_ref[j] = alpha[j] * acc_ref[j] + pv
        return c

    lax.fori_loop(0, nkb, attend_block, 0)

    for j in range(N_KV_HEADS):
        a = acc_ref[j]
        o_t = a[0:64, :] / a[64:65, :]
        for g in range(KV_GROUP):
            h = j * KV_GROUP + g
            o_ref[0, :, h * 64:(h + 1) * 64] = o_t[:, g * tq:(g + 1) * tq].T


def dsa_prompt_attend(q, k, v, qi, ki, wi, *, tq, tk):
    bsz, t = q.shape[:2]
    nq, nk = t // tq, t // tk
    topk = min(TOPK_MAX, t // 4)
    qt = (q * ATT_HEAD_DIM ** -0.5).astype(BF16).reshape(bsz, nq, tq, N_KV_HEADS, KV_GROUP, 64)
    qt = qt.transpose(0, 1, 3, 5, 4, 2).reshape(bsz, nq, N_KV_HEADS, 64, KV_GROUP * tq)
    qit = (qi * IDX_DIM ** -0.5).astype(BF16).reshape(bsz, nq, tq, N_IDX_HEADS, 64)
    qit = qit.transpose(0, 1, 4, 3, 2).reshape(bsz, nq, 64, N_IDX_HEADS * tq)
    wit = wi.reshape(bsz, nq, tq, N_IDX_HEADS).transpose(0, 1, 3, 2)
    vt = v.astype(BF16).reshape(bsz, nk, tk, N_KV_HEADS, 64).transpose(0, 3, 1, 4, 2)
    vxt = jnp.concatenate([vt, jnp.ones_like(vt)], axis=3)
    kern = functools.partial(_dsa_prompt_kernel, tq=tq, tk=tk, topk=topk)
    return pl.pallas_call(
        kern,
        grid=(bsz, nq),
        in_specs=[
            pl.BlockSpec((1, 1, N_KV_HEADS, 64, KV_GROUP * tq), lambda b, i: (b, i, 0, 0, 0)),
            pl.BlockSpec((1, 1, 64, N_IDX_HEADS * tq), lambda b, i: (b, i, 0, 0)),
            pl.BlockSpec((1, 1, N_IDX_HEADS, tq), lambda b, i: (b, i, 0, 0)),
            pl.BlockSpec((1, t, 256), lambda b, i: (b, 0, 0), pipeline_mode=pl.Buffered(1)),
            pl.BlockSpec((1, N_KV_HEADS, nk, 128, tk), lambda b, i: (b, 0, 0, 0, 0), pipeline_mode=pl.Buffered(1)),
            pl.BlockSpec((1, t, IDX_DIM), lambda b, i: (b, 0, 0), pipeline_mode=pl.Buffered(1)),
        ],
        out_specs=pl.BlockSpec((1, tq, 1024), lambda b, i: (b, i, 0)),
        out_shape=jax.ShapeDtypeStruct((bsz, t, 1024), F32),
        scratch_shapes=[
            pltpu.VMEM((nk, tk, tq), jnp.int32),
            pltpu.VMEM((N_KV_HEADS, 1, KV_GROUP * tq), F32),
            pltpu.VMEM((N_KV_HEADS, 128, KV_GROUP * tq), F32),
            pltpu.VMEM((N_KV_HEADS, tk, KV_GROUP * tq), F32),
            pltpu.VMEM((N_KV_HEADS, tk, KV_GROUP * tq), BF16),
            pltpu.VMEM((tk, tq), F32),
            pltpu.VMEM((8, tq), F32),
        ],
        compiler_params=pltpu.CompilerParams(
            dimension_semantics=("parallel", "arbitrary"), vmem_limit_bytes=VMEM_LIMIT_BYTES),
        name="dsa_prompt_attend",
    )(qt, qit, wit, k.astype(BF16), vxt, ki.astype(BF16))


MAX_PAGES_PER_STEP = 16


def _sample_select_kernel(pt_ref, qi_ref, wi_ref, kinew_ref, *rest, t, n_steps, topk, pps):
    ki_refs = rest[:pps]
    bias_ref, qis_ref, wib_ref, key_ref = rest[pps:]
    p = pl.program_id(1)
    w = pps * PAGE_SIZE
    nl = w // 128

    @pl.when(p == 0)
    def _():
        for h in range(N_IDX_HEADS):
            qis_ref[h * t:(h + 1) * t, :] = qi_ref[0, :, h * 64:(h + 1) * 64]
            wib_ref[h] = jnp.broadcast_to(wi_ref[0, :, h:h + 1] * (N_IDX_HEADS ** -0.5), (t, 128))

    def scores(s):
        acc = jnp.zeros((t, s.shape[1]), F32)
        for h in range(N_IDX_HEADS):
            wfull = jnp.concatenate([wib_ref[h]] * (s.shape[1] // 128), axis=1)
            acc = acc + wfull * jnp.maximum(s[h * t:(h + 1) * t, :], 0.0)
        return acc

    ki_t = jnp.concatenate([r[...] for r in ki_refs], axis=1).astype(BF16)
    key_ref[p] = _float_key(scores(jnp.dot(qis_ref[...].astype(BF16), ki_t, preferred_element_type=F32)))

    @pl.when(p == n_steps - 1)
    def _():
        knew = _float_key(scores(lax.dot_general(qis_ref[...].astype(BF16), kinew_ref[0].astype(BF16), _NT,
                                                 preferred_element_type=F32)))
        n_idx = lax.broadcasted_iota(jnp.int32, (t, 128), 1)
        r_idx = lax.broadcasted_iota(jnp.int32, (t, 128), 0)
        knew = jnp.where(n_idx <= r_idx, knew, INT_MIN)
        key_ref[n_steps] = jnp.concatenate([knew, jnp.full((t, w - 128), INT_MIN, jnp.int32)], axis=1)

        def count(cand, strict):
            def body(blk, acc):
                kb = key_ref[blk]
                for c in range(nl):
                    part = kb[:, c * 128:(c + 1) * 128]
                    hit = (part > cand) if strict else (part >= cand)
                    acc = acc + jnp.where(hit, 1.0, 0.0)
                return acc
            acc = lax.fori_loop(0, n_steps + 1, body, jnp.zeros((t, 128), F32))
            return jnp.sum(acc, axis=1, keepdims=True)

        kf = float(topk)
        zero = jnp.zeros((t, 128), jnp.int32)
        thr = jnp.where(count(zero, False) >= kf, zero, jnp.full((t, 128), INT_MIN, jnp.int32))

        def bit_step(it, thr):
            cand = thr | jnp.left_shift(jnp.int32(1), 30 - it)
            return jnp.where(count(cand, False) >= kf, cand, thr)

        thr = lax.fori_loop(0, 31, bit_step, thr)
        need = jnp.where(thr == INT_MIN, 0.0, kf - count(thr, True))
        floor_thr = jnp.where(thr == INT_MIN, INT_MIN + 1, thr)
        untied = jnp.where((thr == INT_MIN) | (count(thr, False) <= kf), 1.0, 0.0)
        no_ties = jnp.min(untied) > 0.0

        @pl.when(no_ties)
        def _():
            wide = jnp.concatenate([floor_thr] * nl, axis=1)

            def emit_simple(blk, c):
                bias_ref[0, blk] = jnp.where(key_ref[blk] >= wide, 0.0, NEG_BIG)
                return c

            lax.fori_loop(0, n_steps + 1, emit_simple, 0)

        upper = jnp.where(lax.broadcasted_iota(jnp.int32, (128, 128), 0)
                          < lax.broadcasted_iota(jnp.int32, (128, 128), 1), 1.0, 0.0).astype(BF16)

        def emit(blk, eq_before):
            kb = key_ref[blk]
            parts = []
            for c in range(nl):
                part = kb[:, c * 128:(c + 1) * 128]
                eq = part == thr
                eqf = jnp.where(eq, 1.0, 0.0)
                rank = jnp.dot(eqf.astype(BF16), upper, preferred_element_type=F32) + eq_before
                parts.append(jnp.where(part > thr, 0.0,
                                       jnp.where(eq, jnp.where(rank < need, 0.0, NEG_BIG), NEG_BIG)))
                eq_before = eq_before + jnp.sum(eqf, axis=1, keepdims=True)
            bias_ref[0, blk] = jnp.concatenate(parts, axis=1)
            return eq_before

        @pl.when(jnp.logical_not(no_ties))
        def _():
            lax.fori_loop(0, n_steps + 1, emit, jnp.zeros((t, 128), F32))


def _sample_attend_kernel(pt_ref, q_ref, bias_ref, biasnew_ref, knew_ref, vnew_ref, *rest, t, n_steps, pps):
    k_refs = rest[:pps]
    v_refs = rest[pps:2 * pps]
    o_ref, qs_ref, m_ref, l_ref, acc_ref = rest[2 * pps:]
    p = pl.program_id(1)
    rows = KV_GROUP * t

    @pl.when(p == 0)
    def _():
        for h in range(N_ATT_HEADS):
            j, g = divmod(h, KV_GROUP)
            qs_ref[j, g * t:(g + 1) * t, :] = q_ref[0, :, h * 64:(h + 1) * 64]
        m_ref[...] = jnp.full(m_ref.shape, NEG_BIG, F32)
        l_ref[...] = jnp.zeros(l_ref.shape, F32)
        acc_ref[...] = jnp.zeros(acc_ref.shape, F32)

    def update(k_t, v_t, bias):
        bias4 = jnp.concatenate([bias] * KV_GROUP, axis=0)
        s_all = [jnp.dot(qs_ref[j].astype(BF16), k_t(j).astype(BF16), preferred_element_type=F32) + bias4
                 for j in range(N_KV_HEADS)]
        alphas, probs = [], []
        for j in range(N_KV_HEADS):
            m_old = m_ref[j]
            m_new = jnp.maximum(m_old, jnp.max(s_all[j], axis=1, keepdims=True))
            alpha = jnp.exp(m_old - m_new)
            pr = jnp.exp(s_all[j] - m_new[:, 0:1])
            l_ref[j] = alpha * l_ref[j] + jnp.sum(pr, axis=1, keepdims=True)
            m_ref[j] = m_new
            alphas.append(alpha)
            probs.append(pr.astype(BF16))
        for j in range(N_KV_HEADS):
            pv = lax.dot_general(probs[j], v_t(j).astype(BF16), _NT, preferred_element_type=F32)
            acc_ref[j] = alphas[j][:, 0:64] * acc_ref[j] + pv

    update(lambda j: jnp.concatenate([r[j] for r in k_refs], axis=1),
           lambda j: jnp.concatenate([r[j] for r in v_refs], axis=1), bias_ref[0, 0])

    @pl.when(p == n_steps - 1)
    def _():
        update(lambda j: knew_ref[0, j * 64:(j + 1) * 64, :], lambda j: vnew_ref[0, j * 64:(j + 1) * 64, :],
               biasnew_ref[0, 0, :, 0:128])
        for h in range(N_ATT_HEADS):
            j, g = divmod(h, KV_GROUP)
            sl = slice(g * t, (g + 1) * t)
            o_ref[0, :, h * 64:(h + 1) * 64] = acc_ref[j, sl, :] / l_ref[j, sl, 0:64]


def dsa_sample_attend(q, k, v, qi, ki, wi, cache_k, cache_v, cache_kidx, layer, page_table):
    bsz, t = q.shape[:2]
    n_pages = page_table.shape[1]
    past = n_pages * PAGE_SIZE
    pps = min(MAX_PAGES_PER_STEP, n_pages)
    assert n_pages % pps == 0, (n_pages, pps)
    n_steps = n_pages // pps
    w = pps * PAGE_SIZE
    topk = min(TOPK_MAX, (past + t) // 4)
    pad = lambda a: jnp.pad(a, ((0, 0), (0, PAGE_SIZE - t), (0, 0)))
    pool_k = cache_k.transpose(0, 1, 3, 4, 2)
    pool_v = cache_v.transpose(0, 1, 3, 4, 2)
    pool_ki = cache_kidx.transpose(0, 1, 3, 2)
    knew_t = jnp.swapaxes(pad(k), 1, 2)
    vnew_t = jnp.swapaxes(pad(v), 1, 2)

    def page_spec(r, *major):
        zeros = (0,) * (1 + len(major))
        return pl.BlockSpec((None, None) + major + (PAGE_SIZE,),
                            lambda b, p, pt: (layer, pt[b, p * pps + r]) + zeros)

    bias = pl.pallas_call(
        functools.partial(_sample_select_kernel, t=t, n_steps=n_steps, topk=topk, pps=pps),
        grid_spec=pltpu.PrefetchScalarGridSpec(
            num_scalar_prefetch=1,
            grid=(bsz, n_steps),
            in_specs=[
                pl.BlockSpec((1, t, 512), lambda b, p, pt: (b, 0, 0)),
                pl.BlockSpec((1, t, N_IDX_HEADS), lambda b, p, pt: (b, 0, 0)),
                pl.BlockSpec((1, PAGE_SIZE, IDX_DIM), lambda b, p, pt: (b, 0, 0)),
            ] + [page_spec(r, IDX_DIM) for r in range(pps)],
            out_specs=pl.BlockSpec((1, n_steps + 1, t, w), lambda b, p, pt: (b, 0, 0, 0)),
            scratch_shapes=[
                pltpu.VMEM((N_IDX_HEADS * t, 64), F32),
                pltpu.VMEM((N_IDX_HEADS, t, 128), F32),
                pltpu.VMEM((n_steps + 1, t, w), jnp.int32),
            ]),
        out_shape=jax.ShapeDtypeStruct((bsz, n_steps + 1, t, w), F32),
        compiler_params=pltpu.CompilerParams(
            dimension_semantics=("parallel", "arbitrary"), vmem_limit_bytes=VMEM_LIMIT_BYTES),
        name="dsa_sample_select",
    )(page_table, qi * IDX_DIM ** -0.5, wi, pad(ki), *([pool_ki] * pps))

    return pl.pallas_call(
        functools.partial(_sample_attend_kernel, t=t, n_steps=n_steps, pps=pps),
        grid_spec=pltpu.PrefetchScalarGridSpec(
            num_scalar_prefetch=1,
            grid=(bsz, n_steps),
            in_specs=[
                pl.BlockSpec((1, t, 1024), lambda b, p, pt: (b, 0, 0)),
                pl.BlockSpec((1, 1, t, w), lambda b, p, pt: (b, p, 0, 0)),
                pl.BlockSpec((1, 1, t, w), lambda b, p, pt: (b, n_steps, 0, 0)),
                pl.BlockSpec((1, 256, PAGE_SIZE), lambda b, p, pt: (b, 0, 0)),
                pl.BlockSpec((1, 256, PAGE_SIZE), lambda b, p, pt: (b, 0, 0)),
            ] + [page_spec(r, N_KV_HEADS, ATT_HEAD_DIM) for r in range(pps)] * 2,
            out_specs=pl.BlockSpec((1, t, 1024), lambda b, p, pt: (b, 0, 0)),
            scratch_shapes=[
                pltpu.VMEM((N_KV_HEADS, KV_GROUP * t, 64), F32),
                pltpu.VMEM((N_KV_HEADS, KV_GROUP * t, 128), F32),
                pltpu.VMEM((N_KV_HEADS, KV_GROUP * t, 128), F32),
                pltpu.VMEM((N_KV_HEADS, KV_GROUP * t, 64), F32),
            ]),
        out_shape=jax.ShapeDtypeStruct((bsz, t, 1024), F32),
        compiler_params=pltpu.CompilerParams(
            dimension_semantics=("parallel", "arbitrary"), vmem_limit_bytes=VMEM_LIMIT_BYTES),
        name="dsa_sample_attend",
    )(page_table, q * ATT_HEAD_DIM ** -0.5, bias, bias, knew_t, vnew_t,
      *([pool_k] * pps), *([pool_v] * pps))


def dsa_project(x, w_main, w_tail, pos, tm):
    bsz, t = x.shape[:2]
    x2 = x.reshape(bsz * t, D_MODEL)
    main = proj(x2, w_main, tm=tm, tn=1024).reshape(bsz, t, -1)
    tail = proj(x2, w_tail, tm=tm, tn=128).reshape(bsz, t, -1)
    main, tail = rope_apply(main, tail, pos, tr=min(t, 512))
    q, k, v, qi = jnp.split(main, [1024, 1280, 1536], axis=-1)
    ki, wi = tail[..., :IDX_DIM], tail[..., IDX_DIM:IDX_DIM + N_IDX_HEADS]
    return (q, k.reshape(bsz, t, N_KV_HEADS, ATT_HEAD_DIM), v.reshape(bsz, t, N_KV_HEADS, ATT_HEAD_DIM), qi, ki, wi)


ROPE_V_BLOCKS = (10, 11)


def _rope_kernel(main_ref, tail_ref, cos_ref, sin_ref, mo_ref, to_ref):
    cos, sin = cos_ref[...], sin_ref[...]
    lane = lax.broadcasted_iota(jnp.int32, cos.shape, 1)
    first_half = lane % ATT_HEAD_DIM < ATT_HEAD_DIM // 2

    def rot(xb):
        partner = jnp.where(first_half, -pltpu.roll(xb, 128 - 32, axis=1), pltpu.roll(xb, 32, axis=1))
        return xb * cos + partner * sin

    for cb in range(main_ref.shape[2] // 128):
        xb = main_ref[0, :, cb * 128:(cb + 1) * 128]
        mo_ref[0, :, cb * 128:(cb + 1) * 128] = xb if cb in ROPE_V_BLOCKS else rot(xb)
    tb = tail_ref[0]
    to_ref[0] = jnp.where(lane < IDX_DIM, rot(tb), tb)


def rope_apply(main, tail, pos, *, tr):
    bsz, t, wm = main.shape
    inv_freq = ROPE_THETA ** (-jnp.arange(0, ATT_HEAD_DIM, 2, dtype=F32) / ATT_HEAD_DIM)
    ang = pos.astype(F32)[:, None] * inv_freq[None, :]
    cos = jnp.tile(jnp.cos(ang), (1, 4))
    sin = jnp.tile(jnp.sin(ang), (1, 4))
    return pl.pallas_call(
        _rope_kernel,
        grid=(bsz, t // tr),
        in_specs=[
            pl.BlockSpec((1, tr, wm), lambda b, i: (b, i, 0)),
            pl.BlockSpec((1, tr, 128), lambda b, i: (b, i, 0)),
            pl.BlockSpec((tr, 128), lambda b, i: (i, 0)),
            pl.BlockSpec((tr, 128), lambda b, i: (i, 0)),
        ],
        out_specs=[pl.BlockSpec((1, tr, wm), lambda b, i: (b, i, 0)), pl.BlockSpec((1, tr, 128), lambda b, i: (b, i, 0))],
        out_shape=[jax.ShapeDtypeStruct(main.shape, F32), jax.ShapeDtypeStruct(tail.shape, F32)],
        compiler_params=pltpu.CompilerParams(
            dimension_semantics=("parallel", "parallel"), vmem_limit_bytes=VMEM_LIMIT_BYTES),
        name="rope",
    )(main, tail, cos, sin)


def run_trunk(x, mem_k, mem_v, conv0, ssm0, attend, pos, p, *, tm, tm_mem):
    conv_out, ssm_out, k_out, v_out, ki_out = [], [], [], [], []
    bsz, t = x.shape[:2]
    m = bsz * t
    x = x.reshape(m, D_MODEL)
    for i in range(DEPTH):
        j = i // N_MIXERS
        g, b = p['ln_g'][i], p['ln_b'][i]
        if i % N_MIXERS == 0:
            zx = proj(x, p['w_ssm_main'][j], tm=tm, tn=1024).reshape(bsz, t, -1)
            dt_raw = proj(x, p['w_ssm_dt'][j], tm=tm, tn=128).reshape(bsz, t, -1)
            act, cs = ssm_conv(zx, conv0[j], p['ssm_conv_w'][j], p['ssm_conv_b'][j], tc=min(t, 512))
            h, ss = ssd_scan(act, zx, dt_raw, p['ssm_dt_bias'][j], p['ssm_a_log'][j], p['ssm_d'][j],
                             p['ssm_norm_w'][j], ssm0[j], c=SSM_CHUNK if t % SSM_CHUNK == 0 else t)
            conv_out.append(cs)
            ssm_out.append(ss)
            w_o = p['w_ssm_out'][j]
        else:
            q, k, v, qi, ki, wi = dsa_project(x.reshape(bsz, t, D_MODEL), p['w_att_main'][j], p['w_att_tail'][j],
                                              pos, tm)
            h = attend(j, q.reshape(bsz, t, -1), k.reshape(bsz, t, -1), v.reshape(bsz, t, -1),
                       qi.reshape(bsz, t, -1), ki, wi)
            k_out.append(k)
            v_out.append(v)
            ki_out.append(ki)
            w_o = p['w_att_out'][j]
        x = out_ln(h.reshape(m, -1), w_o, x, g[0], b[0], tm=tm_mem)
        x = mem_block(x.reshape(bsz, t, D_MODEL), p['w_mem_q'][i], mem_k, mem_v, i, p['w_mem_out'][i], g[1], b[1],
                      tm=min(t, tm_mem)).reshape(m, D_MODEL)
        x = ffn_ln(x, p['w_ffn_in'][i], p['w_ffn_out'][i], g[2], b[2], tm=tm)
    return (x.reshape(bsz, t, D_MODEL), jnp.stack(k_out), jnp.stack(v_out), jnp.stack(ki_out),
            jnp.stack(conv_out), jnp.stack(ssm_out))


def kernel(x_prompt, x_sample, cache_k, cache_v, cache_kidx, cache_mem_k, cache_mem_v, state_conv, state_ssm,
           page_table, mem_prompt, w_ssm_in, ssm_conv_w, ssm_conv_b, ssm_dt_bias, ssm_a_log, ssm_d, ssm_norm_w,
           w_ssm_out, w_att_in, w_att_out, w_mem_q, w_mem_kv, w_mem_out, w_ffn_in, w_ffn_out, ln_g, ln_b):
    zx_cols = D_INNER + CONV_DIM
    att_cols = sum(ATT_SPLITS[:4])
    pad128 = lambda w: jnp.pad(w, ((0, 0), (0, 0), (0, 128 - w.shape[-1])))
    params = {'w_ssm_main': w_ssm_in[..., :zx_cols].astype(BF16), 'w_ssm_dt': pad128(w_ssm_in[..., zx_cols:]).astype(BF16),
              'ssm_conv_w': ssm_conv_w, 'ssm_conv_b': ssm_conv_b, 'ssm_dt_bias': ssm_dt_bias,
              'ssm_a_log': ssm_a_log, 'ssm_d': ssm_d, 'ssm_norm_w': ssm_norm_w, 'w_ssm_out': w_ssm_out.astype(BF16),
              'w_att_main': w_att_in[..., :att_cols].astype(BF16), 'w_att_tail': pad128(w_att_in[..., att_cols:]).astype(BF16),
              'w_att_out': w_att_out.astype(BF16),
              'w_mem_q': w_mem_q.astype(BF16), 'w_mem_out': w_mem_out.astype(BF16),
              'w_ffn_in': w_ffn_in.astype(BF16), 'w_ffn_out': w_ffn_out.astype(BF16),
              'ln_g': ln_g, 'ln_b': ln_b}
    bp, seq = x_prompt.shape[:2]
    pos_p = jnp.arange(seq, dtype=jnp.int32)
    w_mem_kv_bf = w_mem_kv.astype(BF16)
    mem_rows = mem_prompt.reshape(bp * N_MEM, D_MODEL)
    mem_kv = jnp.stack([proj(mem_rows, w_mem_kv_bf[l], tm=bp * N_MEM, tn=512) for l in range(DEPTH)])
    memk_prompt = mem_kv[..., :D_MODEL].reshape(DEPTH, bp, N_MEM, N_MEM_HEADS, MEM_HEAD_DIM)
    memv_prompt = mem_kv[..., D_MODEL:].reshape(DEPTH, bp, N_MEM, N_MEM_HEADS, MEM_HEAD_DIM)
    n_ssm = (DEPTH + 1) // 2
    conv0 = jnp.zeros((n_ssm, bp, CONV_W - 1, CONV_DIM), x_prompt.dtype)
    ssm0 = jnp.zeros((n_ssm, bp, N_SSM_HEADS, SSM_HEAD_DIM, D_STATE), x_prompt.dtype)
    attend_p = lambda j, q, k, v, qi, ki, wi: dsa_prompt_attend(q, k, v, qi, ki, wi, tq=min(256, seq),
                                                                tk=min(512, seq))
    y_prompt, k_prompt, v_prompt, kidx_prompt, conv_prompt, ssm_prompt = run_trunk(
        x_prompt, mem_kv, mem_kv, conv0, ssm0, attend_p, pos_p, params, tm=min(1024, bp * seq), tm_mem=min(512, seq))
    past = page_table.shape[1] * PAGE_SIZE
    bs, ts = x_sample.shape[:2]
    pos_s = past + jnp.arange(ts, dtype=jnp.int32)
    attend_s = lambda j, q, k, v, qi, ki, wi: dsa_sample_attend(q, k, v, qi, ki, wi, cache_k, cache_v, cache_kidx, j,
                                                                page_table)
    y_sample, k_sample, v_sample, kidx_sample, conv_sample, ssm_sample = run_trunk(
        x_sample, cache_mem_k, cache_mem_v, state_conv, state_ssm, attend_s, pos_s, params, tm=bs * ts, tm_mem=bs * ts)
    return (y_prompt, y_sample, k_prompt, v_prompt, kidx_prompt, conv_prompt, ssm_prompt, memk_prompt, memv_prompt,
            k_sample, v_sample, kidx_sample, conv_sample, ssm_sample)
```

```python
import functools

import jax
import jax.numpy as jnp
from jax import lax
from jax.experimental import pallas as pl
from jax.experimental.pallas import tpu as pltpu

F32 = jnp.float32
BF16 = jnp.bfloat16

DEPTH = 4
N_MIXERS = 2
D_MODEL = 1024
PAGE_SIZE = 128

D_INNER = 2048
SSM_HEAD_DIM = 64
N_SSM_HEADS = 32
SSM_GROUPS = 4
HEADS_PER_GROUP = 8
D_STATE = 128
CONV_W = 4
CONV_DIM = D_INNER + 2 * SSM_GROUPS * D_STATE
SSM_CHUNK = 128

ATT_HEAD_DIM = 64
N_ATT_HEADS = 16
N_KV_HEADS = 4
KV_GROUP = 4
N_IDX_HEADS = 8
IDX_DIM = 64
TOPK_MAX = 256
ATT_SPLITS = [1024, 256, 256, 512, 64, 8]
ROPE_THETA = 10000.0

N_MEM = 256
N_MEM_HEADS = 4
MEM_HEAD_DIM = 256
FFN_HIDDEN = 2816

DEEPNORM_ALPHA = (2 * DEPTH) ** 0.25
LN_EPS = 1e-5

VMEM_LIMIT_BYTES = 48 * 1024 * 1024


def _layer_norm_rows(y, g, b):
    mu = jnp.mean(y, axis=-1, keepdims=True)
    d = y - mu
    var = jnp.mean(d * d, axis=-1, keepdims=True)
    return d * lax.rsqrt(var + LN_EPS) * g + b


def _ffn_ln_kernel(x_ref, wa_ref, wb_ref, wo_ref, g_ref, b_ref, o_ref, xb_ref, acc_ref):
    k = pl.program_id(1)

    @pl.when(k == 0)
    def _():
        xb_ref[...] = x_ref[...].astype(BF16)
        acc_ref[...] = jnp.zeros_like(acc_ref)

    xb = xb_ref[...]
    a = jnp.dot(xb, wa_ref[...], preferred_element_type=F32)
    b = jnp.dot(xb, wb_ref[...], preferred_element_type=F32)
    h = (a * jax.nn.sigmoid(a)) * b
    acc_ref[...] += jnp.dot(h.astype(BF16), wo_ref[...], preferred_element_type=F32)

    @pl.when(k == pl.num_programs(1) - 1)
    def _():
        y = DEEPNORM_ALPHA * x_ref[...] + acc_ref[...]
        o_ref[...] = _layer_norm_rows(y, g_ref[...], b_ref[...])


def ffn_ln(x, w_in, w_out, g, b, *, tm, th=256):
    m, d = x.shape
    hidden = w_out.shape[0]
    nh = hidden // th
    return pl.pallas_call(
        _ffn_ln_kernel,
        grid=(m // tm, nh),
        in_specs=[
            pl.BlockSpec((tm, d), lambda i, k: (i, 0)),
            pl.BlockSpec((d, th), lambda i, k: (0, k)),
            pl.BlockSpec((d, th), lambda i, k: (0, k + nh)),
            pl.BlockSpec((th, d), lambda i, k: (k, 0)),
            pl.BlockSpec((1, d), lambda i, k: (0, 0)),
            pl.BlockSpec((1, d), lambda i, k: (0, 0)),
        ],
        out_specs=pl.BlockSpec((tm, d), lambda i, k: (i, 0)),
        out_shape=jax.ShapeDtypeStruct((m, d), F32),
        scratch_shapes=[pltpu.VMEM((tm, d), BF16), pltpu.VMEM((tm, d), F32)],
        compiler_params=pltpu.CompilerParams(
            dimension_semantics=("parallel", "arbitrary"), vmem_limit_bytes=VMEM_LIMIT_BYTES),
        name="ffn_ln",
    )(x, w_in, w_in, w_out, g.reshape(1, d), b.reshape(1, d))


def _proj_kernel(x_ref, w_ref, o_ref, xb_ref):
    @pl.when(pl.program_id(1) == 0)
    def _():
        xb_ref[...] = x_ref[...].astype(BF16)

    o_ref[...] = jnp.dot(xb_ref[...], w_ref[...], preferred_element_type=F32)


def proj(x, w, *, tm, tn):
    m, kd = x.shape
    n = w.shape[1]
    return pl.pallas_call(
        _proj_kernel,
        grid=(m // tm, n // tn),
        in_specs=[pl.BlockSpec((tm, kd), lambda i, j: (i, 0)), pl.BlockSpec((kd, tn), lambda i, j: (0, j))],
        out_specs=pl.BlockSpec((tm, tn), lambda i, j: (i, j)),
        out_shape=jax.ShapeDtypeStruct((m, n), F32),
        scratch_shapes=[pltpu.VMEM((tm, kd), BF16)],
        compiler_params=pltpu.CompilerParams(
            dimension_semantics=("parallel", "arbitrary"), vmem_limit_bytes=VMEM_LIMIT_BYTES),
        name="proj",
    )(x, w)


def _out_ln_kernel(h_ref, w_ref, x_ref, g_ref, b_ref, o_ref):
    y = DEEPNORM_ALPHA * x_ref[...] + jnp.dot(h_ref[...].astype(BF16), w_ref[...], preferred_element_type=F32)
    o_ref[...] = _layer_norm_rows(y, g_ref[...], b_ref[...])


def out_ln(h, w, x, g, b, *, tm):
    m, kd = h.shape
    d = w.shape[1]
    return pl.pallas_call(
        _out_ln_kernel,
        grid=(m // tm,),
        in_specs=[
            pl.BlockSpec((tm, kd), lambda i: (i, 0)),
            pl.BlockSpec((kd, d), lambda i: (0, 0)),
            pl.BlockSpec((tm, d), lambda i: (i, 0)),
            pl.BlockSpec((1, d), lambda i: (0, 0)),
            pl.BlockSpec((1, d), lambda i: (0, 0)),
        ],
        out_specs=pl.BlockSpec((tm, d), lambda i: (i, 0)),
        out_shape=jax.ShapeDtypeStruct((m, d), F32),
        compiler_params=pltpu.CompilerParams(
            dimension_semantics=("parallel",), vmem_limit_bytes=VMEM_LIMIT_BYTES),
        name="out_ln",
    )(h, w, x, g.reshape(1, d), b.reshape(1, d))


def _mem_block_kernel(x_ref, wq_ref, mk_ref, mv_ref, wo_ref, g_ref, b_ref, o_ref):
    heads = [slice(h * MEM_HEAD_DIM, (h + 1) * MEM_HEAD_DIM) for h in range(N_MEM_HEADS)]
    scale = MEM_HEAD_DIM ** -0.5
    x = x_ref[0]
    q = jnp.dot(x.astype(BF16), wq_ref[...], preferred_element_type=F32).astype(BF16)
    if mk_ref.ndim == 2:
        k_of = lambda h: mk_ref[:, heads[h]].astype(BF16)
        v_of = lambda h: mv_ref[:, heads[h]].astype(BF16)
        keep = lambda h, s: s
    else:
        rows = N_MEM * N_MEM_HEADS
        k_all = mk_ref[...].reshape(rows, MEM_HEAD_DIM).astype(BF16)
        v_all = mv_ref[...].reshape(rows, MEM_HEAD_DIM).astype(BF16)
        k_of = lambda h: k_all
        v_of = lambda h: v_all
        head_of_col = lax.broadcasted_iota(jnp.int32, (1, rows), 1) % N_MEM_HEADS
        keep = lambda h, s: jnp.where(head_of_col == h, s, NEG_BIG)
    s_all = [keep(h, lax.dot_general(q[:, sl], k_of(h), _NT, preferred_element_type=F32) * scale)
             for h, sl in enumerate(heads)]
    e_all = [jnp.exp(s - jnp.max(s, axis=1, keepdims=True)) for s in s_all]
    o_heads = []
    for h in range(N_MEM_HEADS):
        pv = jnp.dot(e_all[h].astype(BF16), v_of(h), preferred_element_type=F32)
        o_heads.append((pv / jnp.sum(e_all[h], axis=1, keepdims=True)).astype(BF16))
    o = jnp.concatenate(o_heads, axis=1)
    y = DEEPNORM_ALPHA * x + jnp.dot(o, wo_ref[...], preferred_element_type=F32)
    o_ref[0] = _layer_norm_rows(y, g_ref[...], b_ref[...])


def mem_block(x, w_q, mk, mv, layer, w_o, g, b, *, tm):
    bsz, t, d = x.shape
    const = lambda shape: pl.BlockSpec(shape, lambda b, i: (0,) * len(shape))
    if mk.ndim == 5:
        mk_spec = mv_spec = pl.BlockSpec((None, None, N_MEM, N_MEM_HEADS, MEM_HEAD_DIM),
                                         lambda b, i: (layer, b, 0, 0, 0))
    else:
        mk_spec = pl.BlockSpec((None, N_MEM, d), lambda b, i: (layer, b, 0))
        mv_spec = pl.BlockSpec((None, N_MEM, d), lambda b, i: (layer, b, 1))
    return pl.pallas_call(
        _mem_block_kernel,
        grid=(bsz, t // tm),
        in_specs=[pl.BlockSpec((1, tm, d), lambda b, i: (b, i, 0)), const((d, d)), mk_spec, mv_spec,
                  const((d, d)), const((1, d)), const((1, d))],
        out_specs=pl.BlockSpec((1, tm, d), lambda b, i: (b, i, 0)),
        out_shape=jax.ShapeDtypeStruct((bsz, t, d), F32),
        compiler_params=pltpu.CompilerParams(
            dimension_semantics=("parallel", "arbitrary"), vmem_limit_bytes=VMEM_LIMIT_BYTES),
        name="mem_block",
    )(x, w_q, mk, mv, w_o, g.reshape(1, d), b.reshape(1, d))


CONV_COLS = 1024


def _conv_kernel(x_ref, st_ref, w_ref, b_ref, act_ref, last_ref, prev_ref, *, tc):
    @pl.when(pl.program_id(2) == 0)
    def _():
        prev_ref[...] = st_ref[0]

    x = x_ref[0]
    prev = prev_ref[...]
    row = lax.broadcasted_iota(jnp.int32, (8, CONV_COLS), 0)
    acc = jnp.broadcast_to(b_ref[...], x.shape)
    for s in (3, 2, 1):
        rolled = pltpu.roll(x, s, axis=0)
        top = jnp.where(row < s, pltpu.roll(prev, s, axis=0), rolled[0:8])
        shifted = top if tc == 8 else jnp.concatenate([top, rolled[8:]], axis=0)
        acc = acc + shifted * w_ref[CONV_W - 1 - s:CONV_W - s, :]
    acc = acc + x * w_ref[CONV_W - 1:CONV_W, :]
    act_ref[0] = acc * jax.nn.sigmoid(acc)
    prev_ref[...] = x[tc - 8:tc]
    last_ref[0] = x[tc - 8:tc]


def ssm_conv(zx, conv_state, conv_w, conv_b, *, tc):
    bsz, t = zx.shape[:2]
    ncb = CONV_DIM // CONV_COLS
    col0 = D_INNER // CONV_COLS
    st8 = jnp.pad(conv_state, ((0, 0), (8 - (CONV_W - 1), 0), (0, 0)))
    act, last = pl.pallas_call(
        functools.partial(_conv_kernel, tc=tc),
        grid=(bsz, ncb, t // tc),
        in_specs=[
            pl.BlockSpec((1, tc, CONV_COLS), lambda b, c, i: (b, i, col0 + c)),
            pl.BlockSpec((1, 8, CONV_COLS), lambda b, c, i: (b, 0, c)),
            pl.BlockSpec((CONV_W, CONV_COLS), lambda b, c, i: (0, c)),
            pl.BlockSpec((1, CONV_COLS), lambda b, c, i: (0, c)),
        ],
        out_specs=[
            pl.BlockSpec((1, tc, CONV_COLS), lambda b, c, i: (b, i, c)),
            pl.BlockSpec((1, 8, CONV_COLS), lambda b, c, i: (b, 0, c)),
        ],
        out_shape=[jax.ShapeDtypeStruct((bsz, t, CONV_DIM), F32), jax.ShapeDtypeStruct((bsz, 8, CONV_DIM), F32)],
        scratch_shapes=[pltpu.VMEM((8, CONV_COLS), F32)],
        compiler_params=pltpu.CompilerParams(
            dimension_semantics=("parallel", "parallel", "arbitrary"), vmem_limit_bytes=VMEM_LIMIT_BYTES),
        name="ssm_conv",
    )(zx, st8, conv_w, conv_b.reshape(1, CONV_DIM))
    return act, last[:, 8 - (CONV_W - 1):, :]


def _split3(v):
    hi = v.astype(BF16)
    r1 = v - hi.astype(F32)
    mid = r1.astype(BF16)
    lo = (r1 - mid.astype(F32)).astype(BF16)
    return hi, mid, lo


def _dot01_right(v, ones_mat):
    return sum(jnp.dot(p, ones_mat, preferred_element_type=F32) for p in _split3(v))


def _dot01_left(ones_mat, v):
    return sum(jnp.dot(ones_mat, p, preferred_element_type=F32) for p in _split3(v))


def _softplus(x):
    return jnp.maximum(x, 0.0) + jnp.log1p(jnp.exp(-jnp.abs(x)))


def _ssd_kernel(xs_ref, b_ref, c_ref, z_ref, dt_ref, dtt_ref, dtb_ref, dtbt_ref, alog_ref, alogt_ref,
                d_ref, nw_ref, h0_ref, y_ref, hT_ref, st_ref, yacc_ref, *, c):
    ci = pl.program_id(1)

    @pl.when(ci == 0)
    def _():
        st_ref[...] = h0_ref[0]

    gw = HEADS_PER_GROUP * SSM_HEAD_DIM
    dt = _softplus(dt_ref[0][:, 0:N_SSM_HEADS] + dtb_ref[...])
    dtt = _softplus(dtt_ref[0] + dtbt_ref[...])
    a = dt * -jnp.exp(alog_ref[...])
    at = dtt * -jnp.exp(alogt_ref[...])
    ri = lax.broadcasted_iota(jnp.int32, (c, c), 0)
    cj = lax.broadcasted_iota(jnp.int32, (c, c), 1)
    causal = ri >= cj
    acum = _dot01_left(jnp.where(causal, 1.0, 0.0).astype(BF16), a)
    acum_t = _dot01_right(at, jnp.where(ri <= cj, 1.0, 0.0).astype(BF16))
    a_last = acum[c - 1:c, :]
    expand = jnp.where(lax.broadcasted_iota(jnp.int32, (N_SSM_HEADS, D_INNER), 1) // SSM_HEAD_DIM
                       == lax.broadcasted_iota(jnp.int32, (N_SSM_HEADS, D_INNER), 0), 1.0, 0.0).astype(BF16)
    e_dt = _dot01_right(dt, expand)
    e_in = _dot01_right(jnp.exp(acum), expand)
    e_out = _dot01_right(jnp.exp(a_last - acum), expand)
    e_chunk = e_in[c - 1:c, :]
    xs = xs_ref[0]
    xdt = xs * e_dt
    head_of_col = lax.broadcasted_iota(jnp.int32, (c, gw), 1) // SSM_HEAD_DIM
    for g in range(SSM_GROUPS):
        gs = slice(g * gw, (g + 1) * gw)
        bg = b_ref[0][:, g * D_STATE:(g + 1) * D_STATE].astype(BF16)
        cg = c_ref[0][:, g * D_STATE:(g + 1) * D_STATE].astype(BF16)
        cb = lax.dot_general(cg, bg, _NT, preferred_element_type=F32)
        xg = xdt[:, gs]
        yg = jnp.zeros((c, gw), F32)
        for r in range(HEADS_PER_GROUP):
            h = g * HEADS_PER_GROUP + r
            seg = acum[:, h:h + 1] - acum_t[h:h + 1, :]
            m = (cb * jnp.exp(jnp.where(causal, seg, -jnp.inf))).astype(BF16)
            xm = jnp.where(head_of_col == r, xg, 0.0).astype(BF16)
            yg = yg + jnp.dot(m, xm, preferred_element_type=F32)
        state = st_ref[g]
        y_off = jnp.dot(cg, state.astype(BF16), preferred_element_type=F32) * e_in[:, gs]
        yacc_ref[:, gs] = yg + y_off + xs[:, gs] * d_ref[:, gs]
        xd = (xg * e_out[:, gs]).astype(BF16)
        st_ref[g] = state * e_chunk[:, gs] + lax.dot_general(bg, xd, (((0,), (0,)), ((), ())),
                                                             preferred_element_type=F32)
    z = z_ref[0]
    yz = yacc_ref[...] * (z * jax.nn.sigmoid(z))
    y_ref[0] = yz * lax.rsqrt(jnp.mean(yz * yz, axis=-1, keepdims=True) + LN_EPS) * nw_ref[...]

    @pl.when(ci == pl.num_programs(1) - 1)
    def _():
        hT_ref[0] = st_ref[...]


def ssd_scan(act, zx, dt_raw, dt_bias, a_log, d_skip, norm_w, ssm_state, *, c):
    bsz, t = act.shape[:2]
    gw = HEADS_PER_GROUP * SSM_HEAD_DIM
    bc_w = SSM_GROUPS * D_STATE
    h0 = ssm_state.reshape(bsz, SSM_GROUPS, HEADS_PER_GROUP, SSM_HEAD_DIM, D_STATE)
    h0 = h0.transpose(0, 1, 4, 2, 3).reshape(bsz, SSM_GROUPS, D_STATE, gw)
    dtt = jnp.swapaxes(dt_raw[..., :N_SSM_HEADS], 1, 2)
    row = lambda v: v.reshape(1, -1)
    col = lambda v: v.reshape(-1, 1)
    full = lambda shape: pl.BlockSpec(shape, lambda b, i: (0,) * len(shape))
    y, h_t = pl.pallas_call(
        functools.partial(_ssd_kernel, c=c),
        grid=(bsz, t // c),
        in_specs=[
            pl.BlockSpec((1, c, D_INNER), lambda b, i: (b, i, 0)),
            pl.BlockSpec((1, c, bc_w), lambda b, i: (b, i, D_INNER // bc_w)),
            pl.BlockSpec((1, c, bc_w), lambda b, i: (b, i, D_INNER // bc_w + 1)),
            pl.BlockSpec((1, c, D_INNER), lambda b, i: (b, i, 0)),
            pl.BlockSpec((1, c, 128), lambda b, i: (b, i, 0)),
            pl.BlockSpec((1, N_SSM_HEADS, c), lambda b, i: (b, 0, i)),
            full((1, N_SSM_HEADS)), full((N_SSM_HEADS, 1)), full((1, N_SSM_HEADS)), full((N_SSM_HEADS, 1)),
            full((1, D_INNER)), full((1, D_INNER)),
            pl.BlockSpec((1, SSM_GROUPS, D_STATE, gw), lambda b, i: (b, 0, 0, 0)),
        ],
        out_specs=[
            pl.BlockSpec((1, c, D_INNER), lambda b, i: (b, i, 0)),
            pl.BlockSpec((1, SSM_GROUPS, D_STATE, gw), lambda b, i: (b, 0, 0, 0)),
        ],
        out_shape=[jax.ShapeDtypeStruct((bsz, t, D_INNER), F32),
                   jax.ShapeDtypeStruct((bsz, SSM_GROUPS, D_STATE, gw), F32)],
        scratch_shapes=[pltpu.VMEM((SSM_GROUPS, D_STATE, gw), F32), pltpu.VMEM((c, D_INNER), F32)],
        compiler_params=pltpu.CompilerParams(
            dimension_semantics=("parallel", "arbitrary"), vmem_limit_bytes=VMEM_LIMIT_BYTES),
        name="ssd_scan",
    )(act, act, act, zx, dt_raw, dtt, row(dt_bias), col(dt_bias), row(a_log), col(a_log),
      row(jnp.repeat(d_skip, SSM_HEAD_DIM)), row(norm_w), h0)
    h_t = h_t.reshape(bsz, SSM_GROUPS, D_STATE, HEADS_PER_GROUP, SSM_HEAD_DIM)
    return y, h_t.transpose(0, 1, 3, 4, 2).reshape(bsz, N_SSM_HEADS, SSM_HEAD_DIM, D_STATE)


INT_MIN = -2 ** 31
NEG_BIG = -1e30
_NT = (((1,), (1,)), ((), ()))


def _float_key(x):
    x = jnp.where(x == 0.0, 0.0, x)
    bits = lax.bitcast_convert_type(x, jnp.int32)
    return bits ^ ((bits >> 31) & 0x7FFFFFFF)


def _dsa_prompt_kernel(qt_ref, qit_ref, wit_ref, k_ref, vxt_ref, ki_ref, o_ref, key_ref, m_ref, acc_ref,
                         s_ref, p_ref, *, tq, tk, topk):
    i = pl.program_id(1)
    nkb = ((i + 1) * tq + tk - 1) // tk
    gq = KV_GROUP * tq
    rep8 = lambda v: jnp.broadcast_to(v, (8, tq))
    as3 = lambda x: x.reshape(tk // 8, 8, tq)
    w_heads = wit_ref[0, 0] * (N_IDX_HEADS ** -0.5)
    q_pos = i * tq + lax.broadcasted_iota(jnp.int32, (tk, tq), 1)

    def score_block(kb, c):
        off = pl.multiple_of(kb * tk, tk)
        s = jnp.dot(ki_ref[0, pl.ds(off, tk), :], qit_ref[0, 0], preferred_element_type=F32)
        acc = jnp.zeros((tk, tq), F32)
        for h in range(N_IDX_HEADS):
            acc = acc + w_heads[h:h + 1, :] * jnp.maximum(s[:, h * tq:(h + 1) * tq], 0.0)
        kpos = off + lax.broadcasted_iota(jnp.int32, (tk, tq), 0)
        key_ref[kb] = jnp.where(kpos <= q_pos, _float_key(acc), INT_MIN)
        return c

    lax.fori_loop(0, nkb, score_block, 0)

    def count(cand, strict):
        def body(kb, accs):
            blk = key_ref[kb]
            accs = list(accs)
            for r in range(tk // 8):
                part = blk[r * 8:(r + 1) * 8, :]
                hit = (part > cand) if strict else (part >= cand)
                accs[r % 4] = accs[r % 4] + jnp.where(hit, 1.0, 0.0)
            return tuple(accs)
        a0, a1, a2, a3 = lax.fori_loop(0, nkb, body, (jnp.zeros((8, tq), F32),) * 4)
        return jnp.sum((a0 + a1) + (a2 + a3), axis=0, keepdims=True)

    kf = float(topk)
    zero = jnp.zeros((8, tq), jnp.int32)
    thr = jnp.where(rep8(count(zero, False)) >= kf, zero, jnp.full((8, tq), INT_MIN, jnp.int32))

    def bit_step(it, thr):
        cand = thr | jnp.left_shift(jnp.int32(1), 30 - it)
        return jnp.where(rep8(count(cand, False)) >= kf, cand, thr)

    thr = lax.fori_loop(0, 31, bit_step, thr)
    need = jnp.where(thr == INT_MIN, 0.0, kf - rep8(count(thr, True)))

    m_ref[...] = jnp.full(m_ref.shape, NEG_BIG, F32)
    acc_ref[...] = jnp.zeros(acc_ref.shape, F32)
    lower = jnp.where(lax.broadcasted_iota(jnp.int32, (tk, tk), 1) < lax.broadcasted_iota(jnp.int32, (tk, tk), 0),
                      1.0, 0.0).astype(BF16)

    def attend_block(kb, eq_before):
        off = pl.multiple_of(kb * tk, tk)
        key3 = as3(key_ref[kb])
        eq3 = key3 == thr[None]
        eqf = jnp.where(eq3, 1.0, 0.0)
        rank3 = as3(jnp.dot(lower, eqf.reshape(tk, tq).astype(BF16), preferred_element_type=F32)) + eq_before[None]
        bias = jnp.where(key3 > thr[None], 0.0,
                         jnp.where(eq3, jnp.where(rank3 < need[None], 0.0, NEG_BIG), NEG_BIG)).reshape(tk, tq)
        bias4 = jnp.concatenate([bias] * KV_GROUP, axis=1)
        col_max = []
        for j in range(N_KV_HEADS):
            kblk = k_ref[0, pl.ds(off, tk), j * 64:(j + 1) * 64]
            s = jnp.dot(kblk, qt_ref[0, 0, j], preferred_element_type=F32) + bias4
            s_ref[j] = s
            col_max.append(jnp.max(s, axis=0, keepdims=True))
        alpha = []
        for j in range(N_KV_HEADS):
            m_old = m_ref[j]
            m_new = jnp.maximum(m_old, col_max[j])
            alpha.append(jnp.exp(m_old - m_new))
            p_ref[j] = jnp.exp(s_ref[j] - m_new).astype(BF16)
            m_ref[j] = m_new
        for j in range(N_KV_HEADS):
            pv = jnp.dot(vxt_ref[0, j, kb], p_ref[j], preferred_element_type=F32)
            acc_ref[j] = alpha[j] * acc_ref[j] + pv
        return eq_before + rep8(jnp.sum(jnp.sum(eqf, axis=0), axis=0, keepdims=True))

    lax.fori_loop(0, nkb, attend_block, jnp.zeros((8, tq), F32))

    for j in range(N_KV_HEADS):
        a = acc_ref[j]
        o_t = a[0:64, :] / a[64:65, :]
        for g in range(KV_GROUP):
            h = j * KV_GROUP + g
            o_ref[0, :, h * 64:(h + 1) * 64] = o_t[:, g * tq:(g + 1) * tq].T


def dsa_prompt_attend(q, k, v, qi, ki, wi, *, tq, tk):
    bsz, t = q.shape[:2]
    nq, nk = t // tq, t // tk
    topk = min(TOPK_MAX, t // 4)
    qt = (q * ATT_HEAD_DIM ** -0.5).astype(BF16).reshape(bsz, nq, tq, N_KV_HEADS, KV_GROUP, 64)
    qt = qt.transpose(0, 1, 3, 5, 4, 2).reshape(bsz, nq, N_KV_HEADS, 64, KV_GROUP * tq)
    qit = (qi * IDX_DIM ** -0.5).astype(BF16).reshape(bsz, nq, tq, N_IDX_HEADS, 64)
    qit = qit.transpose(0, 1, 4, 3, 2).reshape(bsz, nq, 64, N_IDX_HEADS * tq)
    wit = wi.reshape(bsz, nq, tq, N_IDX_HEADS).transpose(0, 1, 3, 2)
    vt = v.astype(BF16).reshape(bsz, nk, tk, N_KV_HEADS, 64).transpose(0, 3, 1, 4, 2)
    vxt = jnp.concatenate([vt, jnp.ones_like(vt)], axis=3)
    kern = functools.partial(_dsa_prompt_kernel, tq=tq, tk=tk, topk=topk)
    return pl.pallas_call(
        kern,
        grid=(bsz, nq),
        in_specs=[
            pl.BlockSpec((1, 1, N_KV_HEADS, 64, KV_GROUP * tq), lambda b, i: (b, i, 0, 0, 0)),
            pl.BlockSpec((1, 1, 64, N_IDX_HEADS * tq), lambda b, i: (b, i, 0, 0)),
            pl.BlockSpec((1, 1, N_IDX_HEADS, tq), lambda b, i: (b, i, 0, 0)),
            pl.BlockSpec((1, t, 256), lambda b, i: (b, 0, 0), pipeline_mode=pl.Buffered(1)),
            pl.BlockSpec((1, N_KV_HEADS, nk, 128, tk), lambda b, i: (b, 0, 0, 0, 0), pipeline_mode=pl.Buffered(1)),
            pl.BlockSpec((1, t, IDX_DIM), lambda b, i: (b, 0, 0), pipeline_mode=pl.Buffered(1)),
        ],
        out_specs=pl.BlockSpec((1, tq, 1024), lambda b, i: (b, i, 0)),
        out_shape=jax.ShapeDtypeStruct((bsz, t, 1024), F32),
        scratch_shapes=[
            pltpu.VMEM((nk, tk, tq), jnp.int32),
            pltpu.VMEM((N_KV_HEADS, 1, KV_GROUP * tq), F32),
            pltpu.VMEM((N_KV_HEADS, 128, KV_GROUP * tq), F32),
            pltpu.VMEM((N_KV_HEADS, tk, KV_GROUP * tq), F32),
            pltpu.VMEM((N_KV_HEADS, tk, KV_GROUP * tq), BF16),
        ],
        compiler_params=pltpu.CompilerParams(
            dimension_semantics=("parallel", "arbitrary"), vmem_limit_bytes=VMEM_LIMIT_BYTES),
        name="dsa_prompt_attend",
    )(qt, qit, wit, k.astype(BF16), vxt, ki.astype(BF16))


MAX_PAGES_PER_STEP = 32


def _sample_select_kernel(pt_ref, qi_ref, wi_ref, kinew_ref, *rest, t, n_steps, topk, pps):
    ki_refs = rest[:pps]
    bias_ref, qis_ref, wib_ref, key_ref = rest[pps:]
    p = pl.program_id(1)
    w = pps * PAGE_SIZE
    nl = w // 128

    @pl.when(p == 0)
    def _():
        for h in range(N_IDX_HEADS):
            qis_ref[h * t:(h + 1) * t, :] = qi_ref[0, :, h * 64:(h + 1) * 64]
            wib_ref[h] = jnp.broadcast_to(wi_ref[0, :, h:h + 1] * (N_IDX_HEADS ** -0.5), (t, 128))

    def scores(s):
        acc = jnp.zeros((t, s.shape[1]), F32)
        for h in range(N_IDX_HEADS):
            wfull = jnp.concatenate([wib_ref[h]] * (s.shape[1] // 128), axis=1)
            acc = acc + wfull * jnp.maximum(s[h * t:(h + 1) * t, :], 0.0)
        return acc

    ki_t = jnp.concatenate([r[...] for r in ki_refs], axis=1).astype(BF16)
    key_ref[p] = _float_key(scores(jnp.dot(qis_ref[...].astype(BF16), ki_t, preferred_element_type=F32)))

    @pl.when(p == n_steps - 1)
    def _():
        knew = _float_key(scores(lax.dot_general(qis_ref[...].astype(BF16), kinew_ref[0].astype(BF16), _NT,
                                                 preferred_element_type=F32)))
        n_idx = lax.broadcasted_iota(jnp.int32, (t, 128), 1)
        r_idx = lax.broadcasted_iota(jnp.int32, (t, 128), 0)
        knew = jnp.where(n_idx <= r_idx, knew, INT_MIN)
        key_ref[n_steps] = jnp.concatenate([knew, jnp.full((t, w - 128), INT_MIN, jnp.int32)], axis=1)

        def count(cand, strict):
            def body(blk, acc):
                kb = key_ref[blk]
                for c in range(nl):
                    part = kb[:, c * 128:(c + 1) * 128]
                    hit = (part > cand) if strict else (part >= cand)
                    acc = acc + jnp.where(hit, 1.0, 0.0)
                return acc
            acc = lax.fori_loop(0, n_steps + 1, body, jnp.zeros((t, 128), F32))
            return jnp.sum(acc, axis=1, keepdims=True)

        kf = float(topk)
        zero = jnp.zeros((t, 128), jnp.int32)
        thr = jnp.where(count(zero, False) >= kf, zero, jnp.full((t, 128), INT_MIN, jnp.int32))

        def bit_step(it, thr):
            cand = thr | jnp.left_shift(jnp.int32(1), 30 - it)
            return jnp.where(count(cand, False) >= kf, cand, thr)

        thr = lax.fori_loop(0, 31, bit_step, thr)
        need = jnp.where(thr == INT_MIN, 0.0, kf - count(thr, True))
        floor_thr = jnp.where(thr == INT_MIN, INT_MIN + 1, thr)
        untied = jnp.where((thr == INT_MIN) | (count(thr, False) <= kf), 1.0, 0.0)
        no_ties = jnp.min(untied) > 0.0

        @pl.when(no_ties)
        def _():
            wide = jnp.concatenate([floor_thr] * nl, axis=1)

            def emit_simple(blk, c):
                bias_ref[0, blk] = jnp.where(key_ref[blk] >= wide, 0.0, NEG_BIG)
                return c

            lax.fori_loop(0, n_steps + 1, emit_simple, 0)

        upper = jnp.where(lax.broadcasted_iota(jnp.int32, (128, 128), 0)
                          < lax.broadcasted_iota(jnp.int32, (128, 128), 1), 1.0, 0.0).astype(BF16)

        def emit(blk, eq_before):
            kb = key_ref[blk]
            parts = []
            for c in range(nl):
                part = kb[:, c * 128:(c + 1) * 128]
                eq = part == thr
                eqf = jnp.where(eq, 1.0, 0.0)
                rank = jnp.dot(eqf.astype(BF16), upper, preferred_element_type=F32) + eq_before
                parts.append(jnp.where(part > thr, 0.0,
                                       jnp.where(eq, jnp.where(rank < need, 0.0, NEG_BIG), NEG_BIG)))
                eq_before = eq_before + jnp.sum(eqf, axis=1, keepdims=True)
            bias_ref[0, blk] = jnp.concatenate(parts, axis=1)
            return eq_before

        @pl.when(jnp.logical_not(no_ties))
        def _():
            lax.fori_loop(0, n_steps + 1, emit, jnp.zeros((t, 128), F32))


def _sample_attend_kernel(pt_ref, q_ref, bias_ref, biasnew_ref, knew_ref, vnew_ref, *rest, t, n_steps, pps):
    k_refs = rest[:pps]
    v_refs = rest[pps:2 * pps]
    o_ref, qs_ref, m_ref, l_ref, acc_ref = rest[2 * pps:]
    p = pl.program_id(1)
    rows = KV_GROUP * t

    @pl.when(p == 0)
    def _():
        for h in range(N_ATT_HEADS):
            j, g = divmod(h, KV_GROUP)
            qs_ref[j, g * t:(g + 1) * t, :] = q_ref[0, :, h * 64:(h + 1) * 64]
        m_ref[...] = jnp.full(m_ref.shape, NEG_BIG, F32)
        l_ref[...] = jnp.zeros(l_ref.shape, F32)
        acc_ref[...] = jnp.zeros(acc_ref.shape, F32)

    def update(k_t, v_t, bias):
        bias4 = jnp.concatenate([bias] * KV_GROUP, axis=0)
        s_all = [jnp.dot(qs_ref[j].astype(BF16), k_t(j).astype(BF16), preferred_element_type=F32) + bias4
                 for j in range(N_KV_HEADS)]
        alphas, probs = [], []
        for j in range(N_KV_HEADS):
            m_old = m_ref[j]
            m_new = jnp.maximum(m_old, jnp.max(s_all[j], axis=1, keepdims=True))
            alpha = jnp.exp(m_old - m_new)
            pr = jnp.exp(s_all[j] - m_new[:, 0:1])
            l_ref[j] = alpha * l_ref[j] + jnp.sum(pr, axis=1, keepdims=True)
            m_ref[j] = m_new
            alphas.append(alpha)
            probs.append(pr.astype(BF16))
        for j in range(N_KV_HEADS):
            pv = lax.dot_general(probs[j], v_t(j).astype(BF16), _NT, preferred_element_type=F32)
            acc_ref[j] = alphas[j][:, 0:64] * acc_ref[j] + pv

    update(lambda j: jnp.concatenate([r[j] for r in k_refs], axis=1),
           lambda j: jnp.concatenate([r[j] for r in v_refs], axis=1), bias_ref[0, 0])

    @pl.when(p == n_steps - 1)
    def _():
        update(lambda j: knew_ref[0, j * 64:(j + 1) * 64, :], lambda j: vnew_ref[0, j * 64:(j + 1) * 64, :],
               biasnew_ref[0, 0, :, 0:128])
        for h in range(N_ATT_HEADS):
            j, g = divmod(h, KV_GROUP)
            sl = slice(g * t, (g + 1) * t)
            o_ref[0, :, h * 64:(h + 1) * 64] = acc_ref[j, sl, :] / l_ref[j, sl, 0:64]


def dsa_sample_attend(q, k, v, qi, ki, wi, cache_k, cache_v, cache_kidx, layer, page_table):
    bsz, t = q.shape[:2]
    n_pages = page_table.shape[1]
    past = n_pages * PAGE_SIZE
    pps = min(MAX_PAGES_PER_STEP, n_pages)
    assert n_pages % pps == 0, (n_pages, pps)
    n_steps = n_pages // pps
    w = pps * PAGE_SIZE
    topk = min(TOPK_MAX, (past + t) // 4)
    pad = lambda a: jnp.pad(a, ((0, 0), (0, PAGE_SIZE - t), (0, 0)))
    pool_k = cache_k.transpose(0, 1, 3, 4, 2)
    pool_v = cache_v.transpose(0, 1, 3, 4, 2)
    pool_ki = cache_kidx.transpose(0, 1, 3, 2)
    knew_t = jnp.swapaxes(pad(k), 1, 2)
    vnew_t = jnp.swapaxes(pad(v), 1, 2)

    def page_spec(r, *major):
        zeros = (0,) * (1 + len(major))
        return pl.BlockSpec((None, None) + major + (PAGE_SIZE,),
                            lambda b, p, pt: (layer, pt[b, p * pps + r]) + zeros)

    bias = pl.pallas_call(
        functools.partial(_sample_select_kernel, t=t, n_steps=n_steps, topk=topk, pps=pps),
        grid_spec=pltpu.PrefetchScalarGridSpec(
            num_scalar_prefetch=1,
            grid=(bsz, n_steps),
            in_specs=[
                pl.BlockSpec((1, t, 512), lambda b, p, pt: (b, 0, 0)),
                pl.BlockSpec((1, t, N_IDX_HEADS), lambda b, p, pt: (b, 0, 0)),
                pl.BlockSpec((1, PAGE_SIZE, IDX_DIM), lambda b, p, pt: (b, 0, 0)),
            ] + [page_spec(r, IDX_DIM) for r in range(pps)],
            out_specs=pl.BlockSpec((1, n_steps + 1, t, w), lambda b, p, pt: (b, 0, 0, 0)),
            scratch_shapes=[
                pltpu.VMEM((N_IDX_HEADS * t, 64), F32),
                pltpu.VMEM((N_IDX_HEADS, t, 128), F32),
                pltpu.VMEM((n_steps + 1, t, w), jnp.int32),
            ]),
        out_shape=jax.ShapeDtypeStruct((bsz, n_steps + 1, t, w), F32),
        compiler_params=pltpu.CompilerParams(
            dimension_semantics=("parallel", "arbitrary"), vmem_limit_bytes=VMEM_LIMIT_BYTES),
        name="dsa_sample_select",
    )(page_table, qi * IDX_DIM ** -0.5, wi, pad(ki), *([pool_ki] * pps))

    return pl.pallas_call(
        functools.partial(_sample_attend_kernel, t=t, n_steps=n_steps, pps=pps),
        grid_spec=pltpu.PrefetchScalarGridSpec(
            num_scalar_prefetch=1,
            grid=(bsz, n_steps),
            in_specs=[
                pl.BlockSpec((1, t, 1024), lambda b, p, pt: (b, 0, 0)),
                pl.BlockSpec((1, 1, t, w), lambda b, p, pt: (b, p, 0, 0)),
                pl.BlockSpec((1, 1, t, w), lambda b, p, pt: (b, n_steps, 0, 0)),
                pl.BlockSpec((1, 256, PAGE_SIZE), lambda b, p, pt: (b, 0, 0)),
                pl.BlockSpec((1, 256, PAGE_SIZE), lambda b, p, pt: (b, 0, 0)),
            ] + [page_spec(r, N_KV_HEADS, ATT_HEAD_DIM) for r in range(pps)] * 2,
            out_specs=pl.BlockSpec((1, t, 1024), lambda b, p, pt: (b, 0, 0)),
            scratch_shapes=[
                pltpu.VMEM((N_KV_HEADS, KV_GROUP * t, 64), F32),
                pltpu.VMEM((N_KV_HEADS, KV_GROUP * t, 128), F32),
                pltpu.VMEM((N_KV_HEADS, KV_GROUP * t, 128), F32),
                pltpu.VMEM((N_KV_HEADS, KV_GROUP * t, 64), F32),
            ]),
        out_shape=jax.ShapeDtypeStruct((bsz, t, 1024), F32),
        compiler_params=pltpu.CompilerParams(
            dimension_semantics=("parallel", "arbitrary"), vmem_limit_bytes=VMEM_LIMIT_BYTES),
        name="dsa_sample_attend",
    )(page_table, q * ATT_HEAD_DIM ** -0.5, bias, bias, knew_t, vnew_t,
      *([pool_k] * pps), *([pool_v] * pps))


def dsa_project(x, w_main, w_tail, pos, tm):
    bsz, t = x.shape[:2]
    x2 = x.reshape(bsz * t, D_MODEL)
    main = proj(x2, w_main, tm=tm, tn=1024).reshape(bsz, t, -1)
    tail = proj(x2, w_tail, tm=tm, tn=128).reshape(bsz, t, -1)
    main, tail = rope_apply(main, tail, pos, tr=min(t, 512))
    q, k, v, qi = jnp.split(main, [1024, 1280, 1536], axis=-1)
    ki, wi = tail[..., :IDX_DIM], tail[..., IDX_DIM:IDX_DIM + N_IDX_HEADS]
    return (q, k.reshape(bsz, t, N_KV_HEADS, ATT_HEAD_DIM), v.reshape(bsz, t, N_KV_HEADS, ATT_HEAD_DIM), qi, ki, wi)


ROPE_V_BLOCKS = (10, 11)


def _rope_kernel(main_ref, tail_ref, cos_ref, sin_ref, mo_ref, to_ref):
    cos, sin = cos_ref[...], sin_ref[...]
    lane = lax.broadcasted_iota(jnp.int32, cos.shape, 1)
    first_half = lane % ATT_HEAD_DIM < ATT_HEAD_DIM // 2

    def rot(xb):
        partner = jnp.where(first_half, -pltpu.roll(xb, 128 - 32, axis=1), pltpu.roll(xb, 32, axis=1))
        return xb * cos + partner * sin

    for cb in range(main_ref.shape[2] // 128):
        xb = main_ref[0, :, cb * 128:(cb + 1) * 128]
        mo_ref[0, :, cb * 128:(cb + 1) * 128] = xb if cb in ROPE_V_BLOCKS else rot(xb)
    tb = tail_ref[0]
    to_ref[0] = jnp.where(lane < IDX_DIM, rot(tb), tb)


def rope_apply(main, tail, pos, *, tr):
    bsz, t, wm = main.shape
    inv_freq = ROPE_THETA ** (-jnp.arange(0, ATT_HEAD_DIM, 2, dtype=F32) / ATT_HEAD_DIM)
    ang = pos.astype(F32)[:, None] * inv_freq[None, :]
    cos = jnp.tile(jnp.cos(ang), (1, 4))
    sin = jnp.tile(jnp.sin(ang), (1, 4))
    return pl.pallas_call(
        _rope_kernel,
        grid=(bsz, t // tr),
        in_specs=[
            pl.BlockSpec((1, tr, wm), lambda b, i: (b, i, 0)),
            pl.BlockSpec((1, tr, 128), lambda b, i: (b, i, 0)),
            pl.BlockSpec((tr, 128), lambda b, i: (i, 0)),
            pl.BlockSpec((tr, 128), lambda b, i: (i, 0)),
        ],
        out_specs=[pl.BlockSpec((1, tr, wm), lambda b, i: (b, i, 0)), pl.BlockSpec((1, tr, 128), lambda b, i: (b, i, 0))],
        out_shape=[jax.ShapeDtypeStruct(main.shape, F32), jax.ShapeDtypeStruct(tail.shape, F32)],
        compiler_params=pltpu.CompilerParams(
            dimension_semantics=("parallel", "parallel"), vmem_limit_bytes=VMEM_LIMIT_BYTES),
        name="rope",
    )(main, tail, cos, sin)


def run_trunk(x, mem_k, mem_v, conv0, ssm0, attend, pos, p, *, tm, tm_mem):
    conv_out, ssm_out, k_out, v_out, ki_out = [], [], [], [], []
    bsz, t = x.shape[:2]
    m = bsz * t
    x = x.reshape(m, D_MODEL)
    for i in range(DEPTH):
        j = i // N_MIXERS
        g, b = p['ln_g'][i], p['ln_b'][i]
        if i % N_MIXERS == 0:
            zx = proj(x, p['w_ssm_main'][j], tm=tm, tn=1024).reshape(bsz, t, -1)
            dt_raw = proj(x, p['w_ssm_dt'][j], tm=tm, tn=128).reshape(bsz, t, -1)
            act, cs = ssm_conv(zx, conv0[j], p['ssm_conv_w'][j], p['ssm_conv_b'][j], tc=min(t, 512))
            h, ss = ssd_scan(act, zx, dt_raw, p['ssm_dt_bias'][j], p['ssm_a_log'][j], p['ssm_d'][j],
                             p['ssm_norm_w'][j], ssm0[j], c=SSM_CHUNK if t % SSM_CHUNK == 0 else t)
            conv_out.append(cs)
            ssm_out.append(ss)
            w_o = p['w_ssm_out'][j]
        else:
            q, k, v, qi, ki, wi = dsa_project(x.reshape(bsz, t, D_MODEL), p['w_att_main'][j], p['w_att_tail'][j],
                                              pos, tm)
            h = attend(j, q.reshape(bsz, t, -1), k.reshape(bsz, t, -1), v.reshape(bsz, t, -1),
                       qi.reshape(bsz, t, -1), ki, wi)
            k_out.append(k)
            v_out.append(v)
            ki_out.append(ki)
            w_o = p['w_att_out'][j]
        x = out_ln(h.reshape(m, -1), w_o, x, g[0], b[0], tm=tm_mem)
        x = mem_block(x.reshape(bsz, t, D_MODEL), p['w_mem_q'][i], mem_k, mem_v, i, p['w_mem_out'][i], g[1], b[1],
                      tm=min(t, tm_mem)).reshape(m, D_MODEL)
        x = ffn_ln(x, p['w_ffn_in'][i], p['w_ffn_out'][i], g[2], b[2], tm=tm)
    return (x.reshape(bsz, t, D_MODEL), jnp.stack(k_out), jnp.stack(v_out), jnp.stack(ki_out),
            jnp.stack(conv_out), jnp.stack(ssm_out))


def kernel(x_prompt, x_sample, cache_k, cache_v, cache_kidx, cache_mem_k, cache_mem_v, state_conv, state_ssm,
           page_table, mem_prompt, w_ssm_in, ssm_conv_w, ssm_conv_b, ssm_dt_bias, ssm_a_log, ssm_d, ssm_norm_w,
           w_ssm_out, w_att_in, w_att_out, w_mem_q, w_mem_kv, w_mem_out, w_ffn_in, w_ffn_out, ln_g, ln_b):
    zx_cols = D_INNER + CONV_DIM
    att_cols = sum(ATT_SPLITS[:4])
    pad128 = lambda w: jnp.pad(w, ((0, 0), (0, 0), (0, 128 - w.shape[-1])))
    params = {'w_ssm_main': w_ssm_in[..., :zx_cols].astype(BF16), 'w_ssm_dt': pad128(w_ssm_in[..., zx_cols:]).astype(BF16),
              'ssm_conv_w': ssm_conv_w, 'ssm_conv_b': ssm_conv_b, 'ssm_dt_bias': ssm_dt_bias,
              'ssm_a_log': ssm_a_log, 'ssm_d': ssm_d, 'ssm_norm_w': ssm_norm_w, 'w_ssm_out': w_ssm_out.astype(BF16),
              'w_att_main': w_att_in[..., :att_cols].astype(BF16), 'w_att_tail': pad128(w_att_in[..., att_cols:]).astype(BF16),
              'w_att_out': w_att_out.astype(BF16),
              'w_mem_q': w_mem_q.astype(BF16), 'w_mem_out': w_mem_out.astype(BF16),
              'w_ffn_in': w_ffn_in.astype(BF16), 'w_ffn_out': w_ffn_out.astype(BF16),
              'ln_g': ln_g, 'ln_b': ln_b}
    bp, seq = x_prompt.shape[:2]
    pos_p = jnp.arange(seq, dtype=jnp.int32)
    w_mem_kv_bf = w_mem_kv.astype(BF16)
    mem_rows = mem_prompt.reshape(bp * N_MEM, D_MODEL)
    mem_kv = jnp.stack([proj(mem_rows, w_mem_kv_bf[l], tm=bp * N_MEM, tn=512) for l in range(DEPTH)])
    memk_prompt = mem_kv[..., :D_MODEL].reshape(DEPTH, bp, N_MEM, N_MEM_HEADS, MEM_HEAD_DIM)
    memv_prompt = mem_kv[..., D_MODEL:].reshape(DEPTH, bp, N_MEM, N_MEM_HEADS, MEM_HEAD_DIM)
    n_ssm = (DEPTH + 1) // 2
    conv0 = jnp.zeros((n_ssm, bp, CONV_W - 1, CONV_DIM), x_prompt.dtype)
    ssm0 = jnp.zeros((n_ssm, bp, N_SSM_HEADS, SSM_HEAD_DIM, D_STATE), x_prompt.dtype)
    attend_p = lambda j, q, k, v, qi, ki, wi: dsa_prompt_attend(q, k, v, qi, ki, wi, tq=min(256, seq),
                                                                tk=min(512, seq))
    y_prompt, k_prompt, v_prompt, kidx_prompt, conv_prompt, ssm_prompt = run_trunk(
        x_prompt, mem_kv, mem_kv, conv0, ssm0, attend_p, pos_p, params, tm=min(1024, bp * seq), tm_mem=min(512, seq))
    past = page_table.shape[1] * PAGE_SIZE
    bs, ts = x_sample.shape[:2]
    pos_s = past + jnp.arange(ts, dtype=jnp.int32)
    attend_s = lambda j, q, k, v, qi, ki, wi: dsa_sample_attend(q, k, v, qi, ki, wi, cache_k, cache_v, cache_kidx, j,
                                                                page_table)
    y_sample, k_sample, v_sample, kidx_sample, conv_sample, ssm_sample = run_trunk(
        x_sample, cache_mem_k, cache_mem_v, state_conv, state_ssm, attend_s, pos_s, params, tm=bs * ts, tm_mem=bs * ts)
    return (y_prompt, y_sample, k_prompt, v_prompt, kidx_prompt, conv_prompt, ssm_prompt, memk_prompt, memv_prompt,
            k_sample, v_sample, kidx_sample, conv_sample, ssm_sample)
```

```python
import functools

import jax
import jax.numpy as jnp
from jax import lax
from jax.experimental import pallas as pl
from jax.experimental.pallas import tpu as pltpu

F32 = jnp.float32
BF16 = jnp.bfloat16

DEPTH = 4
N_MIXERS = 2
D_MODEL = 1024
PAGE_SIZE = 128

D_INNER = 2048
SSM_HEAD_DIM = 64
N_SSM_HEADS = 32
SSM_GROUPS = 4
HEADS_PER_GROUP = 8
D_STATE = 128
CONV_W = 4
CONV_DIM = D_INNER + 2 * SSM_GROUPS * D_STATE
SSM_CHUNK = 128

ATT_HEAD_DIM = 64
N_ATT_HEADS = 16
N_KV_HEADS = 4
KV_GROUP = 4
N_IDX_HEADS = 8
IDX_DIM = 64
TOPK_MAX = 256
ATT_SPLITS = [1024, 256, 256, 512, 64, 8]
ROPE_THETA = 10000.0

N_MEM = 256
N_MEM_HEADS = 4
MEM_HEAD_DIM = 256
FFN_HIDDEN = 2816

DEEPNORM_ALPHA = (2 * DEPTH) ** 0.25
LN_EPS = 1e-5

VMEM_LIMIT_BYTES = 48 * 1024 * 1024


def _layer_norm_rows(y, g, b):
    mu = jnp.mean(y, axis=-1, keepdims=True)
    d = y - mu
    var = jnp.mean(d * d, axis=-1, keepdims=True)
    return d * lax.rsqrt(var + LN_EPS) * g + b


def _ffn_ln_kernel(x_ref, wa_ref, wb_ref, wo_ref, g_ref, b_ref, o_ref, xb_ref, acc_ref):
    k = pl.program_id(1)

    @pl.when(k == 0)
    def _():
        xb_ref[...] = x_ref[...].astype(BF16)
        acc_ref[...] = jnp.zeros_like(acc_ref)

    xb = xb_ref[...]
    a = jnp.dot(xb, wa_ref[...], preferred_element_type=F32)
    b = jnp.dot(xb, wb_ref[...], preferred_element_type=F32)
    h = (a * jax.nn.sigmoid(a)) * b
    acc_ref[...] += jnp.dot(h.astype(BF16), wo_ref[...], preferred_element_type=F32)

    @pl.when(k == pl.num_programs(1) - 1)
    def _():
        y = DEEPNORM_ALPHA * x_ref[...] + acc_ref[...]
        o_ref[...] = _layer_norm_rows(y, g_ref[...], b_ref[...])


def ffn_ln(x, w_in, w_out, g, b, *, tm, th=256):
    m, d = x.shape
    hidden = w_out.shape[0]
    nh = hidden // th
    return pl.pallas_call(
        _ffn_ln_kernel,
        grid=(m // tm, nh),
        in_specs=[
            pl.BlockSpec((tm, d), lambda i, k: (i, 0)),
            pl.BlockSpec((d, th), lambda i, k: (0, k)),
            pl.BlockSpec((d, th), lambda i, k: (0, k + nh)),
            pl.BlockSpec((th, d), lambda i, k: (k, 0)),
            pl.BlockSpec((1, d), lambda i, k: (0, 0)),
            pl.BlockSpec((1, d), lambda i, k: (0, 0)),
        ],
        out_specs=pl.BlockSpec((tm, d), lambda i, k: (i, 0)),
        out_shape=jax.ShapeDtypeStruct((m, d), F32),
        scratch_shapes=[pltpu.VMEM((tm, d), BF16), pltpu.VMEM((tm, d), F32)],
        compiler_params=pltpu.CompilerParams(
            dimension_semantics=("parallel", "arbitrary"), vmem_limit_bytes=VMEM_LIMIT_BYTES),
        name="ffn_ln",
    )(x, w_in, w_in, w_out, g.reshape(1, d), b.reshape(1, d))


def _proj_kernel(x_ref, w_ref, o_ref, xb_ref):
    @pl.when(pl.program_id(1) == 0)
    def _():
        xb_ref[...] = x_ref[...].astype(BF16)

    o_ref[...] = jnp.dot(xb_ref[...], w_ref[...], preferred_element_type=F32)


def proj(x, w, *, tm, tn):
    m, kd = x.shape
    n = w.shape[1]
    return pl.pallas_call(
        _proj_kernel,
        grid=(m // tm, n // tn),
        in_specs=[pl.BlockSpec((tm, kd), lambda i, j: (i, 0)), pl.BlockSpec((kd, tn), lambda i, j: (0, j))],
        out_specs=pl.BlockSpec((tm, tn), lambda i, j: (i, j)),
        out_shape=jax.ShapeDtypeStruct((m, n), F32),
        scratch_shapes=[pltpu.VMEM((tm, kd), BF16)],
        compiler_params=pltpu.CompilerParams(
            dimension_semantics=("parallel", "arbitrary"), vmem_limit_bytes=VMEM_LIMIT_BYTES),
        name="proj",
    )(x, w)


def _out_ln_kernel(h_ref, w_ref, x_ref, g_ref, b_ref, o_ref):
    y = DEEPNORM_ALPHA * x_ref[...] + jnp.dot(h_ref[...].astype(BF16), w_ref[...], preferred_element_type=F32)
    o_ref[...] = _layer_norm_rows(y, g_ref[...], b_ref[...])


def out_ln(h, w, x, g, b, *, tm):
    m, kd = h.shape
    d = w.shape[1]
    return pl.pallas_call(
        _out_ln_kernel,
        grid=(m // tm,),
        in_specs=[
            pl.BlockSpec((tm, kd), lambda i: (i, 0)),
            pl.BlockSpec((kd, d), lambda i: (0, 0)),
            pl.BlockSpec((tm, d), lambda i: (i, 0)),
            pl.BlockSpec((1, d), lambda i: (0, 0)),
            pl.BlockSpec((1, d), lambda i: (0, 0)),
        ],
        out_specs=pl.BlockSpec((tm, d), lambda i: (i, 0)),
        out_shape=jax.ShapeDtypeStruct((m, d), F32),
        compiler_params=pltpu.CompilerParams(
            dimension_semantics=("parallel",), vmem_limit_bytes=VMEM_LIMIT_BYTES),
        name="out_ln",
    )(h, w, x, g.reshape(1, d), b.reshape(1, d))


def _mem_block_kernel(x_ref, wq_ref, mk_ref, mv_ref, wo_ref, g_ref, b_ref, o_ref):
    heads = [slice(h * MEM_HEAD_DIM, (h + 1) * MEM_HEAD_DIM) for h in range(N_MEM_HEADS)]
    scale = MEM_HEAD_DIM ** -0.5
    x = x_ref[0]
    q = jnp.dot(x.astype(BF16), wq_ref[...], preferred_element_type=F32).astype(BF16)
    if mk_ref.ndim == 2:
        k_of = lambda h: mk_ref[:, heads[h]].astype(BF16)
        v_of = lambda h: mv_ref[:, heads[h]].astype(BF16)
        keep = lambda h, s: s
    else:
        rows = N_MEM * N_MEM_HEADS
        k_all = mk_ref[...].reshape(rows, MEM_HEAD_DIM).astype(BF16)
        v_all = mv_ref[...].reshape(rows, MEM_HEAD_DIM).astype(BF16)
        k_of = lambda h: k_all
        v_of = lambda h: v_all
        head_of_col = lax.broadcasted_iota(jnp.int32, (1, rows), 1) % N_MEM_HEADS
        keep = lambda h, s: jnp.where(head_of_col == h, s, NEG_BIG)
    s_all = [keep(h, lax.dot_general(q[:, sl], k_of(h), _NT, preferred_element_type=F32) * scale)
             for h, sl in enumerate(heads)]
    e_all = [jnp.exp(s - jnp.max(s, axis=1, keepdims=True)) for s in s_all]
    o_heads = []
    for h in range(N_MEM_HEADS):
        pv = jnp.dot(e_all[h].astype(BF16), v_of(h), preferred_element_type=F32)
        o_heads.append((pv / jnp.sum(e_all[h], axis=1, keepdims=True)).astype(BF16))
    o = jnp.concatenate(o_heads, axis=1)
    y = DEEPNORM_ALPHA * x + jnp.dot(o, wo_ref[...], preferred_element_type=F32)
    o_ref[0] = _layer_norm_rows(y, g_ref[...], b_ref[...])


def mem_block(x, w_q, mk, mv, layer, w_o, g, b, *, tm):
    bsz, t, d = x.shape
    const = lambda shape: pl.BlockSpec(shape, lambda b, i: (0,) * len(shape))
    if mk.ndim == 5:
        mk_spec = mv_spec = pl.BlockSpec((None, None, N_MEM, N_MEM_HEADS, MEM_HEAD_DIM),
                                         lambda b, i: (layer, b, 0, 0, 0))
    else:
        mk_spec = pl.BlockSpec((None, N_MEM, d), lambda b, i: (layer, b, 0))
        mv_spec = pl.BlockSpec((None, N_MEM, d), lambda b, i: (layer, b, 1))
    return pl.pallas_call(
        _mem_block_kernel,
        grid=(bsz, t // tm),
        in_specs=[pl.BlockSpec((1, tm, d), lambda b, i: (b, i, 0)), const((d, d)), mk_spec, mv_spec,
                  const((d, d)), const((1, d)), const((1, d))],
        out_specs=pl.BlockSpec((1, tm, d), lambda b, i: (b, i, 0)),
        out_shape=jax.ShapeDtypeStruct((bsz, t, d), F32),
        compiler_params=pltpu.CompilerParams(
            dimension_semantics=("parallel", "arbitrary"), vmem_limit_bytes=VMEM_LIMIT_BYTES),
        name="mem_block",
    )(x, w_q, mk, mv, w_o, g.reshape(1, d), b.reshape(1, d))


CONV_COLS = 1024


def _conv_kernel(x_ref, st_ref, w_ref, b_ref, act_ref, last_ref, prev_ref, *, tc):
    @pl.when(pl.program_id(2) == 0)
    def _():
        prev_ref[...] = st_ref[0]

    x = x_ref[0]
    prev = prev_ref[...]
    row = lax.broadcasted_iota(jnp.int32, (8, CONV_COLS), 0)
    acc = jnp.broadcast_to(b_ref[...], x.shape)
    for s in (3, 2, 1):
        rolled = pltpu.roll(x, s, axis=0)
        top = jnp.where(row < s, pltpu.roll(prev, s, axis=0), rolled[0:8])
        shifted = top if tc == 8 else jnp.concatenate([top, rolled[8:]], axis=0)
        acc = acc + shifted * w_ref[CONV_W - 1 - s:CONV_W - s, :]
    acc = acc + x * w_ref[CONV_W - 1:CONV_W, :]
    act_ref[0] = acc * jax.nn.sigmoid(acc)
    prev_ref[...] = x[tc - 8:tc]
    last_ref[0] = x[tc - 8:tc]


def ssm_conv(zx, conv_state, conv_w, conv_b, *, tc):
    bsz, t = zx.shape[:2]
    ncb = CONV_DIM // CONV_COLS
    col0 = D_INNER // CONV_COLS
    st8 = jnp.pad(conv_state, ((0, 0), (8 - (CONV_W - 1), 0), (0, 0)))
    act, last = pl.pallas_call(
        functools.partial(_conv_kernel, tc=tc),
        grid=(bsz, ncb, t // tc),
        in_specs=[
            pl.BlockSpec((1, tc, CONV_COLS), lambda b, c, i: (b, i, col0 + c)),
            pl.BlockSpec((1, 8, CONV_COLS), lambda b, c, i: (b, 0, c)),
            pl.BlockSpec((CONV_W, CONV_COLS), lambda b, c, i: (0, c)),
            pl.BlockSpec((1, CONV_COLS), lambda b, c, i: (0, c)),
        ],
        out_specs=[
            pl.BlockSpec((1, tc, CONV_COLS), lambda b, c, i: (b, i, c)),
            pl.BlockSpec((1, 8, CONV_COLS), lambda b, c, i: (b, 0, c)),
        ],
        out_shape=[jax.ShapeDtypeStruct((bsz, t, CONV_DIM), F32), jax.ShapeDtypeStruct((bsz, 8, CONV_DIM), F32)],
        scratch_shapes=[pltpu.VMEM((8, CONV_COLS), F32)],
        compiler_params=pltpu.CompilerParams(
            dimension_semantics=("parallel", "parallel", "arbitrary"), vmem_limit_bytes=VMEM_LIMIT_BYTES),
        name="ssm_conv",
    )(zx, st8, conv_w, conv_b.reshape(1, CONV_DIM))
    return act, last[:, 8 - (CONV_W - 1):, :]


def _split3(v):
    hi = v.astype(BF16)
    r1 = v - hi.astype(F32)
    mid = r1.astype(BF16)
    lo = (r1 - mid.astype(F32)).astype(BF16)
    return hi, mid, lo


def _dot01_right(v, ones_mat):
    return sum(jnp.dot(p, ones_mat, preferred_element_type=F32) for p in _split3(v))


def _dot01_left(ones_mat, v):
    return sum(jnp.dot(ones_mat, p, preferred_element_type=F32) for p in _split3(v))


def _softplus(x):
    return jnp.maximum(x, 0.0) + jnp.log1p(jnp.exp(-jnp.abs(x)))


def _ssd_kernel(xs_ref, b_ref, c_ref, z_ref, dt_ref, dtt_ref, dtb_ref, dtbt_ref, alog_ref, alogt_ref,
                d_ref, nw_ref, h0_ref, y_ref, hT_ref, st_ref, yacc_ref, *, c):
    ci = pl.program_id(1)

    @pl.when(ci == 0)
    def _():
        st_ref[...] = h0_ref[0]

    gw = HEADS_PER_GROUP * SSM_HEAD_DIM
    dt = _softplus(dt_ref[0][:, 0:N_SSM_HEADS] + dtb_ref[...])
    dtt = _softplus(dtt_ref[0] + dtbt_ref[...])
    a = dt * -jnp.exp(alog_ref[...])
    at = dtt * -jnp.exp(alogt_ref[...])
    ri = lax.broadcasted_iota(jnp.int32, (c, c), 0)
    cj = lax.broadcasted_iota(jnp.int32, (c, c), 1)
    causal = ri >= cj
    acum = _dot01_left(jnp.where(causal, 1.0, 0.0).astype(BF16), a)
    acum_t = _dot01_right(at, jnp.where(ri <= cj, 1.0, 0.0).astype(BF16))
    a_last = acum[c - 1:c, :]
    expand = jnp.where(lax.broadcasted_iota(jnp.int32, (N_SSM_HEADS, D_INNER), 1) // SSM_HEAD_DIM
                       == lax.broadcasted_iota(jnp.int32, (N_SSM_HEADS, D_INNER), 0), 1.0, 0.0).astype(BF16)
    e_dt = _dot01_right(dt, expand)
    e_in = _dot01_right(jnp.exp(acum), expand)
    e_out = _dot01_right(jnp.exp(a_last - acum), expand)
    e_chunk = e_in[c - 1:c, :]
    xs = xs_ref[0]
    xdt = xs * e_dt
    head_of_col = lax.broadcasted_iota(jnp.int32, (c, gw), 1) // SSM_HEAD_DIM
    for g in range(SSM_GROUPS):
        gs = slice(g * gw, (g + 1) * gw)
        bg = b_ref[0][:, g * D_STATE:(g + 1) * D_STATE].astype(BF16)
        cg = c_ref[0][:, g * D_STATE:(g + 1) * D_STATE].astype(BF16)
        cb = lax.dot_general(cg, bg, _NT, preferred_element_type=F32)
        xg = xdt[:, gs]
        yg = jnp.zeros((c, gw), F32)
        for r in range(HEADS_PER_GROUP):
            h = g * HEADS_PER_GROUP + r
            seg = acum[:, h:h + 1] - acum_t[h:h + 1, :]
            m = (cb * jnp.exp(jnp.where(causal, seg, -jnp.inf))).astype(BF16)
            xm = jnp.where(head_of_col == r, xg, 0.0).astype(BF16)
            yg = yg + jnp.dot(m, xm, preferred_element_type=F32)
        state = st_ref[g]
        y_off = jnp.dot(cg, state.astype(BF16), preferred_element_type=F32) * e_in[:, gs]
        yacc_ref[:, gs] = yg + y_off + xs[:, gs] * d_ref[:, gs]
        xd = (xg * e_out[:, gs]).astype(BF16)
        st_ref[g] = state * e_chunk[:, gs] + lax.dot_general(bg, xd, (((0,), (0,)), ((), ())),
                                                             preferred_element_type=F32)
    z = z_ref[0]
    yz = yacc_ref[...] * (z * jax.nn.sigmoid(z))
    y_ref[0] = yz * lax.rsqrt(jnp.mean(yz * yz, axis=-1, keepdims=True) + LN_EPS) * nw_ref[...]

    @pl.when(ci == pl.num_programs(1) - 1)
    def _():
        hT_ref[0] = st_ref[...]


def ssd_scan(act, zx, dt_raw, dt_bias, a_log, d_skip, norm_w, ssm_state, *, c):
    bsz, t = act.shape[:2]
    gw = HEADS_PER_GROUP * SSM_HEAD_DIM
    bc_w = SSM_GROUPS * D_STATE
    h0 = ssm_state.reshape(bsz, SSM_GROUPS, HEADS_PER_GROUP, SSM_HEAD_DIM, D_STATE)
    h0 = h0.transpose(0, 1, 4, 2, 3).reshape(bsz, SSM_GROUPS, D_STATE, gw)
    dtt = jnp.swapaxes(dt_raw[..., :N_SSM_HEADS], 1, 2)
    row = lambda v: v.reshape(1, -1)
    col = lambda v: v.reshape(-1, 1)
    full = lambda shape: pl.BlockSpec(shape, lambda b, i: (0,) * len(shape))
    y, h_t = pl.pallas_call(
        functools.partial(_ssd_kernel, c=c),
        grid=(bsz, t // c),
        in_specs=[
            pl.BlockSpec((1, c, D_INNER), lambda b, i: (b, i, 0)),
            pl.BlockSpec((1, c, bc_w), lambda b, i: (b, i, D_INNER // bc_w)),
            pl.BlockSpec((1, c, bc_w), lambda b, i: (b, i, D_INNER // bc_w + 1)),
            pl.BlockSpec((1, c, D_INNER), lambda b, i: (b, i, 0)),
            pl.BlockSpec((1, c, 128), lambda b, i: (b, i, 0)),
            pl.BlockSpec((1, N_SSM_HEADS, c), lambda b, i: (b, 0, i)),
            full((1, N_SSM_HEADS)), full((N_SSM_HEADS, 1)), full((1, N_SSM_HEADS)), full((N_SSM_HEADS, 1)),
            full((1, D_INNER)), full((1, D_INNER)),
            pl.BlockSpec((1, SSM_GROUPS, D_STATE, gw), lambda b, i: (b, 0, 0, 0)),
        ],
        out_specs=[
            pl.BlockSpec((1, c, D_INNER), lambda b, i: (b, i, 0)),
            pl.BlockSpec((1, SSM_GROUPS, D_STATE, gw), lambda b, i: (b, 0, 0, 0)),
        ],
        out_shape=[jax.ShapeDtypeStruct((bsz, t, D_INNER), F32),
                   jax.ShapeDtypeStruct((bsz, SSM_GROUPS, D_STATE, gw), F32)],
        scratch_shapes=[pltpu.VMEM((SSM_GROUPS, D_STATE, gw), F32), pltpu.VMEM((c, D_INNER), F32)],
        compiler_params=pltpu.CompilerParams(
            dimension_semantics=("parallel", "arbitrary"), vmem_limit_bytes=VMEM_LIMIT_BYTES),
        name="ssd_scan",
    )(act, act, act, zx, dt_raw, dtt, row(dt_bias), col(dt_bias), row(a_log), col(a_log),
      row(jnp.repeat(d_skip, SSM_HEAD_DIM)), row(norm_w), h0)
    h_t = h_t.reshape(bsz, SSM_GROUPS, D_STATE, HEADS_PER_GROUP, SSM_HEAD_DIM)
    return y, h_t.transpose(0, 1, 3, 4, 2).reshape(bsz, N_SSM_HEADS, SSM_HEAD_DIM, D_STATE)


INT_MIN = -2 ** 31
NEG_BIG = -1e30
_NT = (((1,), (1,)), ((), ()))


def _float_key(x):
    x = jnp.where(x == 0.0, 0.0, x)
    bits = lax.bitcast_convert_type(x, jnp.int32)
    return bits ^ ((bits >> 31) & 0x7FFFFFFF)


def _dsa_prompt_kernel(qt_ref, qit_ref, wit_ref, k_ref, vxt_ref, ki_ref, o_ref, key_ref, m_ref, acc_ref,
                         s_ref, p_ref, *, tq, tk, topk):
    i = pl.program_id(1)
    nkb = ((i + 1) * tq + tk - 1) // tk
    gq = KV_GROUP * tq
    rep8 = lambda v: jnp.broadcast_to(v, (8, tq))
    as3 = lambda x: x.reshape(tk // 8, 8, tq)
    w_heads = wit_ref[0, 0] * (N_IDX_HEADS ** -0.5)
    q_pos = i * tq + lax.broadcasted_iota(jnp.int32, (tk, tq), 1)

    def score_block(kb, c):
        off = pl.multiple_of(kb * tk, tk)
        s = jnp.dot(ki_ref[0, pl.ds(off, tk), :], qit_ref[0, 0], preferred_element_type=F32)
        acc = jnp.zeros((tk, tq), F32)
        for h in range(N_IDX_HEADS):
            acc = acc + w_heads[h:h + 1, :] * jnp.maximum(s[:, h * tq:(h + 1) * tq], 0.0)
        kpos = off + lax.broadcasted_iota(jnp.int32, (tk, tq), 0)
        key_ref[kb] = jnp.where(kpos <= q_pos, _float_key(acc), INT_MIN)
        return c

    lax.fori_loop(0, nkb, score_block, 0)

    def count(cand, strict):
        def body(kb, accs):
            blk = key_ref[kb]
            accs = list(accs)
            for r in range(tk // 8):
                part = blk[r * 8:(r + 1) * 8, :]
                hit = (part > cand) if strict else (part >= cand)
                accs[r % 4] = accs[r % 4] + jnp.where(hit, 1.0, 0.0)
            return tuple(accs)
        a0, a1, a2, a3 = lax.fori_loop(0, nkb, body, (jnp.zeros((8, tq), F32),) * 4)
        return jnp.sum((a0 + a1) + (a2 + a3), axis=0, keepdims=True)

    kf = float(topk)
    zero = jnp.zeros((8, tq), jnp.int32)
    thr = jnp.where(rep8(count(zero, False)) >= kf, zero, jnp.full((8, tq), INT_MIN, jnp.int32))

    def bit_step(it, thr):
        cand = thr | jnp.left_shift(jnp.int32(1), 30 - it)
        return jnp.where(rep8(count(cand, False)) >= kf, cand, thr)

    thr = lax.fori_loop(0, 31, bit_step, thr)
    need = jnp.where(thr == INT_MIN, 0.0, kf - rep8(count(thr, True)))

    m_ref[...] = jnp.full(m_ref.shape, NEG_BIG, F32)
    acc_ref[...] = jnp.zeros(acc_ref.shape, F32)
    lower = jnp.where(lax.broadcasted_iota(jnp.int32, (tk, tk), 1) < lax.broadcasted_iota(jnp.int32, (tk, tk), 0),
                      1.0, 0.0).astype(BF16)

    def attend_block(kb, eq_before):
        off = pl.multiple_of(kb * tk, tk)
        key3 = as3(key_ref[kb])
        eq3 = key3 == thr[None]
        eqf = jnp.where(eq3, 1.0, 0.0)
        rank3 = as3(jnp.dot(lower, eqf.reshape(tk, tq).astype(BF16), preferred_element_type=F32)) + eq_before[None]
        bias = jnp.where(key3 > thr[None], 0.0,
                         jnp.where(eq3, jnp.where(rank3 < need[None], 0.0, NEG_BIG), NEG_BIG)).reshape(tk, tq)
        bias4 = jnp.concatenate([bias] * KV_GROUP, axis=1)
        col_max = []
        for j in range(N_KV_HEADS):
            kblk = k_ref[0, pl.ds(off, tk), j * 64:(j + 1) * 64]
            s = jnp.dot(kblk, qt_ref[0, 0, j], preferred_element_type=F32) + bias4
            s_ref[j] = s
            col_max.append(jnp.max(s, axis=0, keepdims=True))
        alpha = []
        for j in range(N_KV_HEADS):
            m_old = m_ref[j]
            m_new = jnp.maximum(m_old, col_max[j])
            alpha.append(jnp.exp(m_old - m_new))
            p_ref[j] = jnp.exp(s_ref[j] - m_new).astype(BF16)
            m_ref[j] = m_new
        for j in range(N_KV_HEADS):
            pv = jnp.dot(vxt_ref[0, j, kb], p_ref[j], preferred_element_type=F32)
            acc_ref[j] = alpha[j] * acc_ref[j] + pv
        return eq_before + rep8(jnp.sum(jnp.sum(eqf, axis=0), axis=0, keepdims=True))

    lax.fori_loop(0, nkb, attend_block, jnp.zeros((8, tq), F32))

    for j in range(N_KV_HEADS):
        a = acc_ref[j]
        o_t = a[0:64, :] / a[64:65, :]
        for g in range(KV_GROUP):
            h = j * KV_GROUP + g
            o_ref[0, :, h * 64:(h + 1) * 64] = o_t[:, g * tq:(g + 1) * tq].T


def dsa_prompt_attend(q, k, v, qi, ki, wi, *, tq, tk):
    bsz, t = q.shape[:2]
    nq, nk = t // tq, t // tk
    topk = min(TOPK_MAX, t // 4)
    qt = (q * ATT_HEAD_DIM ** -0.5).astype(BF16).reshape(bsz, nq, tq, N_KV_HEADS, KV_GROUP, 64)
    qt = qt.transpose(0, 1, 3, 5, 4, 2).reshape(bsz, nq, N_KV_HEADS, 64, KV_GROUP * tq)
    qit = (qi * IDX_DIM ** -0.5).astype(BF16).reshape(bsz, nq, tq, N_IDX_HEADS, 64)
    qit = qit.transpose(0, 1, 4, 3, 2).reshape(bsz, nq, 64, N_IDX_HEADS * tq)
    wit = wi.reshape(bsz, nq, tq, N_IDX_HEADS).transpose(0, 1, 3, 2)
    vt = v.astype(BF16).reshape(bsz, nk, tk, N_KV_HEADS, 64).transpose(0, 3, 1, 4, 2)
    vxt = jnp.concatenate([vt, jnp.ones_like(vt)], axis=3)
    kern = functools.partial(_dsa_prompt_kernel, tq=tq, tk=tk, topk=topk)
    return pl.pallas_call(
        kern,
        grid=(bsz, nq),
        in_specs=[
            pl.BlockSpec((1, 1, N_KV_HEADS, 64, KV_GROUP * tq), lambda b, i: (b, i, 0, 0, 0)),
            pl.BlockSpec((1, 1, 64, N_IDX_HEADS * tq), lambda b, i: (b, i, 0, 0)),
            pl.BlockSpec((1, 1, N_IDX_HEADS, tq), lambda b, i: (b, i, 0, 0)),
            pl.BlockSpec((1, t, 256), lambda b, i: (b, 0, 0), pipeline_mode=pl.Buffered(1)),
            pl.BlockSpec((1, N_KV_HEADS, nk, 128, tk), lambda b, i: (b, 0, 0, 0, 0), pipeline_mode=pl.Buffered(1)),
            pl.BlockSpec((1, t, IDX_DIM), lambda b, i: (b, 0, 0), pipeline_mode=pl.Buffered(1)),
        ],
        out_specs=pl.BlockSpec((1, tq, 1024), lambda b, i: (b, i, 0)),
        out_shape=jax.ShapeDtypeStruct((bsz, t, 1024), F32),
        scratch_shapes=[
            pltpu.VMEM((nk, tk, tq), jnp.int32),
            pltpu.VMEM((N_KV_HEADS, 1, KV_GROUP * tq), F32),
            pltpu.VMEM((N_KV_HEADS, 128, KV_GROUP * tq), F32),
            pltpu.VMEM((N_KV_HEADS, tk, KV_GROUP * tq), F32),
            pltpu.VMEM((N_KV_HEADS, tk, KV_GROUP * tq), BF16),
        ],
        compiler_params=pltpu.CompilerParams(
            dimension_semantics=("parallel", "arbitrary"), vmem_limit_bytes=VMEM_LIMIT_BYTES),
        name="dsa_prompt_attend",
    )(qt, qit, wit, k.astype(BF16), vxt, ki.astype(BF16))


MAX_PAGES_PER_STEP = 64


def _sample_select_kernel(pt_ref, qi_ref, wi_ref, kinew_ref, *rest, t, n_steps, topk, pps):
    ki_refs = rest[:pps]
    bias_ref, qis_ref, wib_ref, key_ref = rest[pps:]
    p = pl.program_id(1)
    w = pps * PAGE_SIZE
    nl = w // 128

    @pl.when(p == 0)
    def _():
        for h in range(N_IDX_HEADS):
            qis_ref[h * t:(h + 1) * t, :] = qi_ref[0, :, h * 64:(h + 1) * 64]
            wib_ref[h] = jnp.broadcast_to(wi_ref[0, :, h:h + 1] * (N_IDX_HEADS ** -0.5), (t, 128))

    def scores(s):
        acc = jnp.zeros((t, s.shape[1]), F32)
        for h in range(N_IDX_HEADS):
            wfull = jnp.concatenate([wib_ref[h]] * (s.shape[1] // 128), axis=1)
            acc = acc + wfull * jnp.maximum(s[h * t:(h + 1) * t, :], 0.0)
        return acc

    ki_t = jnp.concatenate([r[...] for r in ki_refs], axis=1).astype(BF16)
    key_ref[p] = _float_key(scores(jnp.dot(qis_ref[...].astype(BF16), ki_t, preferred_element_type=F32)))

    @pl.when(p == n_steps - 1)
    def _():
        knew = _float_key(scores(lax.dot_general(qis_ref[...].astype(BF16), kinew_ref[0].astype(BF16), _NT,
                                                 preferred_element_type=F32)))
        n_idx = lax.broadcasted_iota(jnp.int32, (t, 128), 1)
        r_idx = lax.broadcasted_iota(jnp.int32, (t, 128), 0)
        knew = jnp.where(n_idx <= r_idx, knew, INT_MIN)
        key_ref[n_steps] = jnp.concatenate([knew, jnp.full((t, w - 128), INT_MIN, jnp.int32)], axis=1)

        def count(cand, strict):
            def body(blk, acc):
                kb = key_ref[blk]
                for c in range(nl):
                    part = kb[:, c * 128:(c + 1) * 128]
                    hit = (part > cand) if strict else (part >= cand)
                    acc = acc + jnp.where(hit, 1.0, 0.0)
                return acc
            acc = lax.fori_loop(0, n_steps + 1, body, jnp.zeros((t, 128), F32))
            return jnp.sum(acc, axis=1, keepdims=True)

        kf = float(topk)
        zero = jnp.zeros((t, 128), jnp.int32)
        thr = jnp.where(count(zero, False) >= kf, zero, jnp.full((t, 128), INT_MIN, jnp.int32))

        def bit_step(it, thr):
            cand = thr | jnp.left_shift(jnp.int32(1), 30 - it)
            return jnp.where(count(cand, False) >= kf, cand, thr)

        thr = lax.fori_loop(0, 31, bit_step, thr)
        need = jnp.where(thr == INT_MIN, 0.0, kf - count(thr, True))
        floor_thr = jnp.where(thr == INT_MIN, INT_MIN + 1, thr)
        untied = jnp.where((thr == INT_MIN) | (count(thr, False) <= kf), 1.0, 0.0)
        no_ties = jnp.min(untied) > 0.0

        @pl.when(no_ties)
        def _():
            wide = jnp.concatenate([floor_thr] * nl, axis=1)

            def emit_simple(blk, c):
                bias_ref[0, blk] = jnp.where(key_ref[blk] >= wide, 0.0, NEG_BIG)
                return c

            lax.fori_loop(0, n_steps + 1, emit_simple, 0)

        upper = jnp.where(lax.broadcasted_iota(jnp.int32, (128, 128), 0)
                          < lax.broadcasted_iota(jnp.int32, (128, 128), 1), 1.0, 0.0).astype(BF16)

        def emit(blk, eq_before):
            kb = key_ref[blk]
            parts = []
            for c in range(nl):
                part = kb[:, c * 128:(c + 1) * 128]
                eq = part == thr
                eqf = jnp.where(eq, 1.0, 0.0)
                rank = jnp.dot(eqf.astype(BF16), upper, preferred_element_type=F32) + eq_before
                parts.append(jnp.where(part > thr, 0.0,
                                       jnp.where(eq, jnp.where(rank < need, 0.0, NEG_BIG), NEG_BIG)))
                eq_before = eq_before + jnp.sum(eqf, axis=1, keepdims=True)
            bias_ref[0, blk] = jnp.concatenate(parts, axis=1)
            return eq_before

        @pl.when(jnp.logical_not(no_ties))
        def _():
            lax.fori_loop(0, n_steps + 1, emit, jnp.zeros((t, 128), F32))


def _sample_attend_kernel(pt_ref, q_ref, bias_ref, biasnew_ref, knew_ref, vnew_ref, *rest, t, n_steps, pps):
    k_refs = rest[:pps]
    v_refs = rest[pps:2 * pps]
    o_ref, qs_ref, m_ref, l_ref, acc_ref = rest[2 * pps:]
    p = pl.program_id(1)
    rows = KV_GROUP * t

    @pl.when(p == 0)
    def _():
        for h in range(N_ATT_HEADS):
            j, g = divmod(h, KV_GROUP)
            qs_ref[j, g * t:(g + 1) * t, :] = q_ref[0, :, h * 64:(h + 1) * 64]
        m_ref[...] = jnp.full(m_ref.shape, NEG_BIG, F32)
        l_ref[...] = jnp.zeros(l_ref.shape, F32)
        acc_ref[...] = jnp.zeros(acc_ref.shape, F32)

    def update(k_t, v_t, bias):
        bias4 = jnp.concatenate([bias] * KV_GROUP, axis=0)
        s_all = [jnp.dot(qs_ref[j].astype(BF16), k_t(j).astype(BF16), preferred_element_type=F32) + bias4
                 for j in range(N_KV_HEADS)]
        alphas, probs = [], []
        for j in range(N_KV_HEADS):
            m_old = m_ref[j]
            m_new = jnp.maximum(m_old, jnp.max(s_all[j], axis=1, keepdims=True))
            alpha = jnp.exp(m_old - m_new)
            pr = jnp.exp(s_all[j] - m_new[:, 0:1])
            l_ref[j] = alpha * l_ref[j] + jnp.sum(pr, axis=1, keepdims=True)
            m_ref[j] = m_new
            alphas.append(alpha)
            probs.append(pr.astype(BF16))
        for j in range(N_KV_HEADS):
            pv = lax.dot_general(probs[j], v_t(j).astype(BF16), _NT, preferred_element_type=F32)
            acc_ref[j] = alphas[j][:, 0:64] * acc_ref[j] + pv

    update(lambda j: jnp.concatenate([r[j] for r in k_refs], axis=1),
           lambda j: jnp.concatenate([r[j] for r in v_refs], axis=1), bias_ref[0, 0])

    @pl.when(p == n_steps - 1)
    def _():
        update(lambda j: knew_ref[0, j * 64:(j + 1) * 64, :], lambda j: vnew_ref[0, j * 64:(j + 1) * 64, :],
               biasnew_ref[0, 0, :, 0:128])
        for h in range(N_ATT_HEADS):
            j, g = divmod(h, KV_GROUP)
            sl = slice(g * t, (g + 1) * t)
            o_ref[0, :, h * 64:(h + 1) * 64] = acc_ref[j, sl, :] / l_ref[j, sl, 0:64]


def dsa_sample_attend(q, k, v, qi, ki, wi, cache_k, cache_v, cache_kidx, layer, page_table):
    bsz, t = q.shape[:2]
    n_pages = page_table.shape[1]
    past = n_pages * PAGE_SIZE
    pps = min(MAX_PAGES_PER_STEP, n_pages)
    assert n_pages % pps == 0, (n_pages, pps)
    n_steps = n_pages // pps
    w = pps * PAGE_SIZE
    topk = min(TOPK_MAX, (past + t) // 4)
    pad = lambda a: jnp.pad(a, ((0, 0), (0, PAGE_SIZE - t), (0, 0)))
    pool_k = cache_k.transpose(0, 1, 3, 4, 2)
    pool_v = cache_v.transpose(0, 1, 3, 4, 2)
    pool_ki = cache_kidx.transpose(0, 1, 3, 2)
    knew_t = jnp.swapaxes(pad(k), 1, 2)
    vnew_t = jnp.swapaxes(pad(v), 1, 2)

    def page_spec(r, *major):
        zeros = (0,) * (1 + len(major))
        return pl.BlockSpec((None, None) + major + (PAGE_SIZE,),
                            lambda b, p, pt: (layer, pt[b, p * pps + r]) + zeros)

    bias = pl.pallas_call(
        functools.partial(_sample_select_kernel, t=t, n_steps=n_steps, topk=topk, pps=pps),
        grid_spec=pltpu.PrefetchScalarGridSpec(
            num_scalar_prefetch=1,
            grid=(bsz, n_steps),
            in_specs=[
                pl.BlockSpec((1, t, 512), lambda b, p, pt: (b, 0, 0)),
                pl.BlockSpec((1, t, N_IDX_HEADS), lambda b, p, pt: (b, 0, 0)),
                pl.BlockSpec((1, PAGE_SIZE, IDX_DIM), lambda b, p, pt: (b, 0, 0)),
            ] + [page_spec(r, IDX_DIM) for r in range(pps)],
            out_specs=pl.BlockSpec((1, n_steps + 1, t, w), lambda b, p, pt: (b, 0, 0, 0)),
            scratch_shapes=[
                pltpu.VMEM((N_IDX_HEADS * t, 64), F32),
                pltpu.VMEM((N_IDX_HEADS, t, 128), F32),
                pltpu.VMEM((n_steps + 1, t, w), jnp.int32),
            ]),
        out_shape=jax.ShapeDtypeStruct((bsz, n_steps + 1, t, w), F32),
        compiler_params=pltpu.CompilerParams(
            dimension_semantics=("parallel", "arbitrary"), vmem_limit_bytes=VMEM_LIMIT_BYTES),
        name="dsa_sample_select",
    )(page_table, qi * IDX_DIM ** -0.5, wi, pad(ki), *([pool_ki] * pps))

    return pl.pallas_call(
        functools.partial(_sample_attend_kernel, t=t, n_steps=n_steps, pps=pps),
        grid_spec=pltpu.PrefetchScalarGridSpec(
            num_scalar_prefetch=1,
            grid=(bsz, n_steps),
            in_specs=[
                pl.BlockSpec((1, t, 1024), lambda b, p, pt: (b, 0, 0)),
                pl.BlockSpec((1, 1, t, w), lambda b, p, pt: (b, p, 0, 0)),
                pl.BlockSpec((1, 1, t, w), lambda b, p, pt: (b, n_steps, 0, 0)),
                pl.BlockSpec((1, 256, PAGE_SIZE), lambda b, p, pt: (b, 0, 0)),
                pl.BlockSpec((1, 256, PAGE_SIZE), lambda b, p, pt: (b, 0, 0)),
            ] + [page_spec(r, N_KV_HEADS, ATT_HEAD_DIM) for r in range(pps)] * 2,
            out_specs=pl.BlockSpec((1, t, 1024), lambda b, p, pt: (b, 0, 0)),
            scratch_shapes=[
                pltpu.VMEM((N_KV_HEADS, KV_GROUP * t, 64), F32),
                pltpu.VMEM((N_KV_HEADS, KV_GROUP * t, 128), F32),
                pltpu.VMEM((N_KV_HEADS, KV_GROUP * t, 128), F32),
                pltpu.VMEM((N_KV_HEADS, KV_GROUP * t, 64), F32),
            ]),
        out_shape=jax.ShapeDtypeStruct((bsz, t, 1024), F32),
        compiler_params=pltpu.CompilerParams(
            dimension_semantics=("parallel", "arbitrary"), vmem_limit_bytes=VMEM_LIMIT_BYTES),
        name="dsa_sample_attend",
    )(page_table, q * ATT_HEAD_DIM ** -0.5, bias, bias, knew_t, vnew_t,
      *([pool_k] * pps), *([pool_v] * pps))


def dsa_project(x, w_main, w_tail, pos, tm):
    bsz, t = x.shape[:2]
    x2 = x.reshape(bsz * t, D_MODEL)
    main = proj(x2, w_main, tm=tm, tn=1024).reshape(bsz, t, -1)
    tail = proj(x2, w_tail, tm=tm, tn=128).reshape(bsz, t, -1)
    main, tail = rope_apply(main, tail, pos, tr=min(t, 512))
    q, k, v, qi = jnp.split(main, [1024, 1280, 1536], axis=-1)
    ki, wi = tail[..., :IDX_DIM], tail[..., IDX_DIM:IDX_DIM + N_IDX_HEADS]
    return (q, k.reshape(bsz, t, N_KV_HEADS, ATT_HEAD_DIM), v.reshape(bsz, t, N_KV_HEADS, ATT_HEAD_DIM), qi, ki, wi)


ROPE_V_BLOCKS = (10, 11)


def _rope_kernel(main_ref, tail_ref, cos_ref, sin_ref, mo_ref, to_ref):
    cos, sin = cos_ref[...], sin_ref[...]
    lane = lax.broadcasted_iota(jnp.int32, cos.shape, 1)
    first_half = lane % ATT_HEAD_DIM < ATT_HEAD_DIM // 2

    def rot(xb):
        partner = jnp.where(first_half, -pltpu.roll(xb, 128 - 32, axis=1), pltpu.roll(xb, 32, axis=1))
        return xb * cos + partner * sin

    for cb in range(main_ref.shape[2] // 128):
        xb = main_ref[0, :, cb * 128:(cb + 1) * 128]
        mo_ref[0, :, cb * 128:(cb + 1) * 128] = xb if cb in ROPE_V_BLOCKS else rot(xb)
    tb = tail_ref[0]
    to_ref[0] = jnp.where(lane < IDX_DIM, rot(tb), tb)


def rope_apply(main, tail, pos, *, tr):
    bsz, t, wm = main.shape
    inv_freq = ROPE_THETA ** (-jnp.arange(0, ATT_HEAD_DIM, 2, dtype=F32) / ATT_HEAD_DIM)
    ang = pos.astype(F32)[:, None] * inv_freq[None, :]
    cos = jnp.tile(jnp.cos(ang), (1, 4))
    sin = jnp.tile(jnp.sin(ang), (1, 4))
    return pl.pallas_call(
        _rope_kernel,
        grid=(bsz, t // tr),
        in_specs=[
            pl.BlockSpec((1, tr, wm), lambda b, i: (b, i, 0)),
            pl.BlockSpec((1, tr, 128), lambda b, i: (b, i, 0)),
            pl.BlockSpec((tr, 128), lambda b, i: (i, 0)),
            pl.BlockSpec((tr, 128), lambda b, i: (i, 0)),
        ],
        out_specs=[pl.BlockSpec((1, tr, wm), lambda b, i: (b, i, 0)), pl.BlockSpec((1, tr, 128), lambda b, i: (b, i, 0))],
        out_shape=[jax.ShapeDtypeStruct(main.shape, F32), jax.ShapeDtypeStruct(tail.shape, F32)],
        compiler_params=pltpu.CompilerParams(
            dimension_semantics=("parallel", "parallel"), vmem_limit_bytes=VMEM_LIMIT_BYTES),
        name="rope",
    )(main, tail, cos, sin)


def run_trunk(x, mem_k, mem_v, conv0, ssm0, attend, pos, p, *, tm, tm_mem):
    conv_out, ssm_out, k_out, v_out, ki_out = [], [], [], [], []
    bsz, t = x.shape[:2]
    m = bsz * t
    x = x.reshape(m, D_MODEL)
    for i in range(DEPTH):
        j = i // N_MIXERS
        g, b = p['ln_g'][i], p['ln_b'][i]
        if i % N_MIXERS == 0:
            zx = proj(x, p['w_ssm_main'][j], tm=tm, tn=1024).reshape(bsz, t, -1)
            dt_raw = proj(x, p['w_ssm_dt'][j], tm=tm, tn=128).reshape(bsz, t, -1)
            act, cs = ssm_conv(zx, conv0[j], p['ssm_conv_w'][j], p['ssm_conv_b'][j], tc=min(t, 512))
            h, ss = ssd_scan(act, zx, dt_raw, p['ssm_dt_bias'][j], p['ssm_a_log'][j], p['ssm_d'][j],
                             p['ssm_norm_w'][j], ssm0[j], c=SSM_CHUNK if t % SSM_CHUNK == 0 else t)
            conv_out.append(cs)
            ssm_out.append(ss)
            w_o = p['w_ssm_out'][j]
        else:
            q, k, v, qi, ki, wi = dsa_project(x.reshape(bsz, t, D_MODEL), p['w_att_main'][j], p['w_att_tail'][j],
                                              pos, tm)
            h = attend(j, q.reshape(bsz, t, -1), k.reshape(bsz, t, -1), v.reshape(bsz, t, -1),
                       qi.reshape(bsz, t, -1), ki, wi)
            k_out.append(k)
            v_out.append(v)
            ki_out.append(ki)
            w_o = p['w_att_out'][j]
        x = out_ln(h.reshape(m, -1), w_o, x, g[0], b[0], tm=tm_mem)
        x = mem_block(x.reshape(bsz, t, D_MODEL), p['w_mem_q'][i], mem_k, mem_v, i, p['w_mem_out'][i], g[1], b[1],
                      tm=min(t, tm_mem)).reshape(m, D_MODEL)
        x = ffn_ln(x, p['w_ffn_in'][i], p['w_ffn_out'][i], g[2], b[2], tm=tm)
    return (x.reshape(bsz, t, D_MODEL), jnp.stack(k_out), jnp.stack(v_out), jnp.stack(ki_out),
            jnp.stack(conv_out), jnp.stack(ssm_out))


def kernel(x_prompt, x_sample, cache_k, cache_v, cache_kidx, cache_mem_k, cache_mem_v, state_conv, state_ssm,
           page_table, mem_prompt, w_ssm_in, ssm_conv_w, ssm_conv_b, ssm_dt_bias, ssm_a_log, ssm_d, ssm_norm_w,
           w_ssm_out, w_att_in, w_att_out, w_mem_q, w_mem_kv, w_mem_out, w_ffn_in, w_ffn_out, ln_g, ln_b):
    zx_cols = D_INNER + CONV_DIM
    att_cols = sum(ATT_SPLITS[:4])
    pad128 = lambda w: jnp.pad(w, ((0, 0), (0, 0), (0, 128 - w.shape[-1])))
    params = {'w_ssm_main': w_ssm_in[..., :zx_cols].astype(BF16), 'w_ssm_dt': pad128(w_ssm_in[..., zx_cols:]).astype(BF16),
              'ssm_conv_w': ssm_conv_w, 'ssm_conv_b': ssm_conv_b, 'ssm_dt_bias': ssm_dt_bias,
              'ssm_a_log': ssm_a_log, 'ssm_d': ssm_d, 'ssm_norm_w': ssm_norm_w, 'w_ssm_out': w_ssm_out.astype(BF16),
              'w_att_main': w_att_in[..., :att_cols].astype(BF16), 'w_att_tail': pad128(w_att_in[..., att_cols:]).astype(BF16),
              'w_att_out': w_att_out.astype(BF16),
              'w_mem_q': w_mem_q.astype(BF16), 'w_mem_out': w_mem_out.astype(BF16),
              'w_ffn_in': w_ffn_in.astype(BF16), 'w_ffn_out': w_ffn_out.astype(BF16),
              'ln_g': ln_g, 'ln_b': ln_b}
    bp, seq = x_prompt.shape[:2]
    pos_p = jnp.arange(seq, dtype=jnp.int32)
    w_mem_kv_bf = w_mem_kv.astype(BF16)
    mem_rows = mem_prompt.reshape(bp * N_MEM, D_MODEL)
    mem_kv = jnp.stack([proj(mem_rows, w_mem_kv_bf[l], tm=bp * N_MEM, tn=512) for l in range(DEPTH)])
    memk_prompt = mem_kv[..., :D_MODEL].reshape(DEPTH, bp, N_MEM, N_MEM_HEADS, MEM_HEAD_DIM)
    memv_prompt = mem_kv[..., D_MODEL:].reshape(DEPTH, bp, N_MEM, N_MEM_HEADS, MEM_HEAD_DIM)
    n_ssm = (DEPTH + 1) // 2
    conv0 = jnp.zeros((n_ssm, bp, CONV_W - 1, CONV_DIM), x_prompt.dtype)
    ssm0 = jnp.zeros((n_ssm, bp, N_SSM_HEADS, SSM_HEAD_DIM, D_STATE), x_prompt.dtype)
    attend_p = lambda j, q, k, v, qi, ki, wi: dsa_prompt_attend(q, k, v, qi, ki, wi, tq=min(256, seq),
                                                                tk=min(512, seq))
    y_prompt, k_prompt, v_prompt, kidx_prompt, conv_prompt, ssm_prompt = run_trunk(
        x_prompt, mem_kv, mem_kv, conv0, ssm0, attend_p, pos_p, params, tm=min(1024, bp * seq), tm_mem=min(512, seq))
    past = page_table.shape[1] * PAGE_SIZE
    bs, ts = x_sample.shape[:2]
    pos_s = past + jnp.arange(ts, dtype=jnp.int32)
    attend_s = lambda j, q, k, v, qi, ki, wi: dsa_sample_attend(q, k, v, qi, ki, wi, cache_k, cache_v, cache_kidx, j,
                                                                page_table)
    y_sample, k_sample, v_sample, kidx_sample, conv_sample, ssm_sample = run_trunk(
        x_sample, cache_mem_k, cache_mem_v, state_conv, state_ssm, attend_s, pos_s, params, tm=bs * ts, tm_mem=bs * ts)
    return (y_prompt, y_sample, k_prompt, v_prompt, kidx_prompt, conv_prompt, ssm_prompt, memk_prompt, memv_prompt,
            k_sample, v_sample, kidx_sample, conv_sample, ssm_sample)
```
